```python
import jax, jax.numpy as jnp
from jax import lax
import numpy as np

D_MODEL = 2048
BATCH = 2
SEQ = 8192
DEPTH = 4

GRID_W = 64
CTX_LEN = 256
N_MIXERS = 3
N_A_LAYERS = (DEPTH + 2) // 3
N_B_LAYERS = (DEPTH + 1) // 3
N_C_LAYERS = DEPTH // 3
RMS_EPS = 1e-6

A_DK = 128
A_HEADS = D_MODEL // A_DK
A_DV = D_MODEL // A_HEADS
A_QK = A_HEADS * A_DK
A_WIDTH = A_HEADS * A_DV
A_IN_COLS = 3 * A_QK + 2 * A_WIDTH
A_CHUNK = 64

B_HEAD_DIM = 64
B_Q_HEADS = D_MODEL // B_HEAD_DIM
B_KV_HEADS = 4
B_WIDTH = B_Q_HEADS * B_HEAD_DIM
B_KV_WIDTH = B_KV_HEADS * B_HEAD_DIM
B_IN_COLS = 2 * B_WIDTH + 2 * B_KV_WIDTH
B_WINDOW = 128
B_BLOCK = 128
ROPE_BASE = 10000.0

C_HEAD = 64
C_HEADS = D_MODEL // C_HEAD
C_WIDTH = C_HEADS * C_HEAD
C_DECAY_LORA = 96
C_ICLR_LORA = 96
C_GN_EPS = 64e-5

kernel_name = "hybrid_hgrn2_swa_rwkv7_prefix_dit"


def _rmsnorm(x, w):
    xf = x.astype(jnp.float32)
    y = xf * lax.rsqrt(jnp.mean(xf * xf, axis=-1, keepdims=True) + RMS_EPS)
    return (y * w.astype(jnp.float32)).astype(x.dtype)


def _groupnorm(y, w, b):
    yf = y.astype(jnp.float32)
    mu = jnp.mean(yf, axis=-1, keepdims=True)
    var = jnp.mean(jnp.square(yf - mu), axis=-1, keepdims=True)
    z = ((yf - mu) * lax.rsqrt(var + C_GN_EPS)).reshape(y.shape[0], y.shape[1], -1)
    return (z * w.astype(jnp.float32) + b.astype(jnp.float32)).astype(y.dtype)


def _split_heads(t, n_heads, head_dim):
    return t.reshape(t.shape[0], t.shape[1], n_heads, head_dim)


def _context_then_latent(scan_fn, ctx_in, lat_in, s0, reverse):
    flip = (lambda t: jnp.flip(t, axis=1)) if reverse else (lambda t: t)
    y_ctx, s_ctx = scan_fn(*[flip(t) for t in ctx_in], s0)
    y_lat, _ = scan_fn(*[flip(t) for t in lat_in], s_ctx)
    return flip(y_ctx), flip(y_lat)


def _gla_chunk_scan(q, k, v, g, s0):
    B, T, H, _ = q.shape
    DV = v.shape[-1]
    n = T // A_CHUNK

    def chunks(t):
        return t.astype(jnp.float32).reshape(B, n, A_CHUNK, H, t.shape[-1]).transpose(1, 0, 3, 2, 4)

    incl = jnp.tril(jnp.ones((A_CHUNK, A_CHUNK), bool))[:, :, None]

    def step(S, inp):
        qc, kc, vc, gc = inp
        bcum = jnp.cumsum(gc, axis=2)
        pair = jnp.exp(jnp.where(incl, bcum[:, :, :, None, :] - bcum[:, :, None, :, :], -jnp.inf))
        att = jnp.einsum('bhtk,bhtsk,bhsk->bhts', qc, pair, kc)
        o = jnp.einsum('bhts,bhsv->bhtv', att, vc) + jnp.einsum('bhtk,bhkv->bhtv', qc * jnp.exp(bcum), S)
        b_end = bcum[:, :, -1:, :]
        S = jnp.exp(b_end[:, :, 0, :])[..., None] * S + jnp.einsum('bhsk,bhsv->bhkv', kc * jnp.exp(b_end - bcum), vc)
        return S, o

    S, o = lax.scan(step, s0, (chunks(q), chunks(k), chunks(v), chunks(g)))
    o = o.transpose(1, 0, 3, 2, 4).reshape(B, T, H, DV)
    return o.astype(v.dtype), S


def _hgrn2_project(h, w_in, lb):
    q, i, zf, zb, gate = jnp.split(h @ w_in, [A_QK, A_QK + A_WIDTH, 2 * A_QK + A_WIDTH, 3 * A_QK + A_WIDTH], axis=-1)
    q = _split_heads(q, A_HEADS, A_DK) * (A_DK ** -0.5)
    v = _split_heads(i, A_HEADS, A_DV)
    lb_h = lb.astype(jnp.float32).reshape(A_HEADS, A_DK)

    def forget(z):
        z = _split_heads(z, A_HEADS, A_DK).astype(jnp.float32)
        logf = jnp.logaddexp(jnp.log(lb_h), jnp.log1p(-lb_h) + jax.nn.log_sigmoid(z))
        return -jnp.expm1(logf), logf

    return q, v, forget(zf), forget(zb), gate


def _hgrn2_mixer(h_lat, h_ctx, w_in, lb, onorm_w, w_out, need_ctx):
    ql, vl, (kfl, gfl), (kbl, gbl), gate_l = _hgrn2_project(h_lat, w_in, lb)
    qc, vc, (kfc, gfc), (kbc, gbc), gate_c = _hgrn2_project(h_ctx, w_in, lb)
    s0 = jnp.zeros((h_lat.shape[0], A_HEADS, A_DK, A_DV), jnp.float32)
    yc_f, yl_f = _context_then_latent(_gla_chunk_scan, (qc, kfc, vc, gfc), (ql, kfl, vl, gfl), s0, False)
    yc_b, yl_b = _context_then_latent(_gla_chunk_scan, (qc, kbc, vc, gbc), (ql, kbl, vl, gbl), s0, True)

    def readout(y, gate):
        y = _rmsnorm(y, onorm_w).reshape(y.shape[0], y.shape[1], A_WIDTH)
        return (y * jax.nn.silu(gate)) @ w_out

    y_lat = readout(yl_f + yl_b, gate_l)
    y_ctx = readout(yc_f + yc_b, gate_c) if need_ctx else None
    return y_lat, y_ctx


def _axial_rope(t, row, col):
    quarter = t.shape[-1] // 4
    inv = ROPE_BASE ** (-jnp.arange(quarter, dtype=jnp.float32) / quarter)

    def rot(u, pos):
        ang = pos[:, None] * inv[None, :]
        cos = jnp.cos(ang)[None, :, None, :]
        sin = jnp.sin(ang)[None, :, None, :]
        u1, u2 = u[..., :quarter], u[..., quarter:]
        return jnp.concatenate([u1 * cos - u2 * sin, u2 * cos + u1 * sin], axis=-1)

    tf = t.astype(jnp.float32)
    half = 2 * quarter
    return jnp.concatenate([rot(tf[..., :half], row), rot(tf[..., half:], col)], axis=-1).astype(t.dtype)


def _sink_softmax_attend(q, keys, values, masks, sink):
    bsz, nq, G, R, hd = q.shape
    scale = hd ** -0.5
    logits = []
    for kk, m in zip(keys, masks):
        s = jnp.einsum('bqgrd,bkgd->bgrqk', q, kk).astype(jnp.float32) * scale
        logits.append(s if m is None else jnp.where(m, s, -jnp.inf))
    logits.append(jnp.broadcast_to(sink.astype(jnp.float32)[None, :, :, None, None], (bsz, G, R, nq, 1)))
    p = jax.nn.softmax(jnp.concatenate(logits, axis=-1), axis=-1)
    out = None
    start = 0
    for vv in values:
        n = vv.shape[1]
        o = jnp.einsum('bgrqk,bkgd->bqgrd', p[..., start:start + n].astype(vv.dtype), vv)
        out = o if out is None else out + o
        start += n
    return out


def _swa_latent(q, k, v, k_ctx, v_ctx, sink):
    bsz, T, G, R, hd = q.shape
    nblk = T // B_BLOCK
    span = B_BLOCK + 2 * B_WINDOW
    kp = jnp.pad(k, ((0, 0), (B_WINDOW, B_WINDOW), (0, 0), (0, 0)))
    vp = jnp.pad(v, ((0, 0), (B_WINDOW, B_WINDOW), (0, 0), (0, 0)))
    qb = q.reshape(bsz, nblk, B_BLOCK, G, R, hd)
    q_off = jnp.arange(B_BLOCK)
    k_off = jnp.arange(span) - B_WINDOW
    band = jnp.abs(k_off[None, :] - q_off[:, None]) <= B_WINDOW

    def block(j):
        start = j * B_BLOCK
        kpos = start + k_off
        valid = band & ((kpos >= 0) & (kpos < T))[None, :]
        qj = lax.dynamic_index_in_dim(qb, j, axis=1, keepdims=False)
        kj = lax.dynamic_slice_in_dim(kp, start, span, axis=1)
        vj = lax.dynamic_slice_in_dim(vp, start, span, axis=1)
        return _sink_softmax_attend(qj, [kj, k_ctx], [vj, v_ctx], [valid, None], sink)

    out = lax.map(block, jnp.arange(nblk))
    return jnp.moveaxis(out, 0, 1).reshape(bsz, T, G * R * hd)


def _swa_project(h, w_in):
    q, k, v, gate = jnp.split(h @ w_in, [B_WIDTH, B_WIDTH + B_KV_WIDTH, B_WIDTH + 2 * B_KV_WIDTH], axis=-1)
    return (_split_heads(q, B_Q_HEADS, B_HEAD_DIM), _split_heads(k, B_KV_HEADS, B_HEAD_DIM),
            _split_heads(v, B_KV_HEADS, B_HEAD_DIM), gate)


def _swa_mixer(h_lat, h_ctx, w_in, sink, w_out, row, col, need_ctx):
    bsz, T, _ = h_lat.shape
    G, R = B_KV_HEADS, B_Q_HEADS // B_KV_HEADS
    ql, kl, vl, gate_l = _swa_project(h_lat, w_in)
    qc, kc, vc, gate_c = _swa_project(h_ctx, w_in)
    ql = _axial_rope(ql, row, col)
    kl = _axial_rope(kl, row, col)
    sink_gr = sink.reshape(G, R)
    o_lat = _swa_latent(ql.reshape(bsz, T, G, R, B_HEAD_DIM), kl, vl, kc, vc, sink_gr)
    y_lat = (o_lat * jax.nn.silu(gate_l)) @ w_out
    y_ctx = None
    if need_ctx:
        L = h_ctx.shape[1]
        o_ctx = _sink_softmax_attend(qc.reshape(bsz, L, G, R, B_HEAD_DIM), [kc], [vc], [None], sink_gr)
        y_ctx = (o_ctx.reshape(bsz, L, B_WIDTH) * jax.nn.silu(gate_c)) @ w_out
    return y_lat, y_ctx


def _centred_shift(x):
    xp = jnp.pad(x, ((0, 0), (1, 1), (0, 0)))
    return 0.5 * (xp[:, :-2] + xp[:, 2:])


def _rwkv7_scan(r, w, k, v, a, b, s0):
    def step(S, inp):
        rt, wt, kt, vt, at, bt = inp
        sa = jnp.einsum('bhvk,bhk->bhv', S, at)
        S = S * wt[:, :, None, :] + sa[..., None] * bt[:, :, None, :] + vt[..., None] * kt[:, :, None, :]
        return S, jnp.einsum('bhvk,bhk->bhv', S, rt)

    xs = tuple(jnp.moveaxis(t.astype(jnp.float32), 1, 0) for t in (r, w, k, v, a, b))
    S, y = lax.scan(step, s0, xs)
    return jnp.moveaxis(y, 0, 1).astype(v.dtype), S


def _rwkv7_shared(h, mu, w_in, k_k):
    xx = _centred_shift(h) - h
    lerp = lambda n: h + xx * mu[n]
    r = lerp(0) @ w_in[0]
    k = lerp(1) @ w_in[1]
    v = lerp(2) @ w_in[2]
    gate = lerp(3) @ w_in[3]
    kk = _split_heads(k * k_k, C_HEADS, C_HEAD).astype(jnp.float32)
    kk = (kk / jnp.maximum(jnp.sqrt(jnp.sum(kk * kk, axis=-1, keepdims=True)), 1e-12)).astype(k.dtype)
    return lerp(4), lerp(5), r, k, v, kk, gate


def _rwkv7_direction(shared, w0, w1, w2, a0, a1, a2, k_a):
    xw, xa, r, k, v, kk, _ = shared
    w_log = -jax.nn.softplus(-(w0 + jnp.tanh(xw @ w1) @ w2)) - 0.5
    decay = jnp.exp(-jnp.exp(w_log.astype(jnp.float32)))
    iclr = jax.nn.sigmoid(a0 + (xa @ a1) @ a2)
    k_d = k * (1.0 + (iclr - 1.0) * k_a)
    hs = lambda t: _split_heads(t, C_HEADS, C_HEAD)
    return (hs(r), hs(decay), hs(k_d), hs(v), -kk, kk * hs(iclr))


def _rwkv7_bonus(scan_in, r_k):
    rh, _, kh, vh, _, _ = scan_in
    return jnp.sum(rh * kh * r_k, axis=-1, keepdims=True) * vh


def _rwkv7_mixer(h_lat, h_ctx, mu, w_in, w0, w1, w2, a0, a1, a2, k_k, k_a, r_k, ln_w, ln_b, w_out, need_ctx):
    sh_lat = _rwkv7_shared(h_lat, mu, w_in, k_k)
    sh_ctx = _rwkv7_shared(h_ctx, mu, w_in, k_k)
    s0 = jnp.zeros((h_lat.shape[0], C_HEADS, C_HEAD, C_HEAD), jnp.float32)
    y_l, bon_l, y_c, bon_c = [], [], [], []
    for d in range(2):
        in_lat = _rwkv7_direction(sh_lat, w0[d], w1[d], w2[d], a0[d], a1[d], a2[d], k_a)
        in_ctx = _rwkv7_direction(sh_ctx, w0[d], w1[d], w2[d], a0[d], a1[d], a2[d], k_a)
        yc, yl = _context_then_latent(_rwkv7_scan, in_ctx, in_lat, s0, d == 1)
        y_l.append(yl)
        bon_l.append(_rwkv7_bonus(in_lat, r_k))
        if need_ctx:
            y_c.append(yc)
            bon_c.append(_rwkv7_bonus(in_ctx, r_k))

    def readout(y, bonus, gate):
        o = _groupnorm(y, ln_w, ln_b) + bonus.reshape(bonus.shape[0], bonus.shape[1], C_WIDTH)
        return (o * jax.nn.silu(gate)) @ w_out

    y_lat = readout(y_l[0] + y_l[1], bon_l[0] + bon_l[1], sh_lat[-1])
    y_ctx = readout(y_c[0] + y_c[1], bon_c[0] + bon_c[1], sh_ctx[-1]) if need_ctx else None
    return y_lat, y_ctx


def setup_inputs(seed: int = 0) -> dict:
    key = jax.random.key(seed)
    ks = iter(jax.random.split(key, 32))
    nrm = lambda shape, s: jax.random.normal(next(ks), shape, jnp.float32) * s
    D = D_MODEL
    return {
        "x": nrm((BATCH, SEQ, D), 1.0),
        "c": nrm((BATCH, D), 1.0),
        "ctx": nrm((BATCH, CTX_LEN, D), 1.0),
        "c_ctx": nrm((D,), 1.0),
        "norm_w": 1.0 + nrm((DEPTH, D), 0.02),
        "mod_w": nrm((DEPTH, D, 3 * D), 0.5 * D ** -0.5),
        "mod_b": nrm((DEPTH, 3 * D), 0.02),
        "a_w_in": nrm((N_A_LAYERS, D, A_IN_COLS), D ** -0.5),
        "a_lb_raw": nrm((N_A_LAYERS, A_QK), 0.5),
        "a_onorm_w": 1.0 + nrm((N_A_LAYERS, A_DV), 0.02),
        "a_w_out": nrm((N_A_LAYERS, A_WIDTH, D), A_WIDTH ** -0.5),
        "b_w_in": nrm((N_B_LAYERS, D, B_IN_COLS), D ** -0.5),
        "b_sink": nrm((N_B_LAYERS, B_Q_HEADS), 0.5),
        "b_w_out": nrm((N_B_LAYERS, B_WIDTH, D), B_WIDTH ** -0.5),
        "c_mu": jax.random.uniform(next(ks), (N_C_LAYERS, 6, D), jnp.float32),
        "c_w_in": nrm((N_C_LAYERS, 4, D, C_WIDTH), D ** -0.5),
        "c_w0": -1.0 + nrm((N_C_LAYERS, 2, C_WIDTH), 0.5),
        "c_w1": nrm((N_C_LAYERS, 2, D, C_DECAY_LORA), D ** -0.5),
        "c_w2": nrm((N_C_LAYERS, 2, C_DECAY_LORA, C_WIDTH), 0.1 * C_DECAY_LORA ** -0.5),
        "c_a0": nrm((N_C_LAYERS, 2, C_WIDTH), 0.1),
        "c_a1": nrm((N_C_LAYERS, 2, D, C_ICLR_LORA), D ** -0.5),
        "c_a2": nrm((N_C_LAYERS, 2, C_ICLR_LORA, C_WIDTH), 0.5 * C_ICLR_LORA ** -0.5),
        "c_k_k": 0.85 + nrm((N_C_LAYERS, C_WIDTH), 0.02),
        "c_k_a": 1.0 + nrm((N_C_LAYERS, C_WIDTH), 0.02),
        "c_r_k": nrm((N_C_LAYERS, C_HEADS, C_HEAD), 0.1),
        "c_ln_w": 1.0 + nrm((N_C_LAYERS, C_WIDTH), 0.02),
        "c_ln_b": nrm((N_C_LAYERS, C_WIDTH), 0.02),
        "c_w_out": nrm((N_C_LAYERS, C_WIDTH, D), C_WIDTH ** -0.5),
        "final_norm_w": 1.0 + nrm((D,), 0.02),
    }


def reference(x, c, ctx, c_ctx, norm_w, mod_w, mod_b, a_w_in, a_lb_raw, a_onorm_w, a_w_out,
              b_w_in, b_sink, b_w_out, c_mu, c_w_in, c_w0, c_w1, c_w2, c_a0, c_a1, c_a2,
              c_k_k, c_k_a, c_r_k, c_ln_w, c_ln_b, c_w_out, final_norm_w):
    n_lat = x.shape[1]
    n_rows = n_lat // GRID_W
    row = jnp.repeat(jnp.arange(n_rows), GRID_W).astype(jnp.float32)
    col = jnp.tile(jnp.arange(GRID_W), n_rows).astype(jnp.float32)
    lb_all = jnp.cumsum(jax.nn.softmax(a_lb_raw.astype(jnp.float32), axis=0), axis=0)
    lb_all = lb_all - lb_all[0]
    for i in range(DEPTH):
        j = i // N_MIXERS
        kind = i % N_MIXERS
        need_ctx = i < DEPTH - 1
        mod_l = jax.nn.silu(c) @ mod_w[i] + mod_b[i]
        mod_c = jax.nn.silu(c_ctx) @ mod_w[i] + mod_b[i]
        sh_l, sc_l, g_l = jnp.split(mod_l[:, None, :], 3, axis=-1)
        sh_c, sc_c, g_c = jnp.split(mod_c[None, None, :], 3, axis=-1)
        h_lat = _rmsnorm(x, norm_w[i]) * (1.0 + sc_l) + sh_l
        h_ctx = _rmsnorm(ctx, norm_w[i]) * (1.0 + sc_c) + sh_c
        if kind == 0:
            y_lat, y_ctx = _hgrn2_mixer(h_lat, h_ctx, a_w_in[j], lb_all[j], a_onorm_w[j], a_w_out[j], need_ctx)
        elif kind == 1:
            y_lat, y_ctx = _swa_mixer(h_lat, h_ctx, b_w_in[j], b_sink[j], b_w_out[j], row, col, need_ctx)
        else:
            y_lat, y_ctx = _rwkv7_mixer(h_lat, h_ctx, c_mu[j], c_w_in[j], c_w0[j], c_w1[j], c_w2[j],
                                        c_a0[j], c_a1[j], c_a2[j], c_k_k[j], c_k_a[j], c_r_k[j],
                                        c_ln_w[j], c_ln_b[j], c_w_out[j], need_ctx)
        x = x + g_l * y_lat
        if need_ctx:
            ctx = ctx + g_c * y_ctx
    return _rmsnorm(x, final_norm_w)
```

```python
import functools
import math

import jax
import jax.numpy as jnp
from jax import lax
from jax.experimental import pallas as pl
from jax.experimental.pallas import tpu as pltpu

F32 = jnp.float32
BF16 = jnp.bfloat16
NEG_INF = float("-inf")

RMS_EPS = 1e-6
CHUNK = 64
SUB = 16
NSUB = CHUNK // SUB
A_DK = 128
VMEM_LIMIT = 56 * 1024 * 1024


def _cparams(sem):
    return pltpu.CompilerParams(dimension_semantics=sem, vmem_limit_bytes=VMEM_LIMIT)


def _tile(n, target, mult=8):
    best = None
    for t in range(mult, min(n, target) + 1, mult):
        if n % t == 0:
            best = t
    assert best is not None, (n, target, mult)
    return best


def _dot(a, b, ca=1, cb=0):
    return lax.dot_general(a.astype(BF16), b.astype(BF16), (((ca,), (cb,)), ((), ())),
                           preferred_element_type=F32)


def _split2(x):
    hi = x.astype(BF16)
    lo = (x - hi.astype(F32)).astype(BF16)
    return hi, lo


def _split3(x):
    hi = x.astype(BF16)
    r = x - hi.astype(F32)
    mid = r.astype(BF16)
    lo = (r - mid.astype(F32)).astype(BF16)
    return hi, mid, lo


def _dot_sel(sel, x):
    hi, mid, lo = _split3(x)
    return _dot(sel, hi) + _dot(sel, mid) + _dot(sel, lo)


def _dot_hi(a, b, ca=1, cb=0):
    ah, al = _split2(a)
    bh, bl = _split2(b)
    return _dot(ah, bh, ca, cb) + _dot(ah, bl, ca, cb) + _dot(al, bh, ca, cb)


def _sigmoid(x):
    return 1.0 / (1.0 + jnp.exp(-x))


def _silu(x):
    return x * _sigmoid(x)


def _mod_kernel(c_ref, w_ref, b_ref, o_ref):
    s = _silu(c_ref[...])
    o_ref[0] = _dot_hi(s, w_ref[0]) + b_ref[0]


def _modulation(cvec, mod_w, mod_b):
    depth, d, n = mod_w.shape
    tn = _tile(n, 512, 128)
    return pl.pallas_call(
        _mod_kernel,
        grid=(depth, n // tn),
        in_specs=[pl.BlockSpec((8, d), lambda i, j: (0, 0)),
                  pl.BlockSpec((1, d, tn), lambda i, j: (i, 0, j)),
                  pl.BlockSpec((1, 1, tn), lambda i, j: (i, 0, j))],
        out_specs=pl.BlockSpec((1, 8, tn), lambda i, j: (i, 0, j)),
        out_shape=jax.ShapeDtypeStruct((depth, 8, n), F32),
        compiler_params=_cparams(("parallel", "parallel")),
        name="modulation",
    )(cvec, mod_w, mod_b.reshape(depth, 1, n))


def _normmod(x, nw, is_ctx, scl, shl, scc, shc):
    ms = jnp.mean(x * x, axis=-1, keepdims=True)
    y = x * lax.rsqrt(ms + RMS_EPS) * nw
    sc = jnp.where(is_ctx, scc, scl)
    sh = jnp.where(is_ctx, shc, shl)
    return y * (1.0 + sc) + sh


def _prep_kernel(x_ref, nw_ref, shl_ref, scl_ref, shc_ref, scc_ref, o_ref, *, tm, ctx):
    row = pl.program_id(1) * tm + lax.broadcasted_iota(jnp.int32, (tm, 1), 0)
    h = _normmod(x_ref[0], nw_ref[...], row < ctx, scl_ref[0], shl_ref[0], scc_ref[0], shc_ref[0])
    o_ref[0] = h.astype(o_ref.dtype)


def _mod_specs(d, nb):
    return [pl.BlockSpec((1, 1, d), lambda b, i, *_: (b, 0, 0)),
            pl.BlockSpec((1, 1, d), lambda b, i, *_: (b, 0, 1)),
            pl.BlockSpec((1, 1, d), lambda b, i, *_: (nb, 0, 0)),
            pl.BlockSpec((1, 1, d), lambda b, i, *_: (nb, 0, 1))]


def _prep(xs, nw, mod, ctx):
    nb, t, d = xs.shape
    tm = _tile(t, 1056)
    return pl.pallas_call(
        functools.partial(_prep_kernel, tm=tm, ctx=ctx),
        grid=(nb, t // tm),
        in_specs=[pl.BlockSpec((1, tm, d), lambda b, i: (b, i, 0)),
                  pl.BlockSpec((1, d), lambda b, i: (0, 0))] + _mod_specs(d, nb),
        out_specs=pl.BlockSpec((1, tm, d), lambda b, i: (b, i, 0)),
        out_shape=jax.ShapeDtypeStruct((nb, t, d), BF16),
        compiler_params=_cparams(("parallel", "parallel")),
        name="prep",
    )(xs, nw.reshape(1, d), mod, mod, mod, mod)


def _mm_kernel(a_ref, w_ref, o_ref):
    o_ref[0] = _dot(a_ref[0], w_ref[...]).astype(o_ref.dtype)


def _mm_sel_kernel(a_ref, w_ref, o_ref):
    o_ref[0] = _dot(a_ref[0, 0], w_ref[...]).astype(o_ref.dtype)


def _matmul(a, w, out_dtype=F32, sel=None):
    nb, t, k = a.shape[-3:]
    n = w.shape[1]
    tm = _tile(t, 1056)
    tn = _tile(n, 512, 128)
    if sel is None:
        a_spec = pl.BlockSpec((1, tm, k), lambda b, i, j: (b, i, 0))
    else:
        a_spec = pl.BlockSpec((1, 1, tm, k), lambda b, i, j: (sel, b, i, 0))
    return pl.pallas_call(
        _mm_kernel if sel is None else _mm_sel_kernel,
        grid=(nb, t // tm, n // tn),
        in_specs=[a_spec,
                  pl.BlockSpec((k, tn), lambda b, i, j: (0, j))],
        out_specs=pl.BlockSpec((1, tm, tn), lambda b, i, j: (b, i, j)),
        out_shape=jax.ShapeDtypeStruct((nb, t, n), out_dtype),
        compiler_params=_cparams(("parallel", "parallel", "arbitrary")),
        name="matmul",
    )(a, w)


def _mm_res_kernel(a_ref, w_ref, x_ref, gl_ref, gc_ref, o_ref, *, tm, ctx):
    row = pl.program_id(1) * tm + lax.broadcasted_iota(jnp.int32, (tm, 1), 0)
    g = jnp.where(row < ctx, gc_ref[0], gl_ref[0])
    o_ref[0] = x_ref[0] + g * _dot(a_ref[0], w_ref[...])


def _matmul_residual(a, w, xs, mod, ctx):
    nb, t, k = a.shape
    n = w.shape[1]
    tm = _tile(t, 1056)
    tn = _tile(n, 512, 128)
    goff = 2 * n // tn
    return pl.pallas_call(
        functools.partial(_mm_res_kernel, tm=tm, ctx=ctx),
        grid=(nb, t // tm, n // tn),
        in_specs=[pl.BlockSpec((1, tm, k), lambda b, i, j: (b, i, 0)),
                  pl.BlockSpec((k, tn), lambda b, i, j: (0, j)),
                  pl.BlockSpec((1, tm, tn), lambda b, i, j: (b, i, j)),
                  pl.BlockSpec((1, 1, tn), lambda b, i, j: (b, 0, goff + j)),
                  pl.BlockSpec((1, 1, tn), lambda b, i, j: (nb, 0, goff + j))],
        out_specs=pl.BlockSpec((1, tm, tn), lambda b, i, j: (b, i, j)),
        out_shape=jax.ShapeDtypeStruct((nb, t, n), F32),
        compiler_params=_cparams(("parallel", "parallel", "arbitrary")),
        name="matmul_residual",
    )(a, w, xs, mod, mod)


def _time_block_map(nblk, nctx_blk, reverse):
    if not reverse:
        return lambda i: i
    return lambda i: jnp.where(i < nctx_blk, nctx_blk - 1 - i, nblk - 1 - i + nctx_blk)


def _gla_chunk(q, v, z, llb, l1m, st, reverse):
    ls = -(jnp.maximum(-z, 0.0) + jnp.log1p(jnp.exp(-jnp.abs(z))))
    c2 = l1m + ls
    g = jnp.maximum(llb, c2) + jnp.log1p(jnp.exp(-jnp.abs(llb - c2)))
    k = 1.0 - jnp.exp(g)

    ti = lax.broadcasted_iota(jnp.int32, (CHUNK, CHUNK), 0)
    si = lax.broadcasted_iota(jnp.int32, (CHUNK, CHUNK), 1)
    incl = (si >= ti) if reverse else (si <= ti)
    b = _dot_sel(incl.astype(BF16), g)

    end = 0 if reverse else CHUNK - 1
    b_end = b[end:end + 1, :]
    o = _dot(q * jnp.exp(b), st, 1, 1)
    kbar = k * jnp.exp(b_end - b)
    st_new = st * jnp.exp(b_end) + _dot(v, kbar, 0, 0)

    row = lax.broadcasted_iota(jnp.int32, (CHUNK, 1), 0)
    ends = [(SUB * j if reverse else SUB * j + SUB - 1) for j in range(NSUB)]
    b_e = jnp.concatenate([jnp.broadcast_to(b[e:e + 1, :], (SUB, A_DK)) for e in ends], axis=0)
    ktil = k * jnp.exp(b_e - b)
    srcs = list(range(1, NSUB)) if reverse else list(range(NSUB - 1))
    qst = []
    for j in srcs:
        after = (row < SUB * j) if reverse else (row >= SUB * (j + 1))
        qst.append(q * jnp.exp(jnp.where(after, b - b[ends[j]:ends[j] + 1, :], NEG_INF)))
    p = _dot(jnp.concatenate(qst, axis=0), ktil, 1, 1)
    colblk = lax.broadcasted_iota(jnp.int32, (CHUNK, CHUNK), 1) // SUB
    a_off = jnp.zeros((CHUNK, CHUNK), F32)
    for n, j in enumerate(srcs):
        a_off = a_off + jnp.where(colblk == j, p[n * CHUNK:(n + 1) * CHUNK, :], 0.0)
    o = o + _dot(a_off, v)

    t16 = lax.broadcasted_iota(jnp.int32, (SUB, 1), 0)
    ones = jnp.ones((A_DK, A_DK), BF16)
    gi = lax.broadcasted_iota(jnp.int32, (SUB, SUB * SUB), 0)
    gj = lax.broadcasted_iota(jnp.int32, (SUB, SUB * SUB), 1) // SUB
    sel = (gi == gj).astype(BF16)
    o_diag = []
    for j in range(NSUB):
        sl = slice(SUB * j, SUB * (j + 1))
        bj, kj, qj, vj = b[sl], k[sl], q[sl], v[sl]
        slabs = []
        for t in range(SUB):
            ok = (t16 >= t) if reverse else (t16 <= t)
            w = jnp.exp(jnp.where(ok, bj[t:t + 1, :] - bj, NEG_INF))
            slabs.append(w * kj * qj[t:t + 1, :])
        r = _dot(jnp.concatenate(slabs, axis=0), ones)
        vt = jnp.concatenate([vj] * SUB, axis=0)
        o_diag.append(_dot(sel, r * vt))
    o = o + jnp.concatenate(o_diag, axis=0)
    return o, st_new


def _gla_kernel(q_ref, v_ref, z_ref, llb_ref, l1m_ref, o_ref, st_ref, *, tb, reverse, scale):
    @pl.when(pl.program_id(2) == 0)
    def _():
        st_ref[...] = jnp.zeros_like(st_ref)

    nch = tb // CHUNK
    for c in (range(nch - 1, -1, -1) if reverse else range(nch)):
        sl = slice(c * CHUNK, (c + 1) * CHUNK)
        o, st = _gla_chunk(q_ref[0, sl, :] * scale, v_ref[0, sl, :], z_ref[0, sl, :],
                           llb_ref[...], l1m_ref[...], st_ref[...], reverse)
        o_ref[0, sl, :] = o
        st_ref[...] = st


def _gla(z, llb, l1m, ctx, reverse, tb=None):
    nb, t, _ = z.shape
    nh = llb.shape[1] // A_DK
    tb = tb or _tile(ctx, 256, CHUNK)
    assert t % tb == 0 and ctx % tb == 0
    nblk, ncb = t // tb, ctx // tb
    tmap = _time_block_map(nblk, ncb, reverse)
    zcol = (3 if reverse else 2) * nh

    def spec(col0):
        return pl.BlockSpec((1, tb, A_DK), lambda b, h, i: (b, tmap(i), col0 + h))

    return pl.pallas_call(
        functools.partial(_gla_kernel, tb=tb, reverse=reverse, scale=A_DK ** -0.5),
        grid=(nb, nh, nblk),
        in_specs=[spec(0), spec(nh), spec(zcol),
                  pl.BlockSpec((1, A_DK), lambda b, h, i: (0, h)),
                  pl.BlockSpec((1, A_DK), lambda b, h, i: (0, h))],
        out_specs=pl.BlockSpec((1, tb, A_DK), lambda b, h, i: (b, tmap(i), h)),
        out_shape=jax.ShapeDtypeStruct((nb, t, nh * A_DK), F32),
        scratch_shapes=[pltpu.VMEM((A_DK, A_DK), F32)],
        compiler_params=_cparams(("parallel", "parallel", "arbitrary")),
        name="gla_bwd" if reverse else "gla_fwd",
    )(z, z, z, llb, l1m)


def _hgrn2_out_kernel(of_ref, ob_ref, g_ref, w_ref, o_ref):
    y = of_ref[0] + ob_ref[0]
    ms = jnp.mean(y * y, axis=-1, keepdims=True)
    yn = y * lax.rsqrt(ms + RMS_EPS) * w_ref[...]
    o_ref[0] = (yn * _silu(g_ref[0])).astype(o_ref.dtype)


def _hgrn2_out(o_f, o_b, z, onorm_w):
    nb, t, w = o_f.shape
    nh = w // A_DK
    tm = _tile(t, 1056)
    blk = lambda col0: pl.BlockSpec((1, tm, A_DK), lambda b, i, h: (b, i, col0 + h))
    return pl.pallas_call(
        _hgrn2_out_kernel,
        grid=(nb, t // tm, nh),
        in_specs=[blk(0), blk(0), blk(4 * nh), pl.BlockSpec((1, A_DK), lambda b, i, h: (0, 0))],
        out_specs=blk(0),
        out_shape=jax.ShapeDtypeStruct((nb, t, w), BF16),
        compiler_params=_cparams(("parallel", "parallel", "arbitrary")),
        name="hgrn2_out",
    )(o_f, o_b, z, onorm_w.reshape(1, A_DK))


def _hgrn2_mix(h, w_in, lb, onorm_w, ctx, tb=None):
    z = _matmul(h, w_in)
    llb = jnp.log(lb).reshape(1, -1)
    l1m = jnp.log1p(-lb).reshape(1, -1)
    o_f = _gla(z, llb, l1m, ctx, False, tb)
    o_b = _gla(z, llb, l1m, ctx, True, tb)
    return _hgrn2_out(o_f, o_b, z, onorm_w)


GRID_W = 64
ROPE_BASE = 10000.0
HEAD = 64
QB = 128
LANES = 128


def _rope_tables(t_all, ctx):
    quarter = HEAD // 4
    inv = ROPE_BASE ** (-jnp.arange(quarter, dtype=F32) / quarter)
    tl = jnp.arange(t_all - ctx)
    row = (tl // GRID_W).astype(F32)
    col = (tl % GRID_W).astype(F32)
    hdim = jnp.arange(LANES) % HEAD
    use_col = hdim >= 2 * quarter
    second = (hdim % (2 * quarter)) >= quarter
    pos = jnp.where(use_col[None, :], col[:, None], row[:, None])
    ang = pos * inv[hdim % quarter][None, :]
    cos = jnp.concatenate([jnp.ones((ctx, LANES), F32), jnp.cos(ang)], axis=0)
    sin = jnp.concatenate([jnp.zeros((ctx, LANES), F32),
                           jnp.where(second[None, :], jnp.sin(ang), -jnp.sin(ang))], axis=0)
    return cos, sin


def _rope(x, cos, sin, first):
    partner = jnp.where(first, pltpu.roll(x, LANES - HEAD // 4, 1), pltpu.roll(x, HEAD // 4, 1))
    return x * cos + partner * sin


def _swa_kernel(sink_ref, q_ref, g_ref, kc_ref, vc_ref, kp_ref, kq_ref, kn_ref, vp_ref, vq_ref, vn_ref,
                cq_ref, sq_ref, cp_ref, sp_ref, cn_ref, sn_ref, o_ref, *, ncb, seq, ngroups):
    i = pl.program_id(1)
    lane = lax.broadcasted_iota(jnp.int32, (1, LANES), 1)
    first = (lane % (HEAD // 2)) < (HEAD // 4)
    left = lane < HEAD
    ntile = kc_ref.shape[2] // LANES

    def tile(x, c):
        return x[:, c * LANES:(c + 1) * LANES]

    kwin = [(kp_ref[0], cp_ref[...], sp_ref[...]), (kq_ref[0], cq_ref[...], sq_ref[...]),
            (kn_ref[0], cn_ref[...], sn_ref[...])]
    kt = [jnp.concatenate([tile(kc_ref[0], c)] + [_rope(tile(kk, c), cs, sn, first) for kk, cs, sn in kwin], axis=0)
          for c in range(ntile)]
    vt = [jnp.concatenate([tile(vc_ref[0], c), tile(vp_ref[0], c), tile(vq_ref[0], c), tile(vn_ref[0], c)], axis=0)
          for c in range(ntile)]
    nk = kt[0].shape[0]
    nctx = nk - 3 * QB

    r = lax.broadcasted_iota(jnp.int32, (QB, QB), 0)
    cidx = lax.broadcasted_iota(jnp.int32, (QB, QB), 1)
    qblk = i - ncb
    open_if = lambda cond: jnp.where(cond, 0.0, NEG_INF)
    b_prev = open_if(cidx >= r) + open_if(qblk >= 1)
    b_cur = jnp.zeros((QB, QB), F32) + open_if(qblk >= 0)
    b_next = open_if(cidx <= r) + open_if(jnp.logical_and(qblk >= 0, (qblk + 2) * QB <= seq))
    bias = jnp.concatenate([jnp.zeros((QB, nctx), F32), b_prev, b_cur, b_next], axis=1)

    scale = HEAD ** -0.5
    heads_per_group = q_ref.shape[2] // HEAD // ngroups
    pairs = heads_per_group // 2
    for g in range(ngroups):
        c, even = g // 2, g % 2 == 0
        own = left if even else jnp.logical_not(left)
        k_own = jnp.where(own, kt[c], 0.0)
        v_own = jnp.where(own, vt[c], 0.0)
        k_swp = pltpu.roll(k_own, HEAD, 1)
        v_swp = pltpu.roll(v_own, HEAD, 1)
        kk2 = jnp.concatenate([k_own, k_swp] if even else [k_swp, k_own], axis=0).astype(BF16)
        vv2 = jnp.concatenate([v_own, v_swp] if even else [v_swp, v_own], axis=0).astype(BF16)
        q8 = jnp.concatenate(
            [_rope(tile(q_ref[0], g * pairs + p), cq_ref[...], sq_ref[...], first) * scale for p in range(pairs)],
            axis=0)
        s8 = _dot(q8, kk2, 1, 1)
        probs, dens = [], []
        for p in range(pairs):
            pe2, den2 = [], []
            for e in range(2):
                sk = sink_ref[g * heads_per_group + 2 * p + e]
                s = s8[p * QB:(p + 1) * QB, e * nk:(e + 1) * nk] + bias
                m = jnp.maximum(jnp.max(s, axis=-1, keepdims=True), sk)
                pe = jnp.exp(s - m)
                den2.append(jnp.sum(pe, axis=-1, keepdims=True) + jnp.exp(sk - m))
                pe2.append(pe.astype(BF16))
            probs.append(jnp.concatenate(pe2, axis=1))
            dens.append(jnp.where(left, den2[0], den2[1]))
        o8 = _dot(jnp.concatenate(probs, axis=0), vv2)
        for p in range(pairs):
            col = g * pairs + p
            o = o8[p * QB:(p + 1) * QB] / dens[p]
            o_ref[0, :, col * LANES:(col + 1) * LANES] = (o * _silu(tile(g_ref[0], col))).astype(o_ref.dtype)


def _swa(z, sink, ctx, nkv):
    nb, t, ncol = z.shape
    kvw = nkv * HEAD
    w = (ncol - 2 * kvw) // 2
    assert t % QB == 0 and ctx % QB == 0 and w % kvw == 0
    nblk, ncb = t // QB, ctx // QB
    kcol, vcol = 2 * w // kvw, 2 * w // kvw + 1
    cos, sin = _rope_tables(t, ctx)
    prev = lambda i: jnp.maximum(i - 1, 0)
    nxt = lambda i: jnp.minimum(i + 1, nblk - 1)
    wide = lambda col: pl.BlockSpec((1, QB, w), lambda b, i: (b, i, col))
    kv = lambda col, f: pl.BlockSpec((1, QB, kvw), lambda b, i: (b, f(i), col))
    kvc = lambda col: pl.BlockSpec((1, ctx, kvw), lambda b, i: (b, 0, col))
    tab = lambda f: pl.BlockSpec((QB, LANES), lambda b, i: (f(i), 0))
    same = lambda i: i
    return pl.pallas_call(
        functools.partial(_swa_kernel, ncb=ncb, seq=t - ctx, ngroups=nkv),
        grid=(nb, nblk),
        in_specs=[pl.BlockSpec(memory_space=pltpu.SMEM), wide(0), wide(1), kvc(kcol), kvc(vcol),
                  kv(kcol, prev), kv(kcol, same), kv(kcol, nxt), kv(vcol, prev), kv(vcol, same), kv(vcol, nxt),
                  tab(same), tab(same), tab(prev), tab(prev), tab(nxt), tab(nxt)],
        out_specs=wide(0),
        out_shape=jax.ShapeDtypeStruct((nb, t, w), BF16),
        compiler_params=_cparams(("parallel", "arbitrary")),
        name="swa",
    )(sink, z, z, z, z, z, z, z, z, z, z, cos, sin, cos, sin, cos, sin)


C_GN_EPS = 64e-5
C_LORA_PAD = 128


def _prepc_kernel(x_ref, xp_ref, xn_ref, nw_ref, shl_ref, scl_ref, shc_ref, scc_ref, mu_ref, o_ref, scr,
                  *, tm, ctx, t_all):
    base = pl.program_id(1) * tm
    nw = nw_ref[...]
    mods = (scl_ref[0], shl_ref[0], scc_ref[0], shc_ref[0])
    row = base + lax.broadcasted_iota(jnp.int32, (tm, 1), 0)
    r8 = lax.broadcasted_iota(jnp.int32, (8, 1), 0)
    h = _normmod(x_ref[0], nw, row < ctx, *mods)
    scr[0:8, :] = _normmod(xp_ref[0], nw, (base - 8 + r8) < ctx, *mods)
    scr[8:tm + 8, :] = h
    scr[tm + 8:tm + 16, :] = _normmod(xn_ref[0], nw, (base + tm + r8) < ctx, *mods)
    has_prev = jnp.logical_and(row != 0, row != ctx)
    has_next = jnp.logical_and(row != ctx - 1, row != t_all - 1)
    xx = 0.5 * (jnp.where(has_prev, scr[7:tm + 7, :], 0.0) + jnp.where(has_next, scr[9:tm + 9, :], 0.0)) - h
    for n in range(o_ref.shape[0]):
        o_ref[n, 0] = (h + xx * mu_ref[n:n + 1, :]).astype(o_ref.dtype)


def _prepc(xs, nw, mod, mu, ctx):
    nb, t, d = xs.shape
    tm = _tile(t, 264)
    nmix = mu.shape[0]
    last8 = t // 8 - 1
    return pl.pallas_call(
        functools.partial(_prepc_kernel, tm=tm, ctx=ctx, t_all=t),
        grid=(nb, t // tm),
        in_specs=[pl.BlockSpec((1, tm, d), lambda b, i: (b, i, 0)),
                  pl.BlockSpec((1, 8, d), lambda b, i: (b, jnp.maximum(i * (tm // 8) - 1, 0), 0)),
                  pl.BlockSpec((1, 8, d), lambda b, i: (b, jnp.minimum((i + 1) * (tm // 8), last8), 0)),
                  pl.BlockSpec((1, d), lambda b, i: (0, 0))] + _mod_specs(d, nb) +
                 [pl.BlockSpec((nmix, d), lambda b, i: (0, 0))],
        out_specs=pl.BlockSpec((nmix, 1, tm, d), lambda b, i: (0, b, i, 0)),
        out_shape=jax.ShapeDtypeStruct((nmix, nb, t, d), BF16),
        scratch_shapes=[pltpu.VMEM((tm + 16, d), F32)],
        compiler_params=_cparams(("parallel", "parallel")),
        name="prep_rwkv",
    )(xs, xs, xs, nw.reshape(1, d), mod, mod, mod, mod, mu)


def _lora_kernel(xw_ref, xa_ref, w1_ref, a1_ref, w2_ref, a2_ref, w0_ref, a0_ref, lw_ref, ic_ref):
    t1 = jnp.tanh(_dot(xw_ref[0, 0], w1_ref[...]))
    t2 = _dot(xa_ref[0, 0], a1_ref[...])
    for d in range(2):
        sl = slice(d * C_LORA_PAD, (d + 1) * C_LORA_PAD)
        nx = -(w0_ref[d] + _dot(t1[:, sl], w2_ref[d]))
        w_log = -(jnp.maximum(nx, 0.0) + jnp.log1p(jnp.exp(-jnp.abs(nx)))) - 0.5
        lw_ref[d, 0] = -jnp.exp(w_log)
        ic_ref[d, 0] = _sigmoid(a0_ref[d] + _dot(t2[:, sl], a2_ref[d]))


def _pad_lora(w_in, w_out):
    r = w_in.shape[2]
    a = jnp.pad(w_in, ((0, 0), (0, 0), (0, C_LORA_PAD - r)))
    a = jnp.concatenate([a[0], a[1]], axis=1).astype(BF16)
    b = jnp.pad(w_out, ((0, 0), (0, C_LORA_PAD - r), (0, 0))).astype(BF16)
    return a, b


def _lora(mix, w0, w1, w2, a0, a1, a2):
    _, nb, t, d = mix.shape
    w = w0.shape[1]
    tm = _tile(t, 264)
    w1p, w2p = _pad_lora(w1, w2)
    a1p, a2p = _pad_lora(a1, a2)
    full = lambda shape: pl.BlockSpec(shape, lambda b, i: (0,) * len(shape))
    out = pl.BlockSpec((2, 1, tm, w), lambda b, i: (0, b, i, 0))
    return pl.pallas_call(
        _lora_kernel,
        grid=(nb, t // tm),
        in_specs=[pl.BlockSpec((1, 1, tm, d), lambda b, i: (4, b, i, 0)),
                  pl.BlockSpec((1, 1, tm, d), lambda b, i: (5, b, i, 0)),
                  full(w1p.shape), full(a1p.shape), full(w2p.shape), full(a2p.shape),
                  full((2, 1, w)), full((2, 1, w))],
        out_specs=[out, out],
        out_shape=[jax.ShapeDtypeStruct((2, nb, t, w), F32)] * 2,
        compiler_params=_cparams(("parallel", "parallel")),
        name="rwkv_lora",
    )(mix, mix, w1p, a1p, w2p, a2p, w0.reshape(2, 1, w), a0.reshape(2, 1, w))


def _rwkv_chunk(r, k, v, lw, ic, kkw, kaw, rkw, ht, reverse):
    lane = lax.broadcasted_iota(jnp.int32, (1, LANES), 1)
    left = lane < HEAD

    def segsum(x):
        sl = jnp.sum(jnp.where(left, x, 0.0), axis=-1, keepdims=True)
        sr = jnp.sum(jnp.where(left, 0.0, x), axis=-1, keepdims=True)
        return jnp.where(left, sl, sr)

    def stack(x):
        return jnp.concatenate([jnp.where(left, x, 0.0), jnp.where(left, 0.0, x)], axis=0)

    kx = k * kkw
    kk = kx / jnp.maximum(jnp.sqrt(segsum(kx * kx)), 1e-12)
    kd = k * (1.0 + (ic - 1.0) * kaw)
    a = -kk
    b = kk * ic
    bonus = segsum(r * kd * rkw) * v

    ti = lax.broadcasted_iota(jnp.int32, (CHUNK, CHUNK), 0)
    si = lax.broadcasted_iota(jnp.int32, (CHUNK, CHUNK), 1)
    lam = _dot_sel(((si >= ti) if reverse else (si <= ti)).astype(BF16), lw)
    end = 0 if reverse else CHUNK - 1
    lam_c = lam[end:end + 1, :]
    einv = jnp.exp(-lam)
    ebar = jnp.exp(lam_c - lam)
    ast = stack(a * jnp.exp(lam - lw))
    rst = stack(r * jnp.exp(lam))
    kst = stack(kd * einv)
    bst = stack(b * einv)
    vst = stack(v)

    g = _dot(jnp.concatenate([ast, rst], axis=0), jnp.concatenate([kst, bst], axis=0), 1, 1)
    n2 = 2 * CHUNK
    t2 = lax.broadcasted_iota(jnp.int32, (n2, n2), 0)
    s2 = lax.broadcasted_iota(jnp.int32, (n2, n2), 1)
    strict = (s2 > t2) if reverse else (s2 < t2)
    incl = (s2 >= t2) if reverse else (s2 <= t2)
    a_ak = jnp.where(strict, g[:n2, :n2], 0.0)
    n_ab = jnp.where(strict, g[:n2, n2:], 0.0)
    a_rk = jnp.where(incl, g[n2:, :n2], 0.0)
    a_rb = jnp.where(incl, g[n2:, n2:], 0.0)

    tinv = jnp.where(t2 == s2, 1.0, 0.0) + n_ab
    m = n_ab
    for _ in range(CHUNK.bit_length() - 2):
        m = _dot_hi(m, m)
        tinv = tinv + _dot_hi(tinv, m)

    vs = _dot(jnp.concatenate([a_ak, a_rk], axis=0), vst)
    hs = _dot(jnp.concatenate([ast, rst], axis=0), ht, 1, 1)
    ust = _dot(tinv, hs[:n2] + vs[:n2])
    yst = hs[n2:] + vs[n2:] + _dot(a_rb, ust)
    y = yst[:CHUNK] + yst[CHUNK:]
    ht_new = ht * jnp.exp(lam_c) + _dot(jnp.concatenate([vst, ust], axis=0),
                                        jnp.concatenate([stack(kd * ebar), stack(b * ebar)], axis=0), 0, 0)
    return y, bonus, ht_new


def _rwkv_kernel(r_ref, k_ref, v_ref, lw_ref, ic_ref, kk_ref, ka_ref, rk_ref, y_ref, bon_ref, ht_ref,
                 *, tb, reverse):
    @pl.when(pl.program_id(2) == 0)
    def _():
        ht_ref[...] = jnp.zeros_like(ht_ref)

    ht = ht_ref[...]
    nch = tb // CHUNK
    for c in (range(nch - 1, -1, -1) if reverse else range(nch)):
        sl = slice(c * CHUNK, (c + 1) * CHUNK)
        y, bon, ht = _rwkv_chunk(r_ref[0, sl, :], k_ref[0, sl, :], v_ref[0, sl, :], lw_ref[0, 0, sl, :],
                                 ic_ref[0, 0, sl, :], kk_ref[...], ka_ref[...], rk_ref[...], ht, reverse)
        y_ref[0, sl, :] = y
        bon_ref[0, sl, :] = bon
    ht_ref[...] = ht


def _rwkv_scan(r, k, v, lw, ic, k_k, k_a, r_k, ctx, reverse, tb=None):
    nb, t, w = r.shape
    tb = tb or _tile(ctx, 256, CHUNK)
    assert t % tb == 0 and ctx % tb == 0
    nblk, ncb = t // tb, ctx // tb
    tmap = _time_block_map(nblk, ncb, reverse)
    d = 1 if reverse else 0
    blk = pl.BlockSpec((1, tb, LANES), lambda b, p, i: (b, tmap(i), p))
    blkd = pl.BlockSpec((1, 1, tb, LANES), lambda b, p, i: (d, b, tmap(i), p))
    par = pl.BlockSpec((1, LANES), lambda b, p, i: (0, p))
    return pl.pallas_call(
        functools.partial(_rwkv_kernel, tb=tb, reverse=reverse),
        grid=(nb, w // LANES, nblk),
        in_specs=[blk, blk, blk, blkd, blkd, par, par, par],
        out_specs=[blk, blk],
        out_shape=[jax.ShapeDtypeStruct((nb, t, w), F32)] * 2,
        scratch_shapes=[pltpu.VMEM((LANES, LANES), F32)],
        compiler_params=_cparams(("parallel", "parallel", "arbitrary")),
        name="rwkv_bwd" if reverse else "rwkv_fwd",
    )(r, k, v, lw, ic, k_k.reshape(1, w), k_a.reshape(1, w), r_k.reshape(1, w))


def _rwkv_out_kernel(yf_ref, yb_ref, bf_ref, bb_ref, g_ref, lnw_ref, lnb_ref, o_ref):
    left = lax.broadcasted_iota(jnp.int32, (1, LANES), 1) < HEAD

    def segmean(x):
        sl = jnp.sum(jnp.where(left, x, 0.0), axis=-1, keepdims=True)
        sr = jnp.sum(jnp.where(left, 0.0, x), axis=-1, keepdims=True)
        return jnp.where(left, sl, sr) * (1.0 / HEAD)

    y = yf_ref[0] + yb_ref[0]
    dlt = y - segmean(y)
    zn = dlt * lax.rsqrt(segmean(dlt * dlt) + C_GN_EPS)
    o = zn * lnw_ref[...] + lnb_ref[...] + bf_ref[0] + bb_ref[0]
    o_ref[0] = (o * _silu(g_ref[0])).astype(o_ref.dtype)


def _rwkv_out(y_f, y_b, bon_f, bon_b, gate, ln_w, ln_b):
    nb, t, w = y_f.shape
    tm = _tile(t, 1056)
    blk = pl.BlockSpec((1, tm, LANES), lambda b, i, p: (b, i, p))
    par = pl.BlockSpec((1, LANES), lambda b, i, p: (0, p))
    return pl.pallas_call(
        _rwkv_out_kernel,
        grid=(nb, t // tm, w // LANES),
        in_specs=[blk] * 5 + [par, par],
        out_specs=blk,
        out_shape=jax.ShapeDtypeStruct((nb, t, w), BF16),
        compiler_params=_cparams(("parallel", "parallel", "arbitrary")),
        name="rwkv_out",
    )(y_f, y_b, bon_f, bon_b, gate, ln_w.reshape(1, w), ln_b.reshape(1, w))


def _rwkv_mix(mix, w_in, w0, w1, w2, a0, a1, a2, k_k, k_a, r_k, ln_w, ln_b, ctx, tb=None):
    r, k, v, gate = (_matmul(mix, w_in[n].astype(BF16), sel=n) for n in range(4))
    lw, ic = _lora(mix, w0, w1, w2, a0, a1, a2)
    y_f, bon_f = _rwkv_scan(r, k, v, lw, ic, k_k, k_a, r_k, ctx, False, tb)
    y_b, bon_b = _rwkv_scan(r, k, v, lw, ic, k_k, k_a, r_k, ctx, True, tb)
    return _rwkv_out(y_f, y_b, bon_f, bon_b, gate, ln_w, ln_b)


def _swa_mix(h, w_in, sink, ctx, width):
    kv2 = w_in.shape[1] - 2 * width
    w_perm = jnp.concatenate([w_in[:, :width], w_in[:, width + kv2:], w_in[:, width:width + kv2]], axis=1)
    z = _matmul(h, w_perm.astype(BF16))
    return _swa(z, sink, ctx, kv2 // 2 // HEAD)


def _final_norm_kernel(x_ref, w_ref, o_ref):
    x = x_ref[0]
    ms = jnp.mean(x * x, axis=-1, keepdims=True)
    o_ref[0] = x * lax.rsqrt(ms + RMS_EPS) * w_ref[...]


def _final_norm(xs, w, ctx):
    nb, t, d = xs.shape
    seq = t - ctx
    tm = _tile(math.gcd(ctx, seq), 1024)
    off = ctx // tm
    return pl.pallas_call(
        _final_norm_kernel,
        grid=(nb, seq // tm),
        in_specs=[pl.BlockSpec((1, tm, d), lambda b, i: (b, i + off, 0)),
                  pl.BlockSpec((1, d), lambda b, i: (0, 0))],
        out_specs=pl.BlockSpec((1, tm, d), lambda b, i: (b, i, 0)),
        out_shape=jax.ShapeDtypeStruct((nb, seq, d), F32),
        compiler_params=_cparams(("parallel", "parallel")),
        name="final_norm",
    )(xs, w.reshape(1, d))


def kernel(x, c, ctx, c_ctx, norm_w, mod_w, mod_b, a_w_in, a_lb_raw, a_onorm_w, a_w_out, b_w_in, b_sink, b_w_out,
           c_mu, c_w_in, c_w0, c_w1, c_w2, c_a0, c_a1, c_a2, c_k_k, c_k_a, c_r_k, c_ln_w, c_ln_b, c_w_out,
           final_norm_w):
    nb, seq, d = x.shape
    nctx = ctx.shape[1]
    depth = norm_w.shape[0]
    xs = jnp.concatenate([ctx, x], axis=1)
    cvec = jnp.zeros((8, d), F32).at[:nb].set(c).at[nb].set(c_ctx)
    mod_all = _modulation(cvec, mod_w, mod_b)
    lb_all = jnp.cumsum(jax.nn.softmax(a_lb_raw.astype(F32), axis=0), axis=0)
    lb_all = lb_all - lb_all[0]
    for i in range(depth):
        j, kind = i // 3, i % 3
        mod = mod_all[i].reshape(8, 1, 3 * d)
        if kind == 0:
            h = _prep(xs, norm_w[i], mod, nctx)
            m = _hgrn2_mix(h, a_w_in[j].astype(BF16), lb_all[j], a_onorm_w[j], nctx)
            w_out = a_w_out[j]
        elif kind == 1:
            h = _prep(xs, norm_w[i], mod, nctx)
            m = _swa_mix(h, b_w_in[j], b_sink[j], nctx, b_w_out.shape[1])
            w_out = b_w_out[j]
        else:
            mix = _prepc(xs, norm_w[i], mod, c_mu[j], nctx)
            m = _rwkv_mix(mix, c_w_in[j], c_w0[j], c_w1[j], c_w2[j], c_a0[j], c_a1[j], c_a2[j],
                          c_k_k[j], c_k_a[j], c_r_k[j], c_ln_w[j], c_ln_b[j], nctx)
            w_out = c_w_out[j]
        xs = _matmul_residual(m, w_out.astype(BF16), xs, mod, nctx)
    return _final_norm(xs, final_norm_w, nctx)
```

```python
import functools
import math

import jax
import jax.numpy as jnp
from jax import lax
from jax.experimental import pallas as pl
from jax.experimental.pallas import tpu as pltpu

F32 = jnp.float32
BF16 = jnp.bfloat16
NEG_INF = float("-inf")

RMS_EPS = 1e-6
CHUNK = 64
SUB = 16
NSUB = CHUNK // SUB
A_DK = 128
VMEM_LIMIT = 56 * 1024 * 1024


def _cparams(sem):
    return pltpu.CompilerParams(dimension_semantics=sem, vmem_limit_bytes=VMEM_LIMIT)


def _tile(n, target, mult=8):
    best = None
    for t in range(mult, min(n, target) + 1, mult):
        if n % t == 0:
            best = t
    assert best is not None, (n, target, mult)
    return best


def _dot(a, b, ca=1, cb=0):
    return lax.dot_general(a.astype(BF16), b.astype(BF16), (((ca,), (cb,)), ((), ())),
                           preferred_element_type=F32)


def _split2(x):
    hi = x.astype(BF16)
    lo = (x - hi.astype(F32)).astype(BF16)
    return hi, lo


def _split3(x):
    hi = x.astype(BF16)
    r = x - hi.astype(F32)
    mid = r.astype(BF16)
    lo = (r - mid.astype(F32)).astype(BF16)
    return hi, mid, lo


def _dot_sel(sel, x):
    hi, mid, lo = _split3(x)
    return _dot(sel, hi) + _dot(sel, mid) + _dot(sel, lo)


def _dot_hi(a, b, ca=1, cb=0):
    ah, al = _split2(a)
    bh, bl = _split2(b)
    return _dot(ah, bh, ca, cb) + _dot(ah, bl, ca, cb) + _dot(al, bh, ca, cb)


def _sigmoid(x):
    return 1.0 / (1.0 + jnp.exp(-x))


def _silu(x):
    return x * _sigmoid(x)


def _mod_kernel(c_ref, w_ref, b_ref, o_ref):
    s = _silu(c_ref[...])
    o_ref[0] = _dot_hi(s, w_ref[0]) + b_ref[0]


def _modulation(cvec, mod_w, mod_b):
    depth, d, n = mod_w.shape
    tn = _tile(n, 512, 128)
    return pl.pallas_call(
        _mod_kernel,
        grid=(depth, n // tn),
        in_specs=[pl.BlockSpec((8, d), lambda i, j: (0, 0)),
                  pl.BlockSpec((1, d, tn), lambda i, j: (i, 0, j)),
                  pl.BlockSpec((1, 1, tn), lambda i, j: (i, 0, j))],
        out_specs=pl.BlockSpec((1, 8, tn), lambda i, j: (i, 0, j)),
        out_shape=jax.ShapeDtypeStruct((depth, 8, n), F32),
        compiler_params=_cparams(("parallel", "parallel")),
        name="modulation",
    )(cvec, mod_w, mod_b.reshape(depth, 1, n))


def _normmod(x, nw, is_ctx, scl, shl, scc, shc):
    ms = jnp.mean(x * x, axis=-1, keepdims=True)
    y = x * lax.rsqrt(ms + RMS_EPS) * nw
    sc = jnp.where(is_ctx, scc, scl)
    sh = jnp.where(is_ctx, shc, shl)
    return y * (1.0 + sc) + sh


def _prep_kernel(x_ref, nw_ref, shl_ref, scl_ref, shc_ref, scc_ref, o_ref, *, tm, ctx):
    row = pl.program_id(1) * tm + lax.broadcasted_iota(jnp.int32, (tm, 1), 0)
    h = _normmod(x_ref[0], nw_ref[...], row < ctx, scl_ref[0], shl_ref[0], scc_ref[0], shc_ref[0])
    o_ref[0] = h.astype(o_ref.dtype)


def _mod_specs(d, nb):
    return [pl.BlockSpec((1, 1, d), lambda b, i, *_: (b, 0, 0)),
            pl.BlockSpec((1, 1, d), lambda b, i, *_: (b, 0, 1)),
            pl.BlockSpec((1, 1, d), lambda b, i, *_: (nb, 0, 0)),
            pl.BlockSpec((1, 1, d), lambda b, i, *_: (nb, 0, 1))]


def _prep(xs, nw, mod, ctx):
    nb, t, d = xs.shape
    tm = _tile(t, 1056)
    return pl.pallas_call(
        functools.partial(_prep_kernel, tm=tm, ctx=ctx),
        grid=(nb, t // tm),
        in_specs=[pl.BlockSpec((1, tm, d), lambda b, i: (b, i, 0)),
                  pl.BlockSpec((1, d), lambda b, i: (0, 0))] + _mod_specs(d, nb),
        out_specs=pl.BlockSpec((1, tm, d), lambda b, i: (b, i, 0)),
        out_shape=jax.ShapeDtypeStruct((nb, t, d), BF16),
        compiler_params=_cparams(("parallel", "parallel")),
        name="prep",
    )(xs, nw.reshape(1, d), mod, mod, mod, mod)


def _mm_kernel(a_ref, w_ref, o_ref):
    o_ref[0] = _dot(a_ref[0], w_ref[...]).astype(o_ref.dtype)


def _mm_sel_kernel(a_ref, w_ref, o_ref):
    o_ref[0] = _dot(a_ref[0, 0], w_ref[...]).astype(o_ref.dtype)


def _matmul(a, w, out_dtype=F32, sel=None):
    nb, t, k = a.shape[-3:]
    n = w.shape[1]
    tm = _tile(t, 1056)
    tn = _tile(n, 512, 128)
    if sel is None:
        a_spec = pl.BlockSpec((1, tm, k), lambda b, i, j: (b, i, 0))
    else:
        a_spec = pl.BlockSpec((1, 1, tm, k), lambda b, i, j: (sel, b, i, 0))
    return pl.pallas_call(
        _mm_kernel if sel is None else _mm_sel_kernel,
        grid=(nb, t // tm, n // tn),
        in_specs=[a_spec,
                  pl.BlockSpec((k, tn), lambda b, i, j: (0, j))],
        out_specs=pl.BlockSpec((1, tm, tn), lambda b, i, j: (b, i, j)),
        out_shape=jax.ShapeDtypeStruct((nb, t, n), out_dtype),
        compiler_params=_cparams(("parallel", "parallel", "arbitrary")),
        name="matmul",
    )(a, w)


def _mm_res_kernel(a_ref, w_ref, x_ref, gl_ref, gc_ref, o_ref, *, tm, ctx):
    row = pl.program_id(1) * tm + lax.broadcasted_iota(jnp.int32, (tm, 1), 0)
    g = jnp.where(row < ctx, gc_ref[0], gl_ref[0])
    o_ref[0] = x_ref[0] + g * _dot(a_ref[0], w_ref[...])


def _matmul_residual(a, w, xs, mod, ctx):
    nb, t, k = a.shape
    n = w.shape[1]
    tm = _tile(t, 1056)
    tn = _tile(n, 512, 128)
    goff = 2 * n // tn
    return pl.pallas_call(
        functools.partial(_mm_res_kernel, tm=tm, ctx=ctx),
        grid=(nb, t // tm, n // tn),
        in_specs=[pl.BlockSpec((1, tm, k), lambda b, i, j: (b, i, 0)),
                  pl.BlockSpec((k, tn), lambda b, i, j: (0, j)),
                  pl.BlockSpec((1, tm, tn), lambda b, i, j: (b, i, j)),
                  pl.BlockSpec((1, 1, tn), lambda b, i, j: (b, 0, goff + j)),
                  pl.BlockSpec((1, 1, tn), lambda b, i, j: (nb, 0, goff + j))],
        out_specs=pl.BlockSpec((1, tm, tn), lambda b, i, j: (b, i, j)),
        out_shape=jax.ShapeDtypeStruct((nb, t, n), F32),
        compiler_params=_cparams(("parallel", "parallel", "arbitrary")),
        name="matmul_residual",
    )(a, w, xs, mod, mod)


def _time_block_map(nblk, nctx_blk, reverse):
    if not reverse:
        return lambda i: i
    return lambda i: jnp.where(i < nctx_blk, nctx_blk - 1 - i, nblk - 1 - i + nctx_blk)


NLEVEL = CHUNK.bit_length() - 1
GLA_HEADS_PER_STEP = 4


def _gla_gates(z, llb, l1m):
    ls = -(jnp.maximum(-z, 0.0) + jnp.log(1.0 + jnp.exp(-jnp.abs(z))))
    c2 = l1m + ls
    g = jnp.maximum(llb, c2) + jnp.log(1.0 + jnp.exp(-jnp.abs(llb - c2)))
    k = 1.0 - jnp.exp(g)

    return g, k


SUBLANES = 8


def _gla_consts(reverse):
    r = lax.broadcasted_iota(jnp.int32, (CHUNK, CHUNK), 0)
    i = lax.broadcasted_iota(jnp.int32, (CHUNK, CHUNK), 1)
    tr = (CHUNK - 1 - r) if reverse else r
    ti = (CHUNK - 1 - i) if reverse else i
    rows = lax.broadcasted_iota(jnp.int32, (CHUNK, 1), 0)
    trow = (CHUNK - 1 - rows) if reverse else rows
    masks, uppers = [], []
    for lvl in range(NLEVEL):
        same = (tr >> (lvl + 1)) == (ti >> (lvl + 1))
        masks.append(jnp.logical_and(jnp.logical_and(same, ((tr >> lvl) & 1) == 1), ((ti >> lvl) & 1) == 0))
        uppers.append(((trow >> lvl) & 1) == 1)
    sub = lax.broadcasted_iota(jnp.int32, (SUBLANES, 1), 0)
    return (ti <= tr).astype(BF16), masks, uppers, r == i, sub


def _boundary_rows(b, lvl, sub, reverse):
    grp, half = 1 << (lvl + 1), 1 << lvl
    row_of = (lambda tau: CHUNK - 1 - tau) if reverse else (lambda tau: tau)
    tiles = []
    for a in range(CHUNK // SUBLANES):
        taus = sorted(row_of(a * SUBLANES + u) for u in range(SUBLANES))
        picks = {}
        for tau in taus:
            src = row_of((tau // grp) * grp + half - 1)
            picks.setdefault(src, []).append(row_of(tau) - a * SUBLANES)
        tile = None
        for src, subs in picks.items():
            piece = jnp.broadcast_to(b[src:src + 1, :], (SUBLANES, b.shape[1]))
            if tile is None:
                tile = piece
            else:
                tile = jnp.where(jnp.logical_and(sub >= min(subs), sub <= max(subs)), piece, tile)
        tiles.append(tile)
    return jnp.concatenate(tiles, axis=0)


def _gla_intra(q, v, g, k, consts, reverse):
    csum, masks, uppers, eye, sub = consts
    b = _dot_sel(csum, g)
    end = 0 if reverse else CHUNK - 1
    b_end = b[end:end + 1, :]
    qt = q * jnp.exp(b)
    kbar = k * jnp.exp(b_end - b)
    a = jnp.where(eye, jnp.sum(q * k, axis=-1, keepdims=True), 0.0)
    for lvl in range(NLEVEL):
        if lvl == 0:
            x = jnp.where(uppers[0], q * (1.0 - k), k)
        else:
            x = jnp.where(uppers[lvl], q, k) * jnp.exp(-jnp.abs(b - _boundary_rows(b, lvl, sub, reverse)))
        x = x.astype(BF16)
        a = a + jnp.where(masks[lvl], _dot(x, x, 1, 1), 0.0)
    return _dot(a, v), qt.astype(BF16), _dot(v, kbar, 0, 0), jnp.exp(b_end)


def _gla_kernel(q_ref, v_ref, z_ref, llb_ref, l1m_ref, o_ref, st_ref, *, tb, reverse, scale, hps):
    @pl.when(pl.program_id(2) == 0)
    def _():
        st_ref[...] = jnp.zeros_like(st_ref)

    consts = _gla_consts(reverse)
    nch = tb // CHUNK
    order = list(range(nch - 1, -1, -1) if reverse else range(nch))
    items = [(h, c) for h in range(hps) for c in order]

    def blk(ref, h, c):
        return ref[0, c * CHUNK:(c + 1) * CHUNK, h * A_DK:(h + 1) * A_DK]

    def par(ref, h):
        return ref[:, h * A_DK:(h + 1) * A_DK]

    gk = [_gla_gates(blk(z_ref, h, c), par(llb_ref, h), par(l1m_ref, h)) for h, c in items]
    intra = [_gla_intra(blk(q_ref, h, c) * scale, blk(v_ref, h, c), g, k, consts, reverse)
             for (h, c), (g, k) in zip(items, gk)]
    for h in range(hps):
        st = st_ref[h]
        states = []
        for n, (hh, c) in enumerate(items):
            if hh == h:
                states.append((c, n, st))
                st = st * intra[n][3] + intra[n][2]
        st_ref[h] = st
        for c, n, st_in in states:
            o_ref[0, c * CHUNK:(c + 1) * CHUNK, h * A_DK:(h + 1) * A_DK] = intra[n][0] + _dot(intra[n][1], st_in, 1, 1)


def _gla(z, llb, l1m, ctx, reverse, tb=None):
    nb, t, _ = z.shape
    nh = llb.shape[1] // A_DK
    tb = tb or _tile(ctx, 256, CHUNK)
    assert t % tb == 0 and ctx % tb == 0
    nblk, ncb = t // tb, ctx // tb
    tmap = _time_block_map(nblk, ncb, reverse)
    hps = GLA_HEADS_PER_STEP
    assert nh % hps == 0
    ng = nh // hps
    zcol = (3 if reverse else 2) * ng
    wide = hps * A_DK

    def spec(col0):
        return pl.BlockSpec((1, tb, wide), lambda b, h, i: (b, tmap(i), col0 + h))

    return pl.pallas_call(
        functools.partial(_gla_kernel, tb=tb, reverse=reverse, scale=A_DK ** -0.5, hps=hps),
        grid=(nb, ng, nblk),
        in_specs=[spec(0), spec(ng), spec(zcol),
                  pl.BlockSpec((1, wide), lambda b, h, i: (0, h)),
                  pl.BlockSpec((1, wide), lambda b, h, i: (0, h))],
        out_specs=pl.BlockSpec((1, tb, wide), lambda b, h, i: (b, tmap(i), h)),
        out_shape=jax.ShapeDtypeStruct((nb, t, nh * A_DK), F32),
        scratch_shapes=[pltpu.VMEM((hps, A_DK, A_DK), F32)],
        compiler_params=_cparams(("parallel", "parallel", "arbitrary")),
        name="gla_bwd" if reverse else "gla_fwd",
    )(z, z, z, llb, l1m)


def _hgrn2_out_kernel(of_ref, ob_ref, g_ref, w_ref, o_ref):
    y = of_ref[0] + ob_ref[0]
    ms = jnp.mean(y * y, axis=-1, keepdims=True)
    yn = y * lax.rsqrt(ms + RMS_EPS) * w_ref[...]
    o_ref[0] = (yn * _silu(g_ref[0])).astype(o_ref.dtype)


def _hgrn2_out(o_f, o_b, z, onorm_w):
    nb, t, w = o_f.shape
    nh = w // A_DK
    tm = _tile(t, 1056)
    blk = lambda col0: pl.BlockSpec((1, tm, A_DK), lambda b, i, h: (b, i, col0 + h))
    return pl.pallas_call(
        _hgrn2_out_kernel,
        grid=(nb, t // tm, nh),
        in_specs=[blk(0), blk(0), blk(4 * nh), pl.BlockSpec((1, A_DK), lambda b, i, h: (0, 0))],
        out_specs=blk(0),
        out_shape=jax.ShapeDtypeStruct((nb, t, w), BF16),
        compiler_params=_cparams(("parallel", "parallel", "arbitrary")),
        name="hgrn2_out",
    )(o_f, o_b, z, onorm_w.reshape(1, A_DK))


def _hgrn2_mix(h, w_in, lb, onorm_w, ctx, tb=None):
    z = _matmul(h, w_in)
    llb = jnp.log(lb).reshape(1, -1)
    l1m = jnp.log1p(-lb).reshape(1, -1)
    o_f = _gla(z, llb, l1m, ctx, False, tb)
    o_b = _gla(z, llb, l1m, ctx, True, tb)
    return _hgrn2_out(o_f, o_b, z, onorm_w)


GRID_W = 64
ROPE_BASE = 10000.0
HEAD = 64
QB = 128
LANES = 128


def _rope_tables(t_all, ctx):
    quarter = HEAD // 4
    inv = ROPE_BASE ** (-jnp.arange(quarter, dtype=F32) / quarter)
    tl = jnp.arange(t_all - ctx)
    row = (tl // GRID_W).astype(F32)
    col = (tl % GRID_W).astype(F32)
    hdim = jnp.arange(LANES) % HEAD
    use_col = hdim >= 2 * quarter
    second = (hdim % (2 * quarter)) >= quarter
    pos = jnp.where(use_col[None, :], col[:, None], row[:, None])
    ang = pos * inv[hdim % quarter][None, :]
    cos = jnp.concatenate([jnp.ones((ctx, LANES), F32), jnp.cos(ang)], axis=0)
    sin = jnp.concatenate([jnp.zeros((ctx, LANES), F32),
                           jnp.where(second[None, :], jnp.sin(ang), -jnp.sin(ang))], axis=0)
    return cos, sin


def _rope(x, cos, sin, first):
    partner = jnp.where(first, pltpu.roll(x, LANES - HEAD // 4, 1), pltpu.roll(x, HEAD // 4, 1))
    return x * cos + partner * sin


def _swa_kernel(sink_ref, q_ref, g_ref, kc_ref, vc_ref, kp_ref, kq_ref, kn_ref, vp_ref, vq_ref, vn_ref,
                cq_ref, sq_ref, cp_ref, sp_ref, cn_ref, sn_ref, o_ref, *, ncb, seq, ngroups):
    i = pl.program_id(1)
    lane = lax.broadcasted_iota(jnp.int32, (1, LANES), 1)
    first = (lane % (HEAD // 2)) < (HEAD // 4)
    left = lane < HEAD
    ntile = kc_ref.shape[2] // LANES

    def tile(x, c):
        return x[:, c * LANES:(c + 1) * LANES]

    kwin = [(kp_ref[0], cp_ref[...], sp_ref[...]), (kq_ref[0], cq_ref[...], sq_ref[...]),
            (kn_ref[0], cn_ref[...], sn_ref[...])]
    kt = [jnp.concatenate([tile(kc_ref[0], c)] + [_rope(tile(kk, c), cs, sn, first) for kk, cs, sn in kwin], axis=0)
          for c in range(ntile)]
    vt = [jnp.concatenate([tile(vc_ref[0], c), tile(vp_ref[0], c), tile(vq_ref[0], c), tile(vn_ref[0], c)], axis=0)
          for c in range(ntile)]
    nk = kt[0].shape[0]
    nctx = nk - 3 * QB

    r = lax.broadcasted_iota(jnp.int32, (QB, QB), 0)
    cidx = lax.broadcasted_iota(jnp.int32, (QB, QB), 1)
    qblk = i - ncb
    open_if = lambda cond: jnp.where(cond, 0.0, NEG_INF)
    b_prev = open_if(cidx >= r) + open_if(qblk >= 1)
    b_cur = jnp.zeros((QB, QB), F32) + open_if(qblk >= 0)
    b_next = open_if(cidx <= r) + open_if(jnp.logical_and(qblk >= 0, (qblk + 2) * QB <= seq))
    bias = jnp.concatenate([jnp.zeros((QB, nctx), F32), b_prev, b_cur, b_next], axis=1)

    scale = HEAD ** -0.5
    heads_per_group = q_ref.shape[2] // HEAD // ngroups
    pairs = heads_per_group // 2
    for g in range(ngroups):
        c, even = g // 2, g % 2 == 0
        own = left if even else jnp.logical_not(left)
        k_own = jnp.where(own, kt[c], 0.0)
        v_own = jnp.where(own, vt[c], 0.0)
        k_swp = pltpu.roll(k_own, HEAD, 1)
        v_swp = pltpu.roll(v_own, HEAD, 1)
        kk2 = jnp.concatenate([k_own, k_swp] if even else [k_swp, k_own], axis=0).astype(BF16)
        vv2 = jnp.concatenate([v_own, v_swp] if even else [v_swp, v_own], axis=0).astype(BF16)
        q8 = jnp.concatenate(
            [_rope(tile(q_ref[0], g * pairs + p), cq_ref[...], sq_ref[...], first) * scale for p in range(pairs)],
            axis=0)
        s8 = _dot(q8, kk2, 1, 1)
        probs, dens = [], []
        for p in range(pairs):
            pe2, den2 = [], []
            for e in range(2):
                sk = sink_ref[g * heads_per_group + 2 * p + e]
                s = s8[p * QB:(p + 1) * QB, e * nk:(e + 1) * nk] + bias
                m = jnp.maximum(jnp.max(s, axis=-1, keepdims=True), sk)
                pe = jnp.exp(s - m)
                den2.append(jnp.sum(pe, axis=-1, keepdims=True) + jnp.exp(sk - m))
                pe2.append(pe.astype(BF16))
            probs.append(jnp.concatenate(pe2, axis=1))
            dens.append(jnp.where(left, den2[0], den2[1]))
        o8 = _dot(jnp.concatenate(probs, axis=0), vv2)
        for p in range(pairs):
            col = g * pairs + p
            o = o8[p * QB:(p + 1) * QB] / dens[p]
            o_ref[0, :, col * LANES:(col + 1) * LANES] = (o * _silu(tile(g_ref[0], col))).astype(o_ref.dtype)


def _swa(z, sink, ctx, nkv):
    nb, t, ncol = z.shape
    kvw = nkv * HEAD
    w = (ncol - 2 * kvw) // 2
    assert t % QB == 0 and ctx % QB == 0 and w % kvw == 0
    nblk, ncb = t // QB, ctx // QB
    kcol, vcol = 2 * w // kvw, 2 * w // kvw + 1
    cos, sin = _rope_tables(t, ctx)
    prev = lambda i: jnp.maximum(i - 1, 0)
    nxt = lambda i: jnp.minimum(i + 1, nblk - 1)
    wide = lambda col: pl.BlockSpec((1, QB, w), lambda b, i: (b, i, col))
    kv = lambda col, f: pl.BlockSpec((1, QB, kvw), lambda b, i: (b, f(i), col))
    kvc = lambda col: pl.BlockSpec((1, ctx, kvw), lambda b, i: (b, 0, col))
    tab = lambda f: pl.BlockSpec((QB, LANES), lambda b, i: (f(i), 0))
    same = lambda i: i
    return pl.pallas_call(
        functools.partial(_swa_kernel, ncb=ncb, seq=t - ctx, ngroups=nkv),
        grid=(nb, nblk),
        in_specs=[pl.BlockSpec(memory_space=pltpu.SMEM), wide(0), wide(1), kvc(kcol), kvc(vcol),
                  kv(kcol, prev), kv(kcol, same), kv(kcol, nxt), kv(vcol, prev), kv(vcol, same), kv(vcol, nxt),
                  tab(same), tab(same), tab(prev), tab(prev), tab(nxt), tab(nxt)],
        out_specs=wide(0),
        out_shape=jax.ShapeDtypeStruct((nb, t, w), BF16),
        compiler_params=_cparams(("parallel", "arbitrary")),
        name="swa",
    )(sink, z, z, z, z, z, z, z, z, z, z, cos, sin, cos, sin, cos, sin)


C_GN_EPS = 64e-5
C_LORA_PAD = 128


def _prepc_kernel(x_ref, xp_ref, xn_ref, nw_ref, shl_ref, scl_ref, shc_ref, scc_ref, mu_ref, o_ref, scr,
                  *, tm, ctx, t_all):
    base = pl.program_id(1) * tm
    nw = nw_ref[...]
    mods = (scl_ref[0], shl_ref[0], scc_ref[0], shc_ref[0])
    row = base + lax.broadcasted_iota(jnp.int32, (tm, 1), 0)
    r8 = lax.broadcasted_iota(jnp.int32, (8, 1), 0)
    h = _normmod(x_ref[0], nw, row < ctx, *mods)
    scr[0:8, :] = _normmod(xp_ref[0], nw, (base - 8 + r8) < ctx, *mods)
    scr[8:tm + 8, :] = h
    scr[tm + 8:tm + 16, :] = _normmod(xn_ref[0], nw, (base + tm + r8) < ctx, *mods)
    has_prev = jnp.logical_and(row != 0, row != ctx)
    has_next = jnp.logical_and(row != ctx - 1, row != t_all - 1)
    xx = 0.5 * (jnp.where(has_prev, scr[7:tm + 7, :], 0.0) + jnp.where(has_next, scr[9:tm + 9, :], 0.0)) - h
    for n in range(o_ref.shape[0]):
        o_ref[n, 0] = (h + xx * mu_ref[n:n + 1, :]).astype(o_ref.dtype)


def _prepc(xs, nw, mod, mu, ctx):
    nb, t, d = xs.shape
    tm = _tile(t, 264)
    nmix = mu.shape[0]
    last8 = t // 8 - 1
    return pl.pallas_call(
        functools.partial(_prepc_kernel, tm=tm, ctx=ctx, t_all=t),
        grid=(nb, t // tm),
        in_specs=[pl.BlockSpec((1, tm, d), lambda b, i: (b, i, 0)),
                  pl.BlockSpec((1, 8, d), lambda b, i: (b, jnp.maximum(i * (tm // 8) - 1, 0), 0)),
                  pl.BlockSpec((1, 8, d), lambda b, i: (b, jnp.minimum((i + 1) * (tm // 8), last8), 0)),
                  pl.BlockSpec((1, d), lambda b, i: (0, 0))] + _mod_specs(d, nb) +
                 [pl.BlockSpec((nmix, d), lambda b, i: (0, 0))],
        out_specs=pl.BlockSpec((nmix, 1, tm, d), lambda b, i: (0, b, i, 0)),
        out_shape=jax.ShapeDtypeStruct((nmix, nb, t, d), BF16),
        scratch_shapes=[pltpu.VMEM((tm + 16, d), F32)],
        compiler_params=_cparams(("parallel", "parallel")),
        name="prep_rwkv",
    )(xs, xs, xs, nw.reshape(1, d), mod, mod, mod, mod, mu)


def _lora_kernel(xw_ref, xa_ref, w1_ref, a1_ref, w2_ref, a2_ref, w0_ref, a0_ref, lw_ref, ic_ref):
    t1 = jnp.tanh(_dot(xw_ref[0, 0], w1_ref[...]))
    t2 = _dot(xa_ref[0, 0], a1_ref[...])
    for d in range(2):
        sl = slice(d * C_LORA_PAD, (d + 1) * C_LORA_PAD)
        nx = -(w0_ref[d] + _dot(t1[:, sl], w2_ref[d]))
        w_log = -(jnp.maximum(nx, 0.0) + jnp.log(1.0 + jnp.exp(-jnp.abs(nx)))) - 0.5
        lw_ref[d, 0] = -jnp.exp(w_log)
        ic_ref[d, 0] = _sigmoid(a0_ref[d] + _dot(t2[:, sl], a2_ref[d]))


def _pad_lora(w_in, w_out):
    r = w_in.shape[2]
    a = jnp.pad(w_in, ((0, 0), (0, 0), (0, C_LORA_PAD - r)))
    a = jnp.concatenate([a[0], a[1]], axis=1).astype(BF16)
    b = jnp.pad(w_out, ((0, 0), (0, C_LORA_PAD - r), (0, 0))).astype(BF16)
    return a, b


def _lora(mix, w0, w1, w2, a0, a1, a2):
    _, nb, t, d = mix.shape
    w = w0.shape[1]
    tm = _tile(t, 264)
    w1p, w2p = _pad_lora(w1, w2)
    a1p, a2p = _pad_lora(a1, a2)
    full = lambda shape: pl.BlockSpec(shape, lambda b, i: (0,) * len(shape))
    out = pl.BlockSpec((2, 1, tm, w), lambda b, i: (0, b, i, 0))
    return pl.pallas_call(
        _lora_kernel,
        grid=(nb, t // tm),
        in_specs=[pl.BlockSpec((1, 1, tm, d), lambda b, i: (4, b, i, 0)),
                  pl.BlockSpec((1, 1, tm, d), lambda b, i: (5, b, i, 0)),
                  full(w1p.shape), full(a1p.shape), full(w2p.shape), full(a2p.shape),
                  full((2, 1, w)), full((2, 1, w))],
        out_specs=[out, out],
        out_shape=[jax.ShapeDtypeStruct((2, nb, t, w), F32)] * 2,
        compiler_params=_cparams(("parallel", "parallel")),
        name="rwkv_lora",
    )(mix, mix, w1p, a1p, w2p, a2p, w0.reshape(2, 1, w), a0.reshape(2, 1, w))


RWKV_PAIRS_PER_STEP = 4


def _rwkv_consts(reverse):
    ti = lax.broadcasted_iota(jnp.int32, (CHUNK, CHUNK), 0)
    si = lax.broadcasted_iota(jnp.int32, (CHUNK, CHUNK), 1)
    csum = ((si >= ti) if reverse else (si <= ti)).astype(BF16)
    t2 = lax.broadcasted_iota(jnp.int32, (CHUNK, LANES), 0)
    s2 = lax.broadcasted_iota(jnp.int32, (CHUNK, LANES), 1) % CHUNK
    strict = (s2 > t2) if reverse else (s2 < t2)
    incl = (s2 >= t2) if reverse else (s2 <= t2)
    eye = jnp.where(t2 == s2, 1.0, 0.0)
    left = lax.broadcasted_iota(jnp.int32, (1, LANES), 1) < HEAD
    vi = lax.broadcasted_iota(jnp.int32, (LANES, LANES), 0) < HEAD
    ki = lax.broadcasted_iota(jnp.int32, (LANES, LANES), 1) < HEAD
    return csum, strict, incl, eye, left, vi == ki


def _stack(x, left):
    x = x.astype(BF16)
    zero = jnp.zeros_like(x)
    return jnp.concatenate([jnp.where(left, x, zero), jnp.where(left, zero, x)], axis=0)


def _rwkv_prepare(r, k, v, lw, ic, kkw, kaw, rkw, consts, reverse):
    csum, _, _, _, left, _ = consts
    stack = functools.partial(_stack, left=left)

    def segsum(x):
        sl = jnp.sum(jnp.where(left, x, 0.0), axis=-1, keepdims=True)
        sr = jnp.sum(jnp.where(left, 0.0, x), axis=-1, keepdims=True)
        return jnp.where(left, sl, sr)

    kx = k * kkw
    kk = kx / jnp.maximum(jnp.sqrt(segsum(kx * kx)), 1e-12)
    kd = k * (1.0 + (ic - 1.0) * kaw)
    b = kk * ic
    bonus = segsum(r * kd * rkw) * v
    lam = _dot_sel(csum, lw)
    end = 0 if reverse else CHUNK - 1
    lam_c = lam[end:end + 1, :]
    einv = jnp.exp(-lam)
    ebar = jnp.exp(lam_c - lam)
    at = (-kk * jnp.exp(lam - lw)).astype(BF16)
    rt = (r * jnp.exp(lam)).astype(BF16)
    return dict(at=at, rt=rt, ar=jnp.concatenate([at, rt], axis=0),
                kb=jnp.concatenate([stack(kd * einv), stack(b * einv)], axis=0),
                v=v.astype(BF16), vst=stack(v), kbar=(kd * ebar).astype(BF16), bbar=(b * ebar).astype(BF16),
                dec=jnp.exp(lam_c), bonus=bonus)


def _inv_unit_lower(ns, eye, left):
    stack = functools.partial(_stack, left=left)
    ps = [eye + n for n in ns]
    ms = [_dot(n, stack(n)) for n in ns]
    for j in range(CHUNK.bit_length() - 4):
        both = [_dot(jnp.concatenate([m.astype(BF16), p.astype(BF16)], axis=0), stack(m)) for m, p in zip(ms, ps)]
        ps = [p + b[CHUNK:] for p, b in zip(ps, both)]
        ms = [b[:CHUNK] for b in both]
    ps = [p + _dot(p, stack(m)) for p, m in zip(ps, ms)]
    out = []
    for n, p in zip(ns, ps):
        tb = p.astype(BF16)
        nh, nl = _split2(n)
        res = _dot(jnp.concatenate([nh, nl], axis=0), stack(tb))
        e = eye - tb.astype(F32) + (res[:CHUNK] + res[CHUNK:])
        out.append(tb.astype(F32) + _dot(tb, stack(e)))
    return out


def _rwkv_kernel(r_ref, k_ref, v_ref, lw_ref, ic_ref, kk_ref, ka_ref, rk_ref, y_ref, bon_ref, ht_ref,
                 *, tb, reverse, pps):
    @pl.when(pl.program_id(2) == 0)
    def _():
        ht_ref[...] = jnp.zeros_like(ht_ref)

    consts = _rwkv_consts(reverse)
    _, strict, incl, eye, left, same_head = consts
    stack = functools.partial(_stack, left=left)
    nch = tb // CHUNK
    order = list(range(nch - 1, -1, -1) if reverse else range(nch))
    items = [(p, c) for p in range(pps) for c in order]

    def blk(ref, p, c):
        return ref[0, c * CHUNK:(c + 1) * CHUNK, p * LANES:(p + 1) * LANES]

    def blkd(ref, p, c):
        return ref[0, 0, c * CHUNK:(c + 1) * CHUNK, p * LANES:(p + 1) * LANES]

    def par(ref, p):
        return ref[:, p * LANES:(p + 1) * LANES]

    pre = [_rwkv_prepare(blk(r_ref, p, c), blk(k_ref, p, c), blk(v_ref, p, c), blkd(lw_ref, p, c),
                         blkd(ic_ref, p, c), par(kk_ref, p), par(ka_ref, p), par(rk_ref, p), consts, reverse)
           for p, c in items]
    gs = [_dot(x["ar"], x["kb"], 1, 1) for x in pre]
    a_ak = [jnp.where(strict, g[:CHUNK, :LANES], 0.0).astype(BF16) for g in gs]
    a_rk = [jnp.where(incl, g[CHUNK:, :LANES], 0.0).astype(BF16) for g in gs]
    a_rb = [jnp.where(incl, g[CHUNK:, LANES:], 0.0).astype(BF16) for g in gs]
    tinv = _inv_unit_lower([jnp.where(strict, g[:CHUNK, LANES:], 0.0) for g in gs], eye, left)
    av = [_dot(jnp.concatenate([ak, rk], axis=0), x["vst"]) for ak, rk, x in zip(a_ak, a_rk, pre)]
    wu = [_dot(t, jnp.concatenate([stack(x["at"]), stack(a[:CHUNK])], axis=1)).astype(BF16)
          for t, x, a in zip(tinv, pre, av)]
    ry = [_dot(rb, jnp.concatenate([stack(w[:, :LANES]), stack(w[:, LANES:])], axis=1)) for rb, w in zip(a_rb, wu)]
    rw = [(x["rt"].astype(F32) + y[:, :LANES]).astype(BF16) for x, y in zip(pre, ry)]
    y0 = [a[CHUNK:] + y[:, LANES:] for a, y in zip(av, ry)]
    ft = [jnp.where(same_head, _dot(jnp.concatenate([x["v"], w[:, LANES:]], axis=0),
                                    jnp.concatenate([x["kbar"], x["bbar"]], axis=0), 0, 0), 0.0)
          for x, w in zip(pre, wu)]
    gm = [jnp.where(same_head, _dot(x["bbar"], w[:, :LANES], 0, 0), 0.0).astype(BF16) for x, w in zip(pre, wu)]

    hts = [ht_ref[p] for p in range(pps)]
    for step in range(nch):
        for p in range(pps):
            n = p * nch + step
            c = items[n][1]
            ht = hts[p]
            htb = ht.astype(BF16)
            y_ref[0, c * CHUNK:(c + 1) * CHUNK, p * LANES:(p + 1) * LANES] = y0[n] + _dot(rw[n], htb, 1, 1)
            bon_ref[0, c * CHUNK:(c + 1) * CHUNK, p * LANES:(p + 1) * LANES] = pre[n]["bonus"]
            hts[p] = ht * pre[n]["dec"] + ft[n] + _dot(htb, gm[n], 1, 1)
    for p in range(pps):
        ht_ref[p] = hts[p]


def _rwkv_scan(r, k, v, lw, ic, k_k, k_a, r_k, ctx, reverse, tb=None):
    nb, t, w = r.shape
    tb = tb or _tile(ctx, 256, CHUNK)
    assert t % tb == 0 and ctx % tb == 0
    nblk, ncb = t // tb, ctx // tb
    tmap = _time_block_map(nblk, ncb, reverse)
    d = 1 if reverse else 0
    pps = RWKV_PAIRS_PER_STEP
    wide = pps * LANES
    assert w % wide == 0
    blk = pl.BlockSpec((1, tb, wide), lambda b, p, i: (b, tmap(i), p))
    blkd = pl.BlockSpec((1, 1, tb, wide), lambda b, p, i: (d, b, tmap(i), p))
    par = pl.BlockSpec((1, wide), lambda b, p, i: (0, p))
    return pl.pallas_call(
        functools.partial(_rwkv_kernel, tb=tb, reverse=reverse, pps=pps),
        grid=(nb, w // wide, nblk),
        in_specs=[blk, blk, blk, blkd, blkd, par, par, par],
        out_specs=[blk, blk],
        out_shape=[jax.ShapeDtypeStruct((nb, t, w), F32)] * 2,
        scratch_shapes=[pltpu.VMEM((pps, LANES, LANES), F32)],
        compiler_params=_cparams(("parallel", "parallel", "arbitrary")),
        name="rwkv_bwd" if reverse else "rwkv_fwd",
    )(r, k, v, lw, ic, k_k.reshape(1, w), k_a.reshape(1, w), r_k.reshape(1, w))


def _rwkv_out_kernel(yf_ref, yb_ref, bf_ref, bb_ref, g_ref, lnw_ref, lnb_ref, o_ref):
    left = lax.broadcasted_iota(jnp.int32, (1, LANES), 1) < HEAD

    def segmean(x):
        sl = jnp.sum(jnp.where(left, x, 0.0), axis=-1, keepdims=True)
        sr = jnp.sum(jnp.where(left, 0.0, x), axis=-1, keepdims=True)
        return jnp.where(left, sl, sr) * (1.0 / HEAD)

    y = yf_ref[0] + yb_ref[0]
    dlt = y - segmean(y)
    zn = dlt * lax.rsqrt(segmean(dlt * dlt) + C_GN_EPS)
    o = zn * lnw_ref[...] + lnb_ref[...] + bf_ref[0] + bb_ref[0]
    o_ref[0] = (o * _silu(g_ref[0])).astype(o_ref.dtype)


def _rwkv_out(y_f, y_b, bon_f, bon_b, gate, ln_w, ln_b):
    nb, t, w = y_f.shape
    tm = _tile(t, 1056)
    blk = pl.BlockSpec((1, tm, LANES), lambda b, i, p: (b, i, p))
    par = pl.BlockSpec((1, LANES), lambda b, i, p: (0, p))
    return pl.pallas_call(
        _rwkv_out_kernel,
        grid=(nb, t // tm, w // LANES),
        in_specs=[blk] * 5 + [par, par],
        out_specs=blk,
        out_shape=jax.ShapeDtypeStruct((nb, t, w), BF16),
        compiler_params=_cparams(("parallel", "parallel", "arbitrary")),
        name="rwkv_out",
    )(y_f, y_b, bon_f, bon_b, gate, ln_w.reshape(1, w), ln_b.reshape(1, w))


def _rwkv_mix(mix, w_in, w0, w1, w2, a0, a1, a2, k_k, k_a, r_k, ln_w, ln_b, ctx, tb=None):
    r, k, v, gate = (_matmul(mix, w_in[n].astype(BF16), sel=n) for n in range(4))
    lw, ic = _lora(mix, w0, w1, w2, a0, a1, a2)
    y_f, bon_f = _rwkv_scan(r, k, v, lw, ic, k_k, k_a, r_k, ctx, False, tb)
    y_b, bon_b = _rwkv_scan(r, k, v, lw, ic, k_k, k_a, r_k, ctx, True, tb)
    return _rwkv_out(y_f, y_b, bon_f, bon_b, gate, ln_w, ln_b)


def _swa_mix(h, w_in, sink, ctx, width):
    kv2 = w_in.shape[1] - 2 * width
    w_perm = jnp.concatenate([w_in[:, :width], w_in[:, width + kv2:], w_in[:, width:width + kv2]], axis=1)
    z = _matmul(h, w_perm.astype(BF16))
    return _swa(z, sink, ctx, kv2 // 2 // HEAD)


def _final_norm_kernel(x_ref, w_ref, o_ref):
    x = x_ref[0]
    ms = jnp.mean(x * x, axis=-1, keepdims=True)
    o_ref[0] = x * lax.rsqrt(ms + RMS_EPS) * w_ref[...]


def _final_norm(xs, w, ctx):
    nb, t, d = xs.shape
    seq = t - ctx
    tm = _tile(math.gcd(ctx, seq), 1024)
    off = ctx // tm
    return pl.pallas_call(
        _final_norm_kernel,
        grid=(nb, seq // tm),
        in_specs=[pl.BlockSpec((1, tm, d), lambda b, i: (b, i + off, 0)),
                  pl.BlockSpec((1, d), lambda b, i: (0, 0))],
        out_specs=pl.BlockSpec((1, tm, d), lambda b, i: (b, i, 0)),
        out_shape=jax.ShapeDtypeStruct((nb, seq, d), F32),
        compiler_params=_cparams(("parallel", "parallel")),
        name="final_norm",
    )(xs, w.reshape(1, d))


def kernel(x, c, ctx, c_ctx, norm_w, mod_w, mod_b, a_w_in, a_lb_raw, a_onorm_w, a_w_out, b_w_in, b_sink, b_w_out,
           c_mu, c_w_in, c_w0, c_w1, c_w2, c_a0, c_a1, c_a2, c_k_k, c_k_a, c_r_k, c_ln_w, c_ln_b, c_w_out,
           final_norm_w):
    nb, seq, d = x.shape
    nctx = ctx.shape[1]
    depth = norm_w.shape[0]
    xs = jnp.concatenate([ctx, x], axis=1)
    cvec = jnp.zeros((8, d), F32).at[:nb].set(c).at[nb].set(c_ctx)
    mod_all = _modulation(cvec, mod_w, mod_b)
    lb_all = jnp.cumsum(jax.nn.softmax(a_lb_raw.astype(F32), axis=0), axis=0)
    lb_all = lb_all - lb_all[0]
    for i in range(depth):
        j, kind = i // 3, i % 3
        mod = mod_all[i].reshape(8, 1, 3 * d)
        if kind == 0:
            h = _prep(xs, norm_w[i], mod, nctx)
            m = _hgrn2_mix(h, a_w_in[j].astype(BF16), lb_all[j], a_onorm_w[j], nctx)
            w_out = a_w_out[j]
        elif kind == 1:
            h = _prep(xs, norm_w[i], mod, nctx)
            m = _swa_mix(h, b_w_in[j], b_sink[j], nctx, b_w_out.shape[1])
            w_out = b_w_out[j]
        else:
            mix = _prepc(xs, norm_w[i], mod, c_mu[j], nctx)
            m = _rwkv_mix(mix, c_w_in[j], c_w0[j], c_w1[j], c_w2[j], c_a0[j], c_a1[j], c_a2[j],
                          c_k_k[j], c_k_a[j], c_r_k[j], c_ln_w[j], c_ln_b[j], nctx)
            w_out = c_w_out[j]
        xs = _matmul_residual(m, w_out.astype(BF16), xs, mod, nctx)
    return _final_norm(xs, final_norm_w, nctx)
```

```python
import functools
import math

import jax
import jax.numpy as jnp
from jax import lax
from jax.experimental import pallas as pl
from jax.experimental.pallas import tpu as pltpu

F32 = jnp.float32
BF16 = jnp.bfloat16
NEG_INF = float("-inf")

RMS_EPS = 1e-6
CHUNK = 64
SUB = 16
NSUB = CHUNK // SUB
A_DK = 128
SUBLANES = 8
VMEM_LIMIT = 56 * 1024 * 1024


def _cparams(sem):
    return pltpu.CompilerParams(dimension_semantics=sem, vmem_limit_bytes=VMEM_LIMIT)


def _tile(n, target, mult=8):
    best = None
    for t in range(mult, min(n, target) + 1, mult):
        if n % t == 0:
            best = t
    assert best is not None, (n, target, mult)
    return best


def _dot(a, b, ca=1, cb=0):
    return lax.dot_general(a.astype(BF16), b.astype(BF16), (((ca,), (cb,)), ((), ())),
                           preferred_element_type=F32)


def _split2(x):
    hi = x.astype(BF16)
    lo = (x - hi.astype(F32)).astype(BF16)
    return hi, lo


def _split3(x):
    hi = x.astype(BF16)
    r = x - hi.astype(F32)
    mid = r.astype(BF16)
    lo = (r - mid.astype(F32)).astype(BF16)
    return hi, mid, lo


def _dot_sel(sel, x):
    hi, mid, lo = _split3(x)
    return _dot(sel, hi) + _dot(sel, mid) + _dot(sel, lo)


def _dot_hi(a, b, ca=1, cb=0):
    ah, al = _split2(a)
    bh, bl = _split2(b)
    return _dot(ah, bh, ca, cb) + _dot(ah, bl, ca, cb) + _dot(al, bh, ca, cb)


def _cumsum_rows(x, reverse):
    ntile = x.shape[0] // SUBLANES
    sub = lax.broadcasted_iota(jnp.int32, (SUBLANES, 1), 0)
    tiles = [x[j * SUBLANES:(j + 1) * SUBLANES] for j in range(ntile)]
    for s in (1, 2, 4):
        if reverse:
            tiles = [y + jnp.where(sub < SUBLANES - s, pltpu.roll(y, SUBLANES - s, 0), 0.0) for y in tiles]
        else:
            tiles = [y + jnp.where(sub >= s, pltpu.roll(y, s, 0), 0.0) for y in tiles]
    edge = 0 if reverse else SUBLANES - 1
    order = range(ntile - 1, -1, -1) if reverse else range(ntile)
    carry = None
    out = [None] * ntile
    for j in order:
        out[j] = tiles[j] if carry is None else tiles[j] + carry
        total = tiles[j][edge:edge + 1, :]
        carry = total if carry is None else carry + total
    return jnp.concatenate(out, axis=0)


def _sigmoid(x):
    return 1.0 / (1.0 + jnp.exp(-x))


def _silu(x):
    return x * _sigmoid(x)


def _mod_kernel(c_ref, w_ref, b_ref, o_ref):
    s = _silu(c_ref[...])
    o_ref[0] = _dot_hi(s, w_ref[0]) + b_ref[0]


def _modulation(cvec, mod_w, mod_b):
    depth, d, n = mod_w.shape
    tn = _tile(n, 512, 128)
    return pl.pallas_call(
        _mod_kernel,
        grid=(depth, n // tn),
        in_specs=[pl.BlockSpec((8, d), lambda i, j: (0, 0)),
                  pl.BlockSpec((1, d, tn), lambda i, j: (i, 0, j)),
                  pl.BlockSpec((1, 1, tn), lambda i, j: (i, 0, j))],
        out_specs=pl.BlockSpec((1, 8, tn), lambda i, j: (i, 0, j)),
        out_shape=jax.ShapeDtypeStruct((depth, 8, n), F32),
        compiler_params=_cparams(("parallel", "parallel")),
        name="modulation",
    )(cvec, mod_w, mod_b.reshape(depth, 1, n))


def _normmod(x, nw, is_ctx, scl, shl, scc, shc):
    ms = jnp.mean(x * x, axis=-1, keepdims=True)
    y = x * lax.rsqrt(ms + RMS_EPS) * nw
    sc = jnp.where(is_ctx, scc, scl)
    sh = jnp.where(is_ctx, shc, shl)
    return y * (1.0 + sc) + sh


def _prep_kernel(x_ref, nw_ref, shl_ref, scl_ref, shc_ref, scc_ref, o_ref, *, tm, ctx):
    row = pl.program_id(1) * tm + lax.broadcasted_iota(jnp.int32, (tm, 1), 0)
    h = _normmod(x_ref[0], nw_ref[...], row < ctx, scl_ref[0], shl_ref[0], scc_ref[0], shc_ref[0])
    o_ref[0] = h.astype(o_ref.dtype)


def _mod_specs(d, nb):
    return [pl.BlockSpec((1, 1, d), lambda b, i, *_: (b, 0, 0)),
            pl.BlockSpec((1, 1, d), lambda b, i, *_: (b, 0, 1)),
            pl.BlockSpec((1, 1, d), lambda b, i, *_: (nb, 0, 0)),
            pl.BlockSpec((1, 1, d), lambda b, i, *_: (nb, 0, 1))]


def _prep(xs, nw, mod, ctx):
    nb, t, d = xs.shape
    tm = _tile(t, 1056)
    return pl.pallas_call(
        functools.partial(_prep_kernel, tm=tm, ctx=ctx),
        grid=(nb, t // tm),
        in_specs=[pl.BlockSpec((1, tm, d), lambda b, i: (b, i, 0)),
                  pl.BlockSpec((1, d), lambda b, i: (0, 0))] + _mod_specs(d, nb),
        out_specs=pl.BlockSpec((1, tm, d), lambda b, i: (b, i, 0)),
        out_shape=jax.ShapeDtypeStruct((nb, t, d), BF16),
        compiler_params=_cparams(("parallel", "parallel")),
        name="prep",
    )(xs, nw.reshape(1, d), mod, mod, mod, mod)


def _mm_kernel(a_ref, w_ref, o_ref):
    o_ref[0] = _dot(a_ref[0], w_ref[...]).astype(o_ref.dtype)


def _mm_sel_kernel(a_ref, w_ref, o_ref):
    o_ref[0] = _dot(a_ref[0, 0], w_ref[...]).astype(o_ref.dtype)


def _matmul(a, w, out_dtype=F32, sel=None):
    nb, t, k = a.shape[-3:]
    n = w.shape[1]
    tm = _tile(t, 1056)
    tn = _tile(n, 512, 128)
    if sel is None:
        a_spec = pl.BlockSpec((1, tm, k), lambda b, i, j: (b, i, 0))
    else:
        a_spec = pl.BlockSpec((1, 1, tm, k), lambda b, i, j: (sel, b, i, 0))
    return pl.pallas_call(
        _mm_kernel if sel is None else _mm_sel_kernel,
        grid=(nb, t // tm, n // tn),
        in_specs=[a_spec,
                  pl.BlockSpec((k, tn), lambda b, i, j: (0, j))],
        out_specs=pl.BlockSpec((1, tm, tn), lambda b, i, j: (b, i, j)),
        out_shape=jax.ShapeDtypeStruct((nb, t, n), out_dtype),
        compiler_params=_cparams(("parallel", "parallel", "arbitrary")),
        name="matmul",
    )(a, w)


def _mm_res_kernel(a_ref, w_ref, x_ref, gl_ref, gc_ref, o_ref, *, tm, ctx):
    row = pl.program_id(1) * tm + lax.broadcasted_iota(jnp.int32, (tm, 1), 0)
    g = jnp.where(row < ctx, gc_ref[0], gl_ref[0])
    o_ref[0] = x_ref[0] + g * _dot(a_ref[0], w_ref[...])


def _matmul_residual(a, w, xs, mod, ctx):
    nb, t, k = a.shape
    n = w.shape[1]
    tm = _tile(t, 1056)
    tn = _tile(n, 512, 128)
    goff = 2 * n // tn
    return pl.pallas_call(
        functools.partial(_mm_res_kernel, tm=tm, ctx=ctx),
        grid=(nb, t // tm, n // tn),
        in_specs=[pl.BlockSpec((1, tm, k), lambda b, i, j: (b, i, 0)),
                  pl.BlockSpec((k, tn), lambda b, i, j: (0, j)),
                  pl.BlockSpec((1, tm, tn), lambda b, i, j: (b, i, j)),
                  pl.BlockSpec((1, 1, tn), lambda b, i, j: (b, 0, goff + j)),
                  pl.BlockSpec((1, 1, tn), lambda b, i, j: (nb, 0, goff + j))],
        out_specs=pl.BlockSpec((1, tm, tn), lambda b, i, j: (b, i, j)),
        out_shape=jax.ShapeDtypeStruct((nb, t, n), F32),
        compiler_params=_cparams(("parallel", "parallel", "arbitrary")),
        name="matmul_residual",
    )(a, w, xs, mod, mod)


def _time_block_map(nblk, nctx_blk, reverse):
    if not reverse:
        return lambda i: i
    return lambda i: jnp.where(i < nctx_blk, nctx_blk - 1 - i, nblk - 1 - i + nctx_blk)


NLEVEL = CHUNK.bit_length() - 1
GLA_HEADS_PER_STEP = 8


def _gla_gates(z, llb, l1m):
    ls = -(jnp.maximum(-z, 0.0) + jnp.log(1.0 + jnp.exp(-jnp.abs(z))))
    c2 = l1m + ls
    g = jnp.maximum(llb, c2) + jnp.log(1.0 + jnp.exp(-jnp.abs(llb - c2)))
    k = 1.0 - jnp.exp(g)

    return g, k


def _gla_consts(reverse):
    r = lax.broadcasted_iota(jnp.int32, (CHUNK, CHUNK), 0)
    i = lax.broadcasted_iota(jnp.int32, (CHUNK, CHUNK), 1)
    tr = (CHUNK - 1 - r) if reverse else r
    ti = (CHUNK - 1 - i) if reverse else i
    masks = []
    for lvl in range(NLEVEL):
        same = (tr >> (lvl + 1)) == (ti >> (lvl + 1))
        pair = jnp.logical_and(jnp.logical_and(same, ((tr >> lvl) & 1) == 1), ((ti >> lvl) & 1) == 0)
        masks.append(jnp.where(pair, 1.0, 0.0))
    masks.append(jnp.where(r == i, 1.0, 0.0))
    return masks


def _boundary_rows(b, lvl, sub, reverse):
    grp, half = 1 << (lvl + 1), 1 << lvl
    row_of = (lambda tau: CHUNK - 1 - tau) if reverse else (lambda tau: tau)
    tiles = []
    for a in range(CHUNK // SUBLANES):
        taus = sorted(row_of(a * SUBLANES + u) for u in range(SUBLANES))
        picks = {}
        for tau in taus:
            src = row_of((tau // grp) * grp + half - 1)
            picks.setdefault(src, []).append(row_of(tau) - a * SUBLANES)
        tile = None
        for src, subs in picks.items():
            piece = jnp.broadcast_to(b[src:src + 1, :], (SUBLANES, b.shape[1]))
            if tile is None:
                tile = piece
            else:
                tile = jnp.where(jnp.logical_and(sub >= min(subs), sub <= max(subs)), piece, tile)
        tiles.append(tile)
    return jnp.concatenate(tiles, axis=0)


def _gla_intra(q, v, g, k, mask_ref, reverse):
    sub = lax.broadcasted_iota(jnp.int32, (SUBLANES, 1), 0)
    b = _cumsum_rows(g, reverse)
    end = 0 if reverse else CHUNK - 1
    b_end = b[end:end + 1, :]
    qt = q * jnp.exp(b)
    kbar = k * jnp.exp(b_end - b)
    a = mask_ref[NLEVEL] * jnp.sum(q * k, axis=-1, keepdims=True)
    k_prev = pltpu.roll(k, CHUNK - 1 if reverse else 1, 0)
    a = a + mask_ref[0] * jnp.sum(q * (1.0 - k) * k_prev, axis=-1, keepdims=True)
    for lvl in range(1, NLEVEL):
        e = jnp.exp(-jnp.abs(b - _boundary_rows(b, lvl, sub, reverse)))
        a = a + mask_ref[lvl] * _dot(q * e, k * e, 1, 1)
    return _dot(a, v), qt.astype(BF16), _dot(v, kbar, 0, 0), jnp.exp(b_end)


def _gla_kernel(q_ref, v_ref, g_ref, o_ref, st_ref, mask_ref, *, tb, reverse, scale, hps):
    @pl.when(pl.program_id(2) == 0)
    def _():
        st_ref[...] = jnp.zeros_like(st_ref)
        for n, m in enumerate(_gla_consts(reverse)):
            mask_ref[n] = m

    consts = mask_ref
    nch = tb // CHUNK
    order = list(range(nch - 1, -1, -1) if reverse else range(nch))
    items = [(h, c) for h in range(hps) for c in order]

    def blk(ref, h, c):
        return ref[0, c * CHUNK:(c + 1) * CHUNK, h * A_DK:(h + 1) * A_DK]

    gs = [blk(g_ref, h, c) for h, c in items]
    intra = [_gla_intra(blk(q_ref, h, c) * scale, blk(v_ref, h, c), g, 1.0 - jnp.exp(g), consts, reverse)
             for (h, c), g in zip(items, gs)]
    for h in range(hps):
        st = st_ref[h]
        states = []
        for n, (hh, c) in enumerate(items):
            if hh == h:
                states.append((c, n, st))
                st = st * intra[n][3] + intra[n][2]
        st_ref[h] = st
        for c, n, st_in in states:
            o = intra[n][0] + _dot(intra[n][1], st_in, 1, 1)
            o_ref[0, c * CHUNK:(c + 1) * CHUNK, h * A_DK:(h + 1) * A_DK] = o.astype(o_ref.dtype)


def _gla(z, nh, ctx, reverse, tb=None):
    nb, t, _ = z.shape
    tb = tb or _tile(ctx, 256, CHUNK)
    assert t % tb == 0 and ctx % tb == 0
    nblk, ncb = t // tb, ctx // tb
    tmap = _time_block_map(nblk, ncb, reverse)
    hps = GLA_HEADS_PER_STEP
    assert nh % hps == 0
    ng = nh // hps
    zcol = (3 if reverse else 2) * ng
    wide = hps * A_DK

    def spec(col0):
        return pl.BlockSpec((1, tb, wide), lambda b, h, i: (b, tmap(i), col0 + h))

    return pl.pallas_call(
        functools.partial(_gla_kernel, tb=tb, reverse=reverse, scale=A_DK ** -0.5, hps=hps),
        grid=(nb, ng, nblk),
        in_specs=[spec(0), spec(ng), spec(zcol)],
        out_specs=pl.BlockSpec((1, tb, wide), lambda b, h, i: (b, tmap(i), h)),
        out_shape=jax.ShapeDtypeStruct((nb, t, nh * A_DK), BF16),
        scratch_shapes=[pltpu.VMEM((hps, A_DK, A_DK), F32), pltpu.VMEM((NLEVEL + 1, CHUNK, CHUNK), F32)],
        compiler_params=_cparams(("parallel", "parallel", "arbitrary")),
        name="gla_bwd" if reverse else "gla_fwd",
    )(z, z, z)


PROJ_ROW_SPLIT = 3
BF16_ROWS = 16


def _hgrn2_in_kernel(a_ref, w_ref, llb_ref, l1m_ref, o_ref, *, lo, hi):
    j = pl.program_id(2)
    is_gate = jnp.logical_and(j >= lo, j < hi)
    tm = a_ref.shape[1]
    rows = tm // PROJ_ROW_SPLIT

    @pl.when(is_gate)
    def _():
        for r in range(PROJ_ROW_SPLIT):
            sl = slice(r * rows, (r + 1) * rows)
            g, _ = _gla_gates(_dot(a_ref[0, sl, :], w_ref[...]), llb_ref[...], l1m_ref[...])
            o_ref[0, sl, :] = g

    @pl.when(jnp.logical_not(is_gate))
    def _():
        o_ref[0] = _dot(a_ref[0], w_ref[...])


def _hgrn2_in(h, w_in, lb, width):
    nb, t, k = h.shape
    n = w_in.shape[1]
    tm = _tile(t, 1056, BF16_ROWS * PROJ_ROW_SPLIT)
    tn = _tile(width, 512, 128)
    per = width // tn
    lo, hi = 2 * per, 4 * per
    llb = jnp.log(lb).reshape(1, width)
    l1m = jnp.log1p(-lb).reshape(1, width)
    par = pl.BlockSpec((1, tn), lambda b, i, j: (0, jnp.where(jnp.logical_and(j >= lo, j < hi), (j - lo) % per, 0)))
    return pl.pallas_call(
        functools.partial(_hgrn2_in_kernel, lo=lo, hi=hi),
        grid=(nb, t // tm, n // tn),
        in_specs=[pl.BlockSpec((1, tm, k), lambda b, i, j: (b, i, 0)),
                  pl.BlockSpec((k, tn), lambda b, i, j: (0, j)), par, par],
        out_specs=pl.BlockSpec((1, tm, tn), lambda b, i, j: (b, i, j)),
        out_shape=jax.ShapeDtypeStruct((nb, t, n), F32),
        compiler_params=_cparams(("parallel", "parallel", "arbitrary")),
        name="hgrn2_in",
    )(h, w_in, llb, l1m)


def _residual_out(lhs_ref, w_ref, x_ref, gl_ref, gc_ref, o_ref, tm, ctx):
    row = pl.program_id(1) * tm + lax.broadcasted_iota(jnp.int32, (tm, 1), 0)
    g = jnp.where(row < ctx, gc_ref[0], gl_ref[0])
    o_ref[0] = x_ref[0] + g * _dot(lhs_ref[...], w_ref[...])


def _hgrn2_out_kernel(of_ref, ob_ref, g_ref, nw_ref, w_ref, x_ref, gl_ref, gc_ref, o_ref, lhs_ref, *, tm, ctx):
    @pl.when(pl.program_id(2) == 0)
    def _():
        for h in range(of_ref.shape[2] // A_DK):
            sl = slice(h * A_DK, (h + 1) * A_DK)
            y = of_ref[0, :, sl].astype(F32) + ob_ref[0, :, sl].astype(F32)
            ms = jnp.mean(y * y, axis=-1, keepdims=True)
            yn = y * lax.rsqrt(ms + RMS_EPS) * nw_ref[...]
            lhs_ref[:, sl] = (yn * _silu(g_ref[0, :, sl])).astype(lhs_ref.dtype)

    _residual_out(lhs_ref, w_ref, x_ref, gl_ref, gc_ref, o_ref, tm, ctx)


def _residual_specs(tm, tn, k, nb, goff):
    return [pl.BlockSpec((k, tn), lambda b, i, j: (0, j)),
            pl.BlockSpec((1, tm, tn), lambda b, i, j: (b, i, j)),
            pl.BlockSpec((1, 1, tn), lambda b, i, j: (b, 0, goff + j)),
            pl.BlockSpec((1, 1, tn), lambda b, i, j: (nb, 0, goff + j))]


def _hgrn2_out(o_f, o_b, z, onorm_w, w_out, xs, mod, ctx):
    nb, t, w = o_f.shape
    n = w_out.shape[1]
    tm = _tile(t, 528)
    tn = _tile(n, 512, 128)
    wide = pl.BlockSpec((1, tm, w), lambda b, i, j: (b, i, 0))
    return pl.pallas_call(
        functools.partial(_hgrn2_out_kernel, tm=tm, ctx=ctx),
        grid=(nb, t // tm, n // tn),
        in_specs=[wide, wide, pl.BlockSpec((1, tm, w), lambda b, i, j: (b, i, 4)),
                  pl.BlockSpec((1, A_DK), lambda b, i, j: (0, 0))] + _residual_specs(tm, tn, w, nb, 2 * n // tn),
        out_specs=pl.BlockSpec((1, tm, tn), lambda b, i, j: (b, i, j)),
        out_shape=jax.ShapeDtypeStruct((nb, t, n), F32),
        scratch_shapes=[pltpu.VMEM((tm, w), BF16)],
        compiler_params=_cparams(("parallel", "parallel", "arbitrary")),
        name="hgrn2_out",
    )(o_f, o_b, z, onorm_w.reshape(1, A_DK), w_out, xs, mod, mod)


def _hgrn2_layer(h, w_in, lb, onorm_w, w_out, xs, mod, ctx, tb=None):
    width = w_out.shape[0]
    z = _hgrn2_in(h, w_in, lb, width)
    o_f = _gla(z, width // A_DK, ctx, False, tb)
    o_b = _gla(z, width // A_DK, ctx, True, tb)
    return _hgrn2_out(o_f, o_b, z, onorm_w, w_out, xs, mod, ctx)


GRID_W = 64
ROPE_BASE = 10000.0
HEAD = 64
QB = 128
LANES = 128


def _rope_tables(t_all, ctx):
    quarter = HEAD // 4
    inv = ROPE_BASE ** (-jnp.arange(quarter, dtype=F32) / quarter)
    tl = jnp.arange(t_all - ctx)
    row = (tl // GRID_W).astype(F32)
    col = (tl % GRID_W).astype(F32)
    hdim = jnp.arange(LANES) % HEAD
    use_col = hdim >= 2 * quarter
    second = (hdim % (2 * quarter)) >= quarter
    pos = jnp.where(use_col[None, :], col[:, None], row[:, None])
    ang = pos * inv[hdim % quarter][None, :]
    cos = jnp.concatenate([jnp.ones((ctx, LANES), F32), jnp.cos(ang)], axis=0)
    sin = jnp.concatenate([jnp.zeros((ctx, LANES), F32),
                           jnp.where(second[None, :], jnp.sin(ang), -jnp.sin(ang))], axis=0)
    return cos, sin


def _rope(x, cos, sin, first):
    partner = jnp.where(first, pltpu.roll(x, LANES - HEAD // 4, 1), pltpu.roll(x, HEAD // 4, 1))
    return x * cos + partner * sin


def _swa_kernel(sink_ref, q_ref, g_ref, kc_ref, vc_ref, kp_ref, kq_ref, kn_ref, vp_ref, vq_ref, vn_ref,
                cq_ref, sq_ref, cp_ref, sp_ref, cn_ref, sn_ref, o_ref, *, ncb, seq, ngroups):
    i = pl.program_id(1)
    lane = lax.broadcasted_iota(jnp.int32, (1, LANES), 1)
    first = (lane % (HEAD // 2)) < (HEAD // 4)
    left = lane < HEAD
    ntile = kc_ref.shape[2] // LANES

    def tile(x, c):
        return x[:, c * LANES:(c + 1) * LANES]

    kwin = [(kp_ref[0], cp_ref[...], sp_ref[...]), (kq_ref[0], cq_ref[...], sq_ref[...]),
            (kn_ref[0], cn_ref[...], sn_ref[...])]
    kt = [jnp.concatenate([tile(kc_ref[0], c)] + [_rope(tile(kk, c), cs, sn, first) for kk, cs, sn in kwin], axis=0)
          for c in range(ntile)]
    vt = [jnp.concatenate([tile(vc_ref[0], c), tile(vp_ref[0], c), tile(vq_ref[0], c), tile(vn_ref[0], c)], axis=0)
          for c in range(ntile)]
    nk = kt[0].shape[0]
    nctx = nk - 3 * QB

    r = lax.broadcasted_iota(jnp.int32, (QB, QB), 0)
    cidx = lax.broadcasted_iota(jnp.int32, (QB, QB), 1)
    qblk = i - ncb
    open_if = lambda cond: jnp.where(cond, 0.0, NEG_INF)
    b_prev = open_if(cidx >= r) + open_if(qblk >= 1)
    b_cur = jnp.zeros((QB, QB), F32) + open_if(qblk >= 0)
    b_next = open_if(cidx <= r) + open_if(jnp.logical_and(qblk >= 0, (qblk + 2) * QB <= seq))
    bias = jnp.concatenate([jnp.zeros((QB, nctx), F32), b_prev, b_cur, b_next], axis=1)

    scale = HEAD ** -0.5
    heads_per_group = q_ref.shape[2] // HEAD // ngroups
    pairs = heads_per_group // 2
    for g in range(ngroups):
        c, even = g // 2, g % 2 == 0
        own = left if even else jnp.logical_not(left)
        k_own = jnp.where(own, kt[c], 0.0)
        v_own = jnp.where(own, vt[c], 0.0)
        k_swp = pltpu.roll(k_own, HEAD, 1)
        v_swp = pltpu.roll(v_own, HEAD, 1)
        kk2 = jnp.concatenate([k_own, k_swp] if even else [k_swp, k_own], axis=0).astype(BF16)
        vv2 = jnp.concatenate([v_own, v_swp] if even else [v_swp, v_own], axis=0).astype(BF16)
        q8 = jnp.concatenate(
            [_rope(tile(q_ref[0], g * pairs + p), cq_ref[...], sq_ref[...], first) * scale for p in range(pairs)],
            axis=0)
        s8 = _dot(q8, kk2, 1, 1)
        probs, dens = [], []
        for p in range(pairs):
            pe2, den2 = [], []
            for e in range(2):
                sk = sink_ref[g * heads_per_group + 2 * p + e]
                s = s8[p * QB:(p + 1) * QB, e * nk:(e + 1) * nk] + bias
                m = jnp.maximum(jnp.max(s, axis=-1, keepdims=True), sk)
                pe = jnp.exp(s - m)
                den2.append(jnp.sum(pe, axis=-1, keepdims=True) + jnp.exp(sk - m))
                pe2.append(pe.astype(BF16))
            probs.append(jnp.concatenate(pe2, axis=1))
            dens.append(jnp.where(left, den2[0], den2[1]))
        o8 = _dot(jnp.concatenate(probs, axis=0), vv2)
        for p in range(pairs):
            col = g * pairs + p
            o = o8[p * QB:(p + 1) * QB] / dens[p]
            o_ref[0, :, col * LANES:(col + 1) * LANES] = (o * _silu(tile(g_ref[0], col))).astype(o_ref.dtype)


def _swa(z, sink, ctx, nkv):
    nb, t, ncol = z.shape
    kvw = nkv * HEAD
    w = (ncol - 2 * kvw) // 2
    assert t % QB == 0 and ctx % QB == 0 and w % kvw == 0
    nblk, ncb = t // QB, ctx // QB
    kcol, vcol = 2 * w // kvw, 2 * w // kvw + 1
    cos, sin = _rope_tables(t, ctx)
    prev = lambda i: jnp.maximum(i - 1, 0)
    nxt = lambda i: jnp.minimum(i + 1, nblk - 1)
    wide = lambda col: pl.BlockSpec((1, QB, w), lambda b, i: (b, i, col))
    kv = lambda col, f: pl.BlockSpec((1, QB, kvw), lambda b, i: (b, f(i), col))
    kvc = lambda col: pl.BlockSpec((1, ctx, kvw), lambda b, i: (b, 0, col))
    tab = lambda f: pl.BlockSpec((QB, LANES), lambda b, i: (f(i), 0))
    same = lambda i: i
    return pl.pallas_call(
        functools.partial(_swa_kernel, ncb=ncb, seq=t - ctx, ngroups=nkv),
        grid=(nb, nblk),
        in_specs=[pl.BlockSpec(memory_space=pltpu.SMEM), wide(0), wide(1), kvc(kcol), kvc(vcol),
                  kv(kcol, prev), kv(kcol, same), kv(kcol, nxt), kv(vcol, prev), kv(vcol, same), kv(vcol, nxt),
                  tab(same), tab(same), tab(prev), tab(prev), tab(nxt), tab(nxt)],
        out_specs=wide(0),
        out_shape=jax.ShapeDtypeStruct((nb, t, w), BF16),
        compiler_params=_cparams(("parallel", "arbitrary")),
        name="swa",
    )(sink, z, z, z, z, z, z, z, z, z, z, cos, sin, cos, sin, cos, sin)


C_GN_EPS = 64e-5
C_LORA_PAD = 128


def _prepc_kernel(x_ref, xp_ref, xn_ref, nw_ref, shl_ref, scl_ref, shc_ref, scc_ref, mu_ref, o_ref, scr,
                  *, tm, ctx, t_all):
    base = pl.program_id(1) * tm
    nw = nw_ref[...]
    mods = (scl_ref[0], shl_ref[0], scc_ref[0], shc_ref[0])
    row = base + lax.broadcasted_iota(jnp.int32, (tm, 1), 0)
    r8 = lax.broadcasted_iota(jnp.int32, (8, 1), 0)
    h = _normmod(x_ref[0], nw, row < ctx, *mods)
    scr[0:8, :] = _normmod(xp_ref[0], nw, (base - 8 + r8) < ctx, *mods)
    scr[8:tm + 8, :] = h
    scr[tm + 8:tm + 16, :] = _normmod(xn_ref[0], nw, (base + tm + r8) < ctx, *mods)
    has_prev = jnp.logical_and(row != 0, row != ctx)
    has_next = jnp.logical_and(row != ctx - 1, row != t_all - 1)
    xx = 0.5 * (jnp.where(has_prev, scr[7:tm + 7, :], 0.0) + jnp.where(has_next, scr[9:tm + 9, :], 0.0)) - h
    for n in range(o_ref.shape[0]):
        o_ref[n, 0] = (h + xx * mu_ref[n:n + 1, :]).astype(o_ref.dtype)


def _prepc(xs, nw, mod, mu, ctx):
    nb, t, d = xs.shape
    tm = _tile(t, 264)
    nmix = mu.shape[0]
    last8 = t // 8 - 1
    return pl.pallas_call(
        functools.partial(_prepc_kernel, tm=tm, ctx=ctx, t_all=t),
        grid=(nb, t // tm),
        in_specs=[pl.BlockSpec((1, tm, d), lambda b, i: (b, i, 0)),
                  pl.BlockSpec((1, 8, d), lambda b, i: (b, jnp.maximum(i * (tm // 8) - 1, 0), 0)),
                  pl.BlockSpec((1, 8, d), lambda b, i: (b, jnp.minimum((i + 1) * (tm // 8), last8), 0)),
                  pl.BlockSpec((1, d), lambda b, i: (0, 0))] + _mod_specs(d, nb) +
                 [pl.BlockSpec((nmix, d), lambda b, i: (0, 0))],
        out_specs=pl.BlockSpec((nmix, 1, tm, d), lambda b, i: (0, b, i, 0)),
        out_shape=jax.ShapeDtypeStruct((nmix, nb, t, d), BF16),
        scratch_shapes=[pltpu.VMEM((tm + 16, d), F32)],
        compiler_params=_cparams(("parallel", "parallel")),
        name="prep_rwkv",
    )(xs, xs, xs, nw.reshape(1, d), mod, mod, mod, mod, mu)


def _lora_kernel(xw_ref, xa_ref, w1_ref, a1_ref, w2_ref, a2_ref, w0_ref, a0_ref, lw_ref, ic_ref):
    t1 = jnp.tanh(_dot(xw_ref[0, 0], w1_ref[...]))
    t2 = _dot(xa_ref[0, 0], a1_ref[...])
    for d in range(2):
        sl = slice(d * C_LORA_PAD, (d + 1) * C_LORA_PAD)
        nx = -(w0_ref[d] + _dot(t1[:, sl], w2_ref[d]))
        w_log = -(jnp.maximum(nx, 0.0) + jnp.log(1.0 + jnp.exp(-jnp.abs(nx)))) - 0.5
        lw_ref[d, 0] = -jnp.exp(w_log)
        ic_ref[d, 0] = _sigmoid(a0_ref[d] + _dot(t2[:, sl], a2_ref[d]))


def _pad_lora(w_in, w_out):
    r = w_in.shape[2]
    a = jnp.pad(w_in, ((0, 0), (0, 0), (0, C_LORA_PAD - r)))
    a = jnp.concatenate([a[0], a[1]], axis=1).astype(BF16)
    b = jnp.pad(w_out, ((0, 0), (0, C_LORA_PAD - r), (0, 0))).astype(BF16)
    return a, b


def _lora(mix, w0, w1, w2, a0, a1, a2):
    _, nb, t, d = mix.shape
    w = w0.shape[1]
    tm = _tile(t, 264)
    w1p, w2p = _pad_lora(w1, w2)
    a1p, a2p = _pad_lora(a1, a2)
    full = lambda shape: pl.BlockSpec(shape, lambda b, i: (0,) * len(shape))
    out = pl.BlockSpec((2, 1, tm, w), lambda b, i: (0, b, i, 0))
    return pl.pallas_call(
        _lora_kernel,
        grid=(nb, t // tm),
        in_specs=[pl.BlockSpec((1, 1, tm, d), lambda b, i: (4, b, i, 0)),
                  pl.BlockSpec((1, 1, tm, d), lambda b, i: (5, b, i, 0)),
                  full(w1p.shape), full(a1p.shape), full(w2p.shape), full(a2p.shape),
                  full((2, 1, w)), full((2, 1, w))],
        out_specs=[out, out],
        out_shape=[jax.ShapeDtypeStruct((2, nb, t, w), F32)] * 2,
        compiler_params=_cparams(("parallel", "parallel")),
        name="rwkv_lora",
    )(mix, mix, w1p, a1p, w2p, a2p, w0.reshape(2, 1, w), a0.reshape(2, 1, w))


RWKV_PAIRS_PER_STEP = 8


def _rwkv_consts(reverse):
    t2 = lax.broadcasted_iota(jnp.int32, (CHUNK, LANES), 0)
    s2 = lax.broadcasted_iota(jnp.int32, (CHUNK, LANES), 1) % CHUNK
    strict = (s2 > t2) if reverse else (s2 < t2)
    incl = (s2 >= t2) if reverse else (s2 <= t2)
    eye = jnp.where(t2 == s2, 1.0, 0.0)
    left = lax.broadcasted_iota(jnp.int32, (1, LANES), 1) < HEAD
    vi = lax.broadcasted_iota(jnp.int32, (LANES, LANES), 0) < HEAD
    ki = lax.broadcasted_iota(jnp.int32, (LANES, LANES), 1) < HEAD
    return strict, incl, eye, left, vi == ki


def _stack(x, left):
    x = x.astype(BF16)
    zero = jnp.zeros_like(x)
    return jnp.concatenate([jnp.where(left, x, zero), jnp.where(left, zero, x)], axis=0)


def _rwkv_prepare(r, k, v, lw, ic, kkw, kaw, rkw, consts, reverse):
    _, _, _, left, _ = consts
    stack = functools.partial(_stack, left=left)

    def segsum(x):
        sl = jnp.sum(jnp.where(left, x, 0.0), axis=-1, keepdims=True)
        sr = jnp.sum(jnp.where(left, 0.0, x), axis=-1, keepdims=True)
        return jnp.where(left, sl, sr)

    kx = k * kkw
    kk = kx / jnp.maximum(jnp.sqrt(segsum(kx * kx)), 1e-12)
    kd = k * (1.0 + (ic - 1.0) * kaw)
    b = kk * ic
    bonus = segsum(r * kd * rkw) * v
    ti = lax.broadcasted_iota(jnp.int32, (CHUNK, CHUNK), 0)
    si = lax.broadcasted_iota(jnp.int32, (CHUNK, CHUNK), 1)
    lam = _dot_sel(((si >= ti) if reverse else (si <= ti)).astype(BF16), lw)
    end = 0 if reverse else CHUNK - 1
    lam_c = lam[end:end + 1, :]
    einv = jnp.exp(-lam)
    ebar = jnp.exp(lam_c - lam)
    at = (-kk * jnp.exp(lam - lw)).astype(BF16)
    rt = (r * jnp.exp(lam)).astype(BF16)
    return dict(at=at, rt=rt, ar=jnp.concatenate([at, rt], axis=0),
                kb=jnp.concatenate([stack(kd * einv), stack(b * einv)], axis=0),
                v=v.astype(BF16), vst=stack(v), kbar=(kd * ebar).astype(BF16), bbar=(b * ebar).astype(BF16),
                dec=jnp.exp(lam_c), bonus=bonus)


def _inv_unit_lower(ns, eye, left):
    stack = functools.partial(_stack, left=left)
    ps = [eye + n for n in ns]
    ms = [_dot(n, stack(n)) for n in ns]
    for j in range(CHUNK.bit_length() - 4):
        both = [_dot(jnp.concatenate([m.astype(BF16), p.astype(BF16)], axis=0), stack(m)) for m, p in zip(ms, ps)]
        ps = [p + b[CHUNK:] for p, b in zip(ps, both)]
        ms = [b[:CHUNK] for b in both]
    ps = [p + _dot(p, stack(m)) for p, m in zip(ps, ms)]
    out = []
    for n, p in zip(ns, ps):
        tb = p.astype(BF16)
        nh, nl = _split2(n)
        res = _dot(jnp.concatenate([nh, nl], axis=0), stack(tb))
        e = eye - tb.astype(F32) + (res[:CHUNK] + res[CHUNK:])
        out.append(tb.astype(F32) + _dot(tb, stack(e)))
    return out


def _rwkv_kernel(r_ref, k_ref, v_ref, lw_ref, ic_ref, kk_ref, ka_ref, rk_ref, y_ref, bon_ref, ht_ref,
                 *, tb, reverse, pps):
    @pl.when(pl.program_id(2) == 0)
    def _():
        ht_ref[...] = jnp.zeros_like(ht_ref)

    consts = _rwkv_consts(reverse)
    strict, incl, eye, left, same_head = consts
    stack = functools.partial(_stack, left=left)
    nch = tb // CHUNK
    order = list(range(nch - 1, -1, -1) if reverse else range(nch))
    items = [(p, c) for p in range(pps) for c in order]

    def blk(ref, p, c):
        return ref[0, c * CHUNK:(c + 1) * CHUNK, p * LANES:(p + 1) * LANES]

    def blkd(ref, p, c):
        return ref[0, 0, c * CHUNK:(c + 1) * CHUNK, p * LANES:(p + 1) * LANES]

    def par(ref, p):
        return ref[:, p * LANES:(p + 1) * LANES]

    pre = [_rwkv_prepare(blk(r_ref, p, c), blk(k_ref, p, c), blk(v_ref, p, c), blkd(lw_ref, p, c),
                         blkd(ic_ref, p, c), par(kk_ref, p), par(ka_ref, p), par(rk_ref, p), consts, reverse)
           for p, c in items]
    gs = [_dot(x["ar"], x["kb"], 1, 1) for x in pre]
    a_ak = [jnp.where(strict, g[:CHUNK, :LANES], 0.0).astype(BF16) for g in gs]
    a_rk = [jnp.where(incl, g[CHUNK:, :LANES], 0.0).astype(BF16) for g in gs]
    a_rb = [jnp.where(incl, g[CHUNK:, LANES:], 0.0).astype(BF16) for g in gs]
    tinv = _inv_unit_lower([jnp.where(strict, g[:CHUNK, LANES:], 0.0) for g in gs], eye, left)
    av = [_dot(jnp.concatenate([ak, rk], axis=0), x["vst"]) for ak, rk, x in zip(a_ak, a_rk, pre)]
    wu = [_dot(t, jnp.concatenate([stack(x["at"]), stack(a[:CHUNK])], axis=1)).astype(BF16)
          for t, x, a in zip(tinv, pre, av)]
    ry = [_dot(rb, jnp.concatenate([stack(w[:, :LANES]), stack(w[:, LANES:])], axis=1)) for rb, w in zip(a_rb, wu)]
    rw = [(x["rt"].astype(F32) + y[:, :LANES]).astype(BF16) for x, y in zip(pre, ry)]
    y0 = [a[CHUNK:] + y[:, LANES:] for a, y in zip(av, ry)]
    ft = [jnp.where(same_head, _dot(jnp.concatenate([x["v"], w[:, LANES:]], axis=0),
                                    jnp.concatenate([x["kbar"], x["bbar"]], axis=0), 0, 0), 0.0)
          for x, w in zip(pre, wu)]
    gm = [jnp.where(same_head, _dot(x["bbar"], w[:, :LANES], 0, 0), 0.0).astype(BF16) for x, w in zip(pre, wu)]

    hts = [ht_ref[p] for p in range(pps)]
    for step in range(nch):
        for p in range(pps):
            n = p * nch + step
            c = items[n][1]
            ht = hts[p]
            htb = ht.astype(BF16)
            y = y0[n] + _dot(rw[n], htb, 1, 1)
            y_ref[0, c * CHUNK:(c + 1) * CHUNK, p * LANES:(p + 1) * LANES] = y.astype(y_ref.dtype)
            bon_ref[0, c * CHUNK:(c + 1) * CHUNK, p * LANES:(p + 1) * LANES] = pre[n]["bonus"].astype(bon_ref.dtype)
            hts[p] = ht * pre[n]["dec"] + ft[n] + _dot(htb, gm[n], 1, 1)
    for p in range(pps):
        ht_ref[p] = hts[p]


def _rwkv_scan(r, k, v, lw, ic, k_k, k_a, r_k, ctx, reverse, tb=None):
    nb, t, w = r.shape
    tb = tb or _tile(ctx, 256, CHUNK)
    assert t % tb == 0 and ctx % tb == 0
    nblk, ncb = t // tb, ctx // tb
    tmap = _time_block_map(nblk, ncb, reverse)
    d = 1 if reverse else 0
    pps = RWKV_PAIRS_PER_STEP
    wide = pps * LANES
    assert w % wide == 0
    blk = pl.BlockSpec((1, tb, wide), lambda b, p, i: (b, tmap(i), p))
    blkd = pl.BlockSpec((1, 1, tb, wide), lambda b, p, i: (d, b, tmap(i), p))
    par = pl.BlockSpec((1, wide), lambda b, p, i: (0, p))
    return pl.pallas_call(
        functools.partial(_rwkv_kernel, tb=tb, reverse=reverse, pps=pps),
        grid=(nb, w // wide, nblk),
        in_specs=[blk, blk, blk, blkd, blkd, par, par, par],
        out_specs=[blk, blk],
        out_shape=[jax.ShapeDtypeStruct((nb, t, w), BF16)] * 2,
        scratch_shapes=[pltpu.VMEM((pps, LANES, LANES), F32)],
        compiler_params=_cparams(("parallel", "parallel", "arbitrary")),
        name="rwkv_bwd" if reverse else "rwkv_fwd",
    )(r, k, v, lw, ic, k_k.reshape(1, w), k_a.reshape(1, w), r_k.reshape(1, w))


def _rwkv_out_kernel(yf_ref, yb_ref, bf_ref, bb_ref, g_ref, lnw_ref, lnb_ref, w_ref, x_ref, gl_ref, gc_ref, o_ref,
                     lhs_ref, *, tm, ctx):
    @pl.when(pl.program_id(2) == 0)
    def _():
        left = lax.broadcasted_iota(jnp.int32, (1, LANES), 1) < HEAD

        def segmean(x):
            sl = jnp.sum(jnp.where(left, x, 0.0), axis=-1, keepdims=True)
            sr = jnp.sum(jnp.where(left, 0.0, x), axis=-1, keepdims=True)
            return jnp.where(left, sl, sr) * (1.0 / HEAD)

        for p in range(yf_ref.shape[2] // LANES):
            sl = slice(p * LANES, (p + 1) * LANES)
            y = yf_ref[0, :, sl].astype(F32) + yb_ref[0, :, sl].astype(F32)
            dlt = y - segmean(y)
            zn = dlt * lax.rsqrt(segmean(dlt * dlt) + C_GN_EPS)
            o = (zn * lnw_ref[:, sl] + lnb_ref[:, sl]
                 + bf_ref[0, :, sl].astype(F32) + bb_ref[0, :, sl].astype(F32))
            lhs_ref[:, sl] = (o * _silu(g_ref[0, :, sl])).astype(lhs_ref.dtype)

    _residual_out(lhs_ref, w_ref, x_ref, gl_ref, gc_ref, o_ref, tm, ctx)


def _rwkv_out(y_f, y_b, bon_f, bon_b, gate, ln_w, ln_b, w_out, xs, mod, ctx):
    nb, t, w = y_f.shape
    n = w_out.shape[1]
    tm = _tile(t, 528)
    tn = _tile(n, 512, 128)
    wide = pl.BlockSpec((1, tm, w), lambda b, i, j: (b, i, 0))
    par = pl.BlockSpec((1, w), lambda b, i, j: (0, 0))
    return pl.pallas_call(
        functools.partial(_rwkv_out_kernel, tm=tm, ctx=ctx),
        grid=(nb, t // tm, n // tn),
        in_specs=[wide] * 5 + [par, par] + _residual_specs(tm, tn, w, nb, 2 * n // tn),
        out_specs=pl.BlockSpec((1, tm, tn), lambda b, i, j: (b, i, j)),
        out_shape=jax.ShapeDtypeStruct((nb, t, n), F32),
        scratch_shapes=[pltpu.VMEM((tm, w), BF16)],
        compiler_params=_cparams(("parallel", "parallel", "arbitrary")),
        name="rwkv_out",
    )(y_f, y_b, bon_f, bon_b, gate, ln_w.reshape(1, w), ln_b.reshape(1, w), w_out, xs, mod, mod)


def _rwkv_layer(mix, w_in, w0, w1, w2, a0, a1, a2, k_k, k_a, r_k, ln_w, ln_b, w_out, xs, mod, ctx, tb=None):
    r, k, v, gate = (_matmul(mix, w_in[n].astype(BF16), sel=n) for n in range(4))
    lw, ic = _lora(mix, w0, w1, w2, a0, a1, a2)
    y_f, bon_f = _rwkv_scan(r, k, v, lw, ic, k_k, k_a, r_k, ctx, False, tb)
    y_b, bon_b = _rwkv_scan(r, k, v, lw, ic, k_k, k_a, r_k, ctx, True, tb)
    return _rwkv_out(y_f, y_b, bon_f, bon_b, gate, ln_w, ln_b, w_out, xs, mod, ctx)


def _swa_mix(h, w_in, sink, ctx, width):
    kv2 = w_in.shape[1] - 2 * width
    w_perm = jnp.concatenate([w_in[:, :width], w_in[:, width + kv2:], w_in[:, width:width + kv2]], axis=1)
    z = _matmul(h, w_perm.astype(BF16))
    return _swa(z, sink, ctx, kv2 // 2 // HEAD)


def _final_norm_kernel(x_ref, w_ref, o_ref):
    x = x_ref[0]
    ms = jnp.mean(x * x, axis=-1, keepdims=True)
    o_ref[0] = x * lax.rsqrt(ms + RMS_EPS) * w_ref[...]


def _final_norm(xs, w, ctx):
    nb, t, d = xs.shape
    seq = t - ctx
    tm = _tile(math.gcd(ctx, seq), 1024)
    off = ctx // tm
    return pl.pallas_call(
        _final_norm_kernel,
        grid=(nb, seq // tm),
        in_specs=[pl.BlockSpec((1, tm, d), lambda b, i: (b, i + off, 0)),
                  pl.BlockSpec((1, d), lambda b, i: (0, 0))],
        out_specs=pl.BlockSpec((1, tm, d), lambda b, i: (b, i, 0)),
        out_shape=jax.ShapeDtypeStruct((nb, seq, d), F32),
        compiler_params=_cparams(("parallel", "parallel")),
        name="final_norm",
    )(xs, w.reshape(1, d))


def kernel(x, c, ctx, c_ctx, norm_w, mod_w, mod_b, a_w_in, a_lb_raw, a_onorm_w, a_w_out, b_w_in, b_sink, b_w_out,
           c_mu, c_w_in, c_w0, c_w1, c_w2, c_a0, c_a1, c_a2, c_k_k, c_k_a, c_r_k, c_ln_w, c_ln_b, c_w_out,
           final_norm_w):
    nb, seq, d = x.shape
    nctx = ctx.shape[1]
    depth = norm_w.shape[0]
    xs = jnp.concatenate([ctx, x], axis=1)
    cvec = jnp.zeros((8, d), F32).at[:nb].set(c).at[nb].set(c_ctx)
    mod_all = _modulation(cvec, mod_w, mod_b)
    lb_all = jnp.cumsum(jax.nn.softmax(a_lb_raw.astype(F32), axis=0), axis=0)
    lb_all = lb_all - lb_all[0]
    for i in range(depth):
        j, kind = i // 3, i % 3
        mod = mod_all[i].reshape(8, 1, 3 * d)
        if kind == 0:
            h = _prep(xs, norm_w[i], mod, nctx)
            xs = _hgrn2_layer(h, a_w_in[j].astype(BF16), lb_all[j], a_onorm_w[j], a_w_out[j].astype(BF16),
                              xs, mod, nctx)
        elif kind == 1:
            h = _prep(xs, norm_w[i], mod, nctx)
            m = _swa_mix(h, b_w_in[j], b_sink[j], nctx, b_w_out.shape[1])
            xs = _matmul_residual(m, b_w_out[j].astype(BF16), xs, mod, nctx)
        else:
            mix = _prepc(xs, norm_w[i], mod, c_mu[j], nctx)
            xs = _rwkv_layer(mix, c_w_in[j], c_w0[j], c_w1[j], c_w2[j], c_a0[j], c_a1[j], c_a2[j],
                             c_k_k[j], c_k_a[j], c_r_k[j], c_ln_w[j], c_ln_b[j], c_w_out[j].astype(BF16),
                             xs, mod, nctx)
    return _final_norm(xs, final_norm_w, nctx)
```

```python
import functools
import math

import jax
import jax.numpy as jnp
from jax import lax
from jax.experimental import pallas as pl
from jax.experimental.pallas import tpu as pltpu

F32 = jnp.float32
BF16 = jnp.bfloat16
NEG_INF = float("-inf")

RMS_EPS = 1e-6
CHUNK = 64
SUB = 16
NSUB = CHUNK // SUB
A_DK = 128
SUBLANES = 8
VMEM_LIMIT = 56 * 1024 * 1024


def _cparams(sem):
    return pltpu.CompilerParams(dimension_semantics=sem, vmem_limit_bytes=VMEM_LIMIT)


def _tile(n, target, mult=8):
    best = None
    for t in range(mult, min(n, target) + 1, mult):
        if n % t == 0:
            best = t
    assert best is not None, (n, target, mult)
    return best


def _dot(a, b, ca=1, cb=0):
    return lax.dot_general(a.astype(BF16), b.astype(BF16), (((ca,), (cb,)), ((), ())),
                           preferred_element_type=F32)


def _split2(x):
    hi = x.astype(BF16)
    lo = (x - hi.astype(F32)).astype(BF16)
    return hi, lo


def _split3(x):
    hi = x.astype(BF16)
    r = x - hi.astype(F32)
    mid = r.astype(BF16)
    lo = (r - mid.astype(F32)).astype(BF16)
    return hi, mid, lo


def _dot_sel(sel, x):
    hi, mid, lo = _split3(x)
    return _dot(sel, hi) + _dot(sel, mid) + _dot(sel, lo)


def _dot_hi(a, b, ca=1, cb=0):
    ah, al = _split2(a)
    bh, bl = _split2(b)
    return _dot(ah, bh, ca, cb) + _dot(ah, bl, ca, cb) + _dot(al, bh, ca, cb)


def _cumsum_rows(x, reverse):
    ntile = x.shape[0] // SUBLANES
    sub = lax.broadcasted_iota(jnp.int32, (SUBLANES, 1), 0)
    tiles = [x[j * SUBLANES:(j + 1) * SUBLANES] for j in range(ntile)]
    for s in (1, 2, 4):
        if reverse:
            tiles = [y + jnp.where(sub < SUBLANES - s, pltpu.roll(y, SUBLANES - s, 0), 0.0) for y in tiles]
        else:
            tiles = [y + jnp.where(sub >= s, pltpu.roll(y, s, 0), 0.0) for y in tiles]
    edge = 0 if reverse else SUBLANES - 1
    order = range(ntile - 1, -1, -1) if reverse else range(ntile)
    carry = None
    out = [None] * ntile
    for j in order:
        out[j] = tiles[j] if carry is None else tiles[j] + carry
        total = tiles[j][edge:edge + 1, :]
        carry = total if carry is None else carry + total
    return jnp.concatenate(out, axis=0)


def _sigmoid(x):
    return 1.0 / (1.0 + jnp.exp(-x))


def _silu(x):
    return x * _sigmoid(x)


def _mod_kernel(c_ref, w_ref, b_ref, o_ref):
    s = _silu(c_ref[...])
    o_ref[0] = _dot_hi(s, w_ref[0]) + b_ref[0]


def _modulation(cvec, mod_w, mod_b):
    depth, d, n = mod_w.shape
    tn = _tile(n, 512, 128)
    return pl.pallas_call(
        _mod_kernel,
        grid=(depth, n // tn),
        in_specs=[pl.BlockSpec((8, d), lambda i, j: (0, 0)),
                  pl.BlockSpec((1, d, tn), lambda i, j: (i, 0, j)),
                  pl.BlockSpec((1, 1, tn), lambda i, j: (i, 0, j))],
        out_specs=pl.BlockSpec((1, 8, tn), lambda i, j: (i, 0, j)),
        out_shape=jax.ShapeDtypeStruct((depth, 8, n), F32),
        compiler_params=_cparams(("parallel", "parallel")),
        name="modulation",
    )(cvec, mod_w, mod_b.reshape(depth, 1, n))


def _normmod(x, nw, is_ctx, scl, shl, scc, shc):
    ms = jnp.mean(x * x, axis=-1, keepdims=True)
    y = x * lax.rsqrt(ms + RMS_EPS) * nw
    sc = jnp.where(is_ctx, scc, scl)
    sh = jnp.where(is_ctx, shc, shl)
    return y * (1.0 + sc) + sh


def _prep_kernel(x_ref, nw_ref, shl_ref, scl_ref, shc_ref, scc_ref, o_ref, *, tm, ctx):
    row = pl.program_id(1) * tm + lax.broadcasted_iota(jnp.int32, (tm, 1), 0)
    h = _normmod(x_ref[0], nw_ref[...], row < ctx, scl_ref[0], shl_ref[0], scc_ref[0], shc_ref[0])
    o_ref[0] = h.astype(o_ref.dtype)


def _mod_specs(d, nb):
    return [pl.BlockSpec((1, 1, d), lambda b, i, *_: (b, 0, 0)),
            pl.BlockSpec((1, 1, d), lambda b, i, *_: (b, 0, 1)),
            pl.BlockSpec((1, 1, d), lambda b, i, *_: (nb, 0, 0)),
            pl.BlockSpec((1, 1, d), lambda b, i, *_: (nb, 0, 1))]


def _prep(xs, nw, mod, ctx):
    nb, t, d = xs.shape
    tm = _tile(t, 1056)
    return pl.pallas_call(
        functools.partial(_prep_kernel, tm=tm, ctx=ctx),
        grid=(nb, t // tm),
        in_specs=[pl.BlockSpec((1, tm, d), lambda b, i: (b, i, 0)),
                  pl.BlockSpec((1, d), lambda b, i: (0, 0))] + _mod_specs(d, nb),
        out_specs=pl.BlockSpec((1, tm, d), lambda b, i: (b, i, 0)),
        out_shape=jax.ShapeDtypeStruct((nb, t, d), BF16),
        compiler_params=_cparams(("parallel", "parallel")),
        name="prep",
    )(xs, nw.reshape(1, d), mod, mod, mod, mod)


def _mm_kernel(a_ref, w_ref, o_ref):
    o_ref[0] = _dot(a_ref[0], w_ref[...]).astype(o_ref.dtype)


def _mm_sel_kernel(a_ref, w_ref, o_ref):
    o_ref[0] = _dot(a_ref[0, 0], w_ref[...]).astype(o_ref.dtype)


def _matmul(a, w, out_dtype=F32, sel=None):
    nb, t, k = a.shape[-3:]
    n = w.shape[1]
    tm = _tile(t, 1056)
    tn = _tile(n, 512, 128)
    if sel is None:
        a_spec = pl.BlockSpec((1, tm, k), lambda b, i, j: (b, i, 0))
    else:
        a_spec = pl.BlockSpec((1, 1, tm, k), lambda b, i, j: (sel, b, i, 0))
    return pl.pallas_call(
        _mm_kernel if sel is None else _mm_sel_kernel,
        grid=(nb, t // tm, n // tn),
        in_specs=[a_spec,
                  pl.BlockSpec((k, tn), lambda b, i, j: (0, j))],
        out_specs=pl.BlockSpec((1, tm, tn), lambda b, i, j: (b, i, j)),
        out_shape=jax.ShapeDtypeStruct((nb, t, n), out_dtype),
        compiler_params=_cparams(("parallel", "parallel", "arbitrary")),
        name="matmul",
    )(a, w)


def _mm_res_kernel(a_ref, w_ref, x_ref, gl_ref, gc_ref, o_ref, *, tm, ctx):
    row = pl.program_id(1) * tm + lax.broadcasted_iota(jnp.int32, (tm, 1), 0)
    g = jnp.where(row < ctx, gc_ref[0], gl_ref[0])
    o_ref[0] = x_ref[0] + g * _dot(a_ref[0], w_ref[...])


def _matmul_residual(a, w, xs, mod, ctx):
    nb, t, k = a.shape
    n = w.shape[1]
    tm = _tile(t, 1056)
    tn = _tile(n, 512, 128)
    goff = 2 * n // tn
    return pl.pallas_call(
        functools.partial(_mm_res_kernel, tm=tm, ctx=ctx),
        grid=(nb, t // tm, n // tn),
        in_specs=[pl.BlockSpec((1, tm, k), lambda b, i, j: (b, i, 0)),
                  pl.BlockSpec((k, tn), lambda b, i, j: (0, j)),
                  pl.BlockSpec((1, tm, tn), lambda b, i, j: (b, i, j)),
                  pl.BlockSpec((1, 1, tn), lambda b, i, j: (b, 0, goff + j)),
                  pl.BlockSpec((1, 1, tn), lambda b, i, j: (nb, 0, goff + j))],
        out_specs=pl.BlockSpec((1, tm, tn), lambda b, i, j: (b, i, j)),
        out_shape=jax.ShapeDtypeStruct((nb, t, n), F32),
        compiler_params=_cparams(("parallel", "parallel", "arbitrary")),
        name="matmul_residual",
    )(a, w, xs, mod, mod)


def _time_block_map(nblk, nctx_blk, reverse):
    if not reverse:
        return lambda i: i
    return lambda i: jnp.where(i < nctx_blk, nctx_blk - 1 - i, nblk - 1 - i + nctx_blk)


NLEVEL = CHUNK.bit_length() - 1
GLA_HEADS_PER_STEP = 8


def _gla_gates(z, llb, l1m):
    ls = jnp.minimum(z, 0.0) - jnp.log(1.0 + jnp.exp(-jnp.abs(z)))
    if llb is None:
        return ls
    c2 = l1m + ls
    return jnp.maximum(llb, c2) + jnp.log(1.0 + jnp.exp(-jnp.abs(llb - c2)))


def _gla_consts(reverse):
    r = lax.broadcasted_iota(jnp.int32, (CHUNK, CHUNK), 0)
    i = lax.broadcasted_iota(jnp.int32, (CHUNK, CHUNK), 1)
    tr = (CHUNK - 1 - r) if reverse else r
    ti = (CHUNK - 1 - i) if reverse else i
    masks = []
    for lvl in range(NLEVEL):
        same = (tr >> (lvl + 1)) == (ti >> (lvl + 1))
        pair = jnp.logical_and(jnp.logical_and(same, ((tr >> lvl) & 1) == 1), ((ti >> lvl) & 1) == 0)
        masks.append(jnp.where(pair, 1.0, 0.0))
    masks.append(jnp.where(r == i, 1.0, 0.0))
    return masks


def _boundary_rows(b, lvl, sub, reverse):
    grp, half = 1 << (lvl + 1), 1 << lvl
    row_of = (lambda tau: CHUNK - 1 - tau) if reverse else (lambda tau: tau)
    tiles = []
    for a in range(CHUNK // SUBLANES):
        taus = sorted(row_of(a * SUBLANES + u) for u in range(SUBLANES))
        picks = {}
        for tau in taus:
            src = row_of((tau // grp) * grp + half - 1)
            picks.setdefault(src, []).append(row_of(tau) - a * SUBLANES)
        tile = None
        for src, subs in picks.items():
            piece = jnp.broadcast_to(b[src:src + 1, :], (SUBLANES, b.shape[1]))
            if tile is None:
                tile = piece
            else:
                tile = jnp.where(jnp.logical_and(sub >= min(subs), sub <= max(subs)), piece, tile)
        tiles.append(tile)
    return jnp.concatenate(tiles, axis=0)


def _gla_intra(q, v, g, k, mask_ref, reverse):
    sub = lax.broadcasted_iota(jnp.int32, (SUBLANES, 1), 0)
    b = _cumsum_rows(g, reverse)
    end = 0 if reverse else CHUNK - 1
    b_end = b[end:end + 1, :]
    qt = q * jnp.exp(b)
    kbar = k * jnp.exp(b_end - b)
    a = mask_ref[NLEVEL] * jnp.sum(q * k, axis=-1, keepdims=True)
    k_prev = pltpu.roll(k, CHUNK - 1 if reverse else 1, 0)
    a = a + mask_ref[0] * jnp.sum(q * (1.0 - k) * k_prev, axis=-1, keepdims=True)
    for lvl in range(1, NLEVEL):
        e = jnp.exp(-jnp.abs(b - _boundary_rows(b, lvl, sub, reverse)))
        a = a + mask_ref[lvl] * _dot(q * e, k * e, 1, 1)
    return _dot(a, v), qt.astype(BF16), _dot(v, kbar, 0, 0), jnp.exp(b_end)


def _gla_kernel(q_ref, v_ref, g_ref, o_ref, st_ref, mask_ref, *, tb, reverse, scale, hps):
    @pl.when(pl.program_id(2) == 0)
    def _():
        st_ref[...] = jnp.zeros_like(st_ref)
        for n, m in enumerate(_gla_consts(reverse)):
            mask_ref[n] = m

    consts = mask_ref
    nch = tb // CHUNK
    order = list(range(nch - 1, -1, -1) if reverse else range(nch))
    items = [(h, c) for h in range(hps) for c in order]

    def blk(ref, h, c):
        return ref[0, c * CHUNK:(c + 1) * CHUNK, h * A_DK:(h + 1) * A_DK]

    gs = [blk(g_ref, h, c) for h, c in items]
    intra = [_gla_intra(blk(q_ref, h, c) * scale, blk(v_ref, h, c), g, 1.0 - jnp.exp(g), consts, reverse)
             for (h, c), g in zip(items, gs)]
    for h in range(hps):
        st = st_ref[h]
        states = []
        for n, (hh, c) in enumerate(items):
            if hh == h:
                states.append((c, n, st))
                st = st * intra[n][3] + intra[n][2]
        st_ref[h] = st
        for c, n, st_in in states:
            o = intra[n][0] + _dot(intra[n][1], st_in, 1, 1)
            o_ref[0, c * CHUNK:(c + 1) * CHUNK, h * A_DK:(h + 1) * A_DK] = o.astype(o_ref.dtype)


def _gla(z, nh, ctx, reverse, tb=None):
    nb, t, _ = z.shape
    tb = tb or _tile(ctx, 256, CHUNK)
    assert t % tb == 0 and ctx % tb == 0
    nblk, ncb = t // tb, ctx // tb
    tmap = _time_block_map(nblk, ncb, reverse)
    hps = GLA_HEADS_PER_STEP
    assert nh % hps == 0
    ng = nh // hps
    zcol = (3 if reverse else 2) * ng
    wide = hps * A_DK

    def spec(col0):
        return pl.BlockSpec((1, tb, wide), lambda b, h, i: (b, tmap(i), col0 + h))

    return pl.pallas_call(
        functools.partial(_gla_kernel, tb=tb, reverse=reverse, scale=A_DK ** -0.5, hps=hps),
        grid=(nb, ng, nblk),
        in_specs=[spec(0), spec(ng), spec(zcol)],
        out_specs=pl.BlockSpec((1, tb, wide), lambda b, h, i: (b, tmap(i), h)),
        out_shape=jax.ShapeDtypeStruct((nb, t, nh * A_DK), BF16),
        scratch_shapes=[pltpu.VMEM((hps, A_DK, A_DK), F32), pltpu.VMEM((NLEVEL + 1, CHUNK, CHUNK), F32)],
        compiler_params=_cparams(("parallel", "parallel", "arbitrary")),
        name="gla_bwd" if reverse else "gla_fwd",
    )(z, z, z)


BF16_ROWS = 16


def _hgrn2_in_kernel(a_ref, w_ref, *refs, lo, hi):
    o_ref = refs[-1]
    bounds = [r[...] for r in refs[:-1]] or [None, None]
    j = pl.program_id(2)
    is_gate = jnp.logical_and(j >= lo, j < hi)

    @pl.when(is_gate)
    def _():
        o_ref[0] = _gla_gates(_dot(a_ref[0], w_ref[...]), *bounds)

    @pl.when(jnp.logical_not(is_gate))
    def _():
        o_ref[0] = _dot(a_ref[0], w_ref[...])


def _hgrn2_in(h, w_in, lb, width):
    nb, t, k = h.shape
    n = w_in.shape[1]
    tm = _tile(t, 1056, BF16_ROWS)
    tn = _tile(width, 512, 128)
    per = width // tn
    lo, hi = 2 * per, 4 * per
    par = pl.BlockSpec((1, tn), lambda b, i, j: (0, jnp.where(jnp.logical_and(j >= lo, j < hi), (j - lo) % per, 0)))
    bounds = [] if lb is None else [jnp.log(lb).reshape(1, width), jnp.log1p(-lb).reshape(1, width)]
    return pl.pallas_call(
        functools.partial(_hgrn2_in_kernel, lo=lo, hi=hi),
        grid=(nb, t // tm, n // tn),
        in_specs=[pl.BlockSpec((1, tm, k), lambda b, i, j: (b, i, 0)),
                  pl.BlockSpec((k, tn), lambda b, i, j: (0, j))] + [par] * len(bounds),
        out_specs=pl.BlockSpec((1, tm, tn), lambda b, i, j: (b, i, j)),
        out_shape=jax.ShapeDtypeStruct((nb, t, n), F32),
        compiler_params=_cparams(("parallel", "parallel", "arbitrary")),
        name="hgrn2_in",
    )(h, w_in, *bounds)


def _residual_out(lhs, w_ref, x_ref, gl_ref, gc_ref, o_ref, tm, ctx):
    k = pl.program_id(2)
    part = _dot(lhs, w_ref[...])

    @pl.when(k == 0)
    def _():
        o_ref[0] = part

    @pl.when(k > 0)
    def _():
        o_ref[0] += part

    @pl.when(k == pl.num_programs(2) - 1)
    def _():
        row = pl.program_id(1) * tm + lax.broadcasted_iota(jnp.int32, (tm, 1), 0)
        g = jnp.where(row < ctx, gc_ref[0], gl_ref[0])
        o_ref[0] = x_ref[0] + g * o_ref[0]


def _hgrn2_out_kernel(of_ref, ob_ref, g_ref, nw_ref, w_ref, x_ref, gl_ref, gc_ref, o_ref, *, tm, ctx):
    parts = []
    for h in range(of_ref.shape[2] // A_DK):
        sl = slice(h * A_DK, (h + 1) * A_DK)
        y = of_ref[0, :, sl].astype(F32) + ob_ref[0, :, sl].astype(F32)
        ms = jnp.mean(y * y, axis=-1, keepdims=True)
        yn = y * lax.rsqrt(ms + RMS_EPS) * nw_ref[...]
        parts.append((yn * _silu(g_ref[0, :, sl])).astype(BF16))
    _residual_out(jnp.concatenate(parts, axis=1), w_ref, x_ref, gl_ref, gc_ref, o_ref, tm, ctx)


OUT_TK = 1024


def _residual_specs(tm, tk, n, nb):
    return [pl.BlockSpec((tk, n), lambda b, i, k: (k, 0)),
            pl.BlockSpec((1, tm, n), lambda b, i, k: (b, i, 0)),
            pl.BlockSpec((1, 1, n), lambda b, i, k: (b, 0, 2)),
            pl.BlockSpec((1, 1, n), lambda b, i, k: (nb, 0, 2))]


def _hgrn2_out(o_f, o_b, z, onorm_w, w_out, xs, mod, ctx):
    nb, t, w = o_f.shape
    n = w_out.shape[1]
    tm = _tile(t, 528, BF16_ROWS)
    tk = _tile(w, OUT_TK, A_DK)
    gate0 = 4 * (w // tk)
    sl = pl.BlockSpec((1, tm, tk), lambda b, i, k: (b, i, k))
    return pl.pallas_call(
        functools.partial(_hgrn2_out_kernel, tm=tm, ctx=ctx),
        grid=(nb, t // tm, w // tk),
        in_specs=[sl, sl, pl.BlockSpec((1, tm, tk), lambda b, i, k: (b, i, gate0 + k)),
                  pl.BlockSpec((1, A_DK), lambda b, i, k: (0, 0))] + _residual_specs(tm, tk, n, nb),
        out_specs=pl.BlockSpec((1, tm, n), lambda b, i, k: (b, i, 0)),
        out_shape=jax.ShapeDtypeStruct((nb, t, n), F32),
        compiler_params=_cparams(("parallel", "parallel", "arbitrary")),
        name="hgrn2_out",
    )(o_f, o_b, z, onorm_w.reshape(1, A_DK), w_out, xs, mod, mod)


def _hgrn2_layer(h, w_in, lb, onorm_w, w_out, xs, mod, ctx, tb=None):
    width = w_out.shape[0]
    z = _hgrn2_in(h, w_in, lb, width)
    o_f = _gla(z, width // A_DK, ctx, False, tb)
    o_b = _gla(z, width // A_DK, ctx, True, tb)
    return _hgrn2_out(o_f, o_b, z, onorm_w, w_out, xs, mod, ctx)


GRID_W = 64
ROPE_BASE = 10000.0
HEAD = 64
QB = 128
LANES = 128


def _rope_tables(t_all, ctx):
    quarter = HEAD // 4
    inv = ROPE_BASE ** (-jnp.arange(quarter, dtype=F32) / quarter)
    tl = jnp.arange(t_all - ctx)
    row = (tl // GRID_W).astype(F32)
    col = (tl % GRID_W).astype(F32)
    hdim = jnp.arange(LANES) % HEAD
    use_col = hdim >= 2 * quarter
    second = (hdim % (2 * quarter)) >= quarter
    pos = jnp.where(use_col[None, :], col[:, None], row[:, None])
    ang = pos * inv[hdim % quarter][None, :]
    cos = jnp.concatenate([jnp.ones((ctx, LANES), F32), jnp.cos(ang)], axis=0)
    sin = jnp.concatenate([jnp.zeros((ctx, LANES), F32),
                           jnp.where(second[None, :], jnp.sin(ang), -jnp.sin(ang))], axis=0)
    return cos, sin


def _rope(x, cos, sin, first):
    partner = jnp.where(first, pltpu.roll(x, LANES - HEAD // 4, 1), pltpu.roll(x, HEAD // 4, 1))
    return x * cos + partner * sin


def _swa_kernel(sink_ref, q_ref, g_ref, kc_ref, vc_ref, kp_ref, kq_ref, kn_ref, vp_ref, vq_ref, vn_ref,
                cq_ref, sq_ref, cp_ref, sp_ref, cn_ref, sn_ref, o_ref, *, ncb, seq, ngroups):
    i = pl.program_id(1)
    lane = lax.broadcasted_iota(jnp.int32, (1, LANES), 1)
    first = (lane % (HEAD // 2)) < (HEAD // 4)
    left = lane < HEAD
    ntile = kc_ref.shape[2] // LANES

    def tile(x, c):
        return x[:, c * LANES:(c + 1) * LANES]

    kwin = [(kp_ref[0], cp_ref[...], sp_ref[...]), (kq_ref[0], cq_ref[...], sq_ref[...]),
            (kn_ref[0], cn_ref[...], sn_ref[...])]
    kt = [jnp.concatenate([tile(kc_ref[0], c)] + [_rope(tile(kk, c), cs, sn, first) for kk, cs, sn in kwin], axis=0)
          for c in range(ntile)]
    vt = [jnp.concatenate([tile(vc_ref[0], c), tile(vp_ref[0], c), tile(vq_ref[0], c), tile(vn_ref[0], c)], axis=0)
          for c in range(ntile)]
    nk = kt[0].shape[0]
    nctx = nk - 3 * QB

    r = lax.broadcasted_iota(jnp.int32, (QB, QB), 0)
    cidx = lax.broadcasted_iota(jnp.int32, (QB, QB), 1)
    qblk = i - ncb
    open_if = lambda cond: jnp.where(cond, 0.0, NEG_INF)
    b_prev = open_if(cidx >= r) + open_if(qblk >= 1)
    b_cur = jnp.zeros((QB, QB), F32) + open_if(qblk >= 0)
    b_next = open_if(cidx <= r) + open_if(jnp.logical_and(qblk >= 0, (qblk + 2) * QB <= seq))
    bias = jnp.concatenate([jnp.zeros((QB, nctx), F32), b_prev, b_cur, b_next], axis=1)

    scale = HEAD ** -0.5
    heads_per_group = q_ref.shape[2] // HEAD // ngroups
    pairs = heads_per_group // 2
    for g in range(ngroups):
        c, even = g // 2, g % 2 == 0
        own = left if even else jnp.logical_not(left)
        k_own = jnp.where(own, kt[c], 0.0)
        v_own = jnp.where(own, vt[c], 0.0)
        k_swp = pltpu.roll(k_own, HEAD, 1)
        v_swp = pltpu.roll(v_own, HEAD, 1)
        kk2 = jnp.concatenate([k_own, k_swp] if even else [k_swp, k_own], axis=0).astype(BF16)
        vv2 = jnp.concatenate([v_own, v_swp] if even else [v_swp, v_own], axis=0).astype(BF16)
        q8 = jnp.concatenate(
            [_rope(tile(q_ref[0], g * pairs + p), cq_ref[...], sq_ref[...], first) * scale for p in range(pairs)],
            axis=0)
        s8 = _dot(q8, kk2, 1, 1)
        probs, dens = [], []
        for p in range(pairs):
            pe2, den2 = [], []
            for e in range(2):
                sk = sink_ref[g * heads_per_group + 2 * p + e]
                s = s8[p * QB:(p + 1) * QB, e * nk:(e + 1) * nk] + bias
                m = jnp.maximum(jnp.max(s, axis=-1, keepdims=True), sk)
                pe = jnp.exp(s - m)
                den2.append(jnp.sum(pe, axis=-1, keepdims=True) + jnp.exp(sk - m))
                pe2.append(pe.astype(BF16))
            probs.append(jnp.concatenate(pe2, axis=1))
            dens.append(jnp.where(left, den2[0], den2[1]))
        o8 = _dot(jnp.concatenate(probs, axis=0), vv2)
        for p in range(pairs):
            col = g * pairs + p
            o = o8[p * QB:(p + 1) * QB] / dens[p]
            o_ref[0, :, col * LANES:(col + 1) * LANES] = (o * _silu(tile(g_ref[0], col))).astype(o_ref.dtype)


def _swa(z, sink, ctx, nkv):
    nb, t, ncol = z.shape
    kvw = nkv * HEAD
    w = (ncol - 2 * kvw) // 2
    assert t % QB == 0 and ctx % QB == 0 and w % kvw == 0
    nblk, ncb = t // QB, ctx // QB
    kcol, vcol = 2 * w // kvw, 2 * w // kvw + 1
    cos, sin = _rope_tables(t, ctx)
    prev = lambda i: jnp.maximum(i - 1, 0)
    nxt = lambda i: jnp.minimum(i + 1, nblk - 1)
    wide = lambda col: pl.BlockSpec((1, QB, w), lambda b, i: (b, i, col))
    kv = lambda col, f: pl.BlockSpec((1, QB, kvw), lambda b, i: (b, f(i), col))
    kvc = lambda col: pl.BlockSpec((1, ctx, kvw), lambda b, i: (b, 0, col))
    tab = lambda f: pl.BlockSpec((QB, LANES), lambda b, i: (f(i), 0))
    same = lambda i: i
    return pl.pallas_call(
        functools.partial(_swa_kernel, ncb=ncb, seq=t - ctx, ngroups=nkv),
        grid=(nb, nblk),
        in_specs=[pl.BlockSpec(memory_space=pltpu.SMEM), wide(0), wide(1), kvc(kcol), kvc(vcol),
                  kv(kcol, prev), kv(kcol, same), kv(kcol, nxt), kv(vcol, prev), kv(vcol, same), kv(vcol, nxt),
                  tab(same), tab(same), tab(prev), tab(prev), tab(nxt), tab(nxt)],
        out_specs=wide(0),
        out_shape=jax.ShapeDtypeStruct((nb, t, w), BF16),
        compiler_params=_cparams(("parallel", "arbitrary")),
        name="swa",
    )(sink, z, z, z, z, z, z, z, z, z, z, cos, sin, cos, sin, cos, sin)


C_GN_EPS = 64e-5
C_LORA_PAD = 128


def _prepc_kernel(x_ref, xp_ref, xn_ref, nw_ref, shl_ref, scl_ref, shc_ref, scc_ref, mu_ref, o_ref, scr,
                  *, tm, ctx, t_all):
    base = pl.program_id(1) * tm
    nw = nw_ref[...]
    mods = (scl_ref[0], shl_ref[0], scc_ref[0], shc_ref[0])
    row = base + lax.broadcasted_iota(jnp.int32, (tm, 1), 0)
    r8 = lax.broadcasted_iota(jnp.int32, (8, 1), 0)
    h = _normmod(x_ref[0], nw, row < ctx, *mods)
    scr[0:8, :] = _normmod(xp_ref[0], nw, (base - 8 + r8) < ctx, *mods)
    scr[8:tm + 8, :] = h
    scr[tm + 8:tm + 16, :] = _normmod(xn_ref[0], nw, (base + tm + r8) < ctx, *mods)
    has_prev = jnp.logical_and(row != 0, row != ctx)
    has_next = jnp.logical_and(row != ctx - 1, row != t_all - 1)
    xx = 0.5 * (jnp.where(has_prev, scr[7:tm + 7, :], 0.0) + jnp.where(has_next, scr[9:tm + 9, :], 0.0)) - h
    for n in range(o_ref.shape[0]):
        o_ref[n, 0] = (h + xx * mu_ref[n:n + 1, :]).astype(o_ref.dtype)


def _prepc(xs, nw, mod, mu, ctx):
    nb, t, d = xs.shape
    tm = _tile(t, 264)
    nmix = mu.shape[0]
    last8 = t // 8 - 1
    return pl.pallas_call(
        functools.partial(_prepc_kernel, tm=tm, ctx=ctx, t_all=t),
        grid=(nb, t // tm),
        in_specs=[pl.BlockSpec((1, tm, d), lambda b, i: (b, i, 0)),
                  pl.BlockSpec((1, 8, d), lambda b, i: (b, jnp.maximum(i * (tm // 8) - 1, 0), 0)),
                  pl.BlockSpec((1, 8, d), lambda b, i: (b, jnp.minimum((i + 1) * (tm // 8), last8), 0)),
                  pl.BlockSpec((1, d), lambda b, i: (0, 0))] + _mod_specs(d, nb) +
                 [pl.BlockSpec((nmix, d), lambda b, i: (0, 0))],
        out_specs=pl.BlockSpec((nmix, 1, tm, d), lambda b, i: (0, b, i, 0)),
        out_shape=jax.ShapeDtypeStruct((nmix, nb, t, d), BF16),
        scratch_shapes=[pltpu.VMEM((tm + 16, d), F32)],
        compiler_params=_cparams(("parallel", "parallel")),
        name="prep_rwkv",
    )(xs, xs, xs, nw.reshape(1, d), mod, mod, mod, mod, mu)


def _lora_kernel(xw_ref, xa_ref, w1_ref, a1_ref, w2_ref, a2_ref, w0_ref, a0_ref, lw_ref, ic_ref):
    t1 = jnp.tanh(_dot(xw_ref[0, 0], w1_ref[...]))
    t2 = _dot(xa_ref[0, 0], a1_ref[...])
    for d in range(2):
        sl = slice(d * C_LORA_PAD, (d + 1) * C_LORA_PAD)
        lw_ref[d, 0] = -math.exp(-0.5) * _sigmoid(w0_ref[d] + _dot(t1[:, sl], w2_ref[d]))
        ic_ref[d, 0] = _sigmoid(a0_ref[d] + _dot(t2[:, sl], a2_ref[d]))


def _pad_lora(w_in, w_out):
    r = w_in.shape[2]
    a = jnp.pad(w_in, ((0, 0), (0, 0), (0, C_LORA_PAD - r)))
    a = jnp.concatenate([a[0], a[1]], axis=1).astype(BF16)
    b = jnp.pad(w_out, ((0, 0), (0, C_LORA_PAD - r), (0, 0))).astype(BF16)
    return a, b


def _lora(mix, w0, w1, w2, a0, a1, a2):
    _, nb, t, d = mix.shape
    w = w0.shape[1]
    tm = _tile(t, 264)
    w1p, w2p = _pad_lora(w1, w2)
    a1p, a2p = _pad_lora(a1, a2)
    full = lambda shape: pl.BlockSpec(shape, lambda b, i: (0,) * len(shape))
    out = pl.BlockSpec((2, 1, tm, w), lambda b, i: (0, b, i, 0))
    return pl.pallas_call(
        _lora_kernel,
        grid=(nb, t // tm),
        in_specs=[pl.BlockSpec((1, 1, tm, d), lambda b, i: (4, b, i, 0)),
                  pl.BlockSpec((1, 1, tm, d), lambda b, i: (5, b, i, 0)),
                  full(w1p.shape), full(a1p.shape), full(w2p.shape), full(a2p.shape),
                  full((2, 1, w)), full((2, 1, w))],
        out_specs=[out, out],
        out_shape=[jax.ShapeDtypeStruct((2, nb, t, w), F32)] * 2,
        compiler_params=_cparams(("parallel", "parallel")),
        name="rwkv_lora",
    )(mix, mix, w1p, a1p, w2p, a2p, w0.reshape(2, 1, w), a0.reshape(2, 1, w))


RWKV_PAIRS_PER_STEP = 8


def _rwkv_consts(reverse):
    t2 = lax.broadcasted_iota(jnp.int32, (CHUNK, LANES), 0)
    s2 = lax.broadcasted_iota(jnp.int32, (CHUNK, LANES), 1) % CHUNK
    strict = (s2 > t2) if reverse else (s2 < t2)
    incl = (s2 >= t2) if reverse else (s2 <= t2)
    eye = jnp.where(t2 == s2, 1.0, 0.0)
    left = lax.broadcasted_iota(jnp.int32, (1, LANES), 1) < HEAD
    vi = lax.broadcasted_iota(jnp.int32, (LANES, LANES), 0) < HEAD
    ki = lax.broadcasted_iota(jnp.int32, (LANES, LANES), 1) < HEAD
    return strict, incl, eye, left, vi == ki


def _stack(x, left):
    x = x.astype(BF16)
    zero = jnp.zeros_like(x)
    return jnp.concatenate([jnp.where(left, x, zero), jnp.where(left, zero, x)], axis=0)


def _rwkv_prepare(r, k, v, lw, ic, kkw, kaw, rkw, consts, reverse):
    _, _, _, left, _ = consts
    stack = functools.partial(_stack, left=left)

    def segsum(x):
        sl = jnp.sum(jnp.where(left, x, 0.0), axis=-1, keepdims=True)
        sr = jnp.sum(jnp.where(left, 0.0, x), axis=-1, keepdims=True)
        return jnp.where(left, sl, sr)

    kx = k * kkw
    kk = kx / jnp.maximum(jnp.sqrt(segsum(kx * kx)), 1e-12)
    kd = k * (1.0 + (ic - 1.0) * kaw)
    b = kk * ic
    bonus = segsum(r * kd * rkw) * v
    ti = lax.broadcasted_iota(jnp.int32, (CHUNK, CHUNK), 0)
    si = lax.broadcasted_iota(jnp.int32, (CHUNK, CHUNK), 1)
    lam = _dot_sel(((si >= ti) if reverse else (si <= ti)).astype(BF16), lw)
    end = 0 if reverse else CHUNK - 1
    lam_c = lam[end:end + 1, :]
    einv = jnp.exp(-lam)
    ebar = jnp.exp(lam_c - lam)
    at = (-kk * jnp.exp(lam - lw)).astype(BF16)
    rt = (r * jnp.exp(lam)).astype(BF16)
    return dict(at=at, rt=rt, ar=jnp.concatenate([at, rt], axis=0),
                kb=jnp.concatenate([stack(kd * einv), stack(b * einv)], axis=0),
                v=v.astype(BF16), vst=stack(v), kbar=(kd * ebar).astype(BF16), bbar=(b * ebar).astype(BF16),
                dec=jnp.exp(lam_c), bonus=bonus)


def _inv_unit_lower(ns, eye, left):
    stack = functools.partial(_stack, left=left)
    ps = [eye + n for n in ns]
    ms = [_dot(n, stack(n)) for n in ns]
    for j in range(CHUNK.bit_length() - 4):
        both = [_dot(jnp.concatenate([m.astype(BF16), p.astype(BF16)], axis=0), stack(m)) for m, p in zip(ms, ps)]
        ps = [p + b[CHUNK:] for p, b in zip(ps, both)]
        ms = [b[:CHUNK] for b in both]
    ps = [p + _dot(p, stack(m)) for p, m in zip(ps, ms)]
    out = []
    for n, p in zip(ns, ps):
        tb = p.astype(BF16)
        nh, nl = _split2(n)
        res = _dot(jnp.concatenate([nh, nl], axis=0), stack(tb))
        e = eye - tb.astype(F32) + (res[:CHUNK] + res[CHUNK:])
        out.append(tb.astype(F32) + _dot(tb, stack(e)))
    return out


def _rwkv_kernel(r_ref, k_ref, v_ref, lw_ref, ic_ref, kk_ref, ka_ref, rk_ref, y_ref, bon_ref, ht_ref,
                 *, tb, reverse, pps):
    @pl.when(pl.program_id(2) == 0)
    def _():
        ht_ref[...] = jnp.zeros_like(ht_ref)

    consts = _rwkv_consts(reverse)
    strict, incl, eye, left, same_head = consts
    stack = functools.partial(_stack, left=left)
    nch = tb // CHUNK
    order = list(range(nch - 1, -1, -1) if reverse else range(nch))
    items = [(p, c) for p in range(pps) for c in order]

    def blk(ref, p, c):
        return ref[0, c * CHUNK:(c + 1) * CHUNK, p * LANES:(p + 1) * LANES]

    def blkd(ref, p, c):
        return ref[0, 0, c * CHUNK:(c + 1) * CHUNK, p * LANES:(p + 1) * LANES]

    def par(ref, p):
        return ref[:, p * LANES:(p + 1) * LANES]

    pre = [_rwkv_prepare(blk(r_ref, p, c), blk(k_ref, p, c), blk(v_ref, p, c), blkd(lw_ref, p, c),
                         blkd(ic_ref, p, c), par(kk_ref, p), par(ka_ref, p), par(rk_ref, p), consts, reverse)
           for p, c in items]
    gs = [_dot(x["ar"], x["kb"], 1, 1) for x in pre]
    a_ak = [jnp.where(strict, g[:CHUNK, :LANES], 0.0).astype(BF16) for g in gs]
    a_rk = [jnp.where(incl, g[CHUNK:, :LANES], 0.0).astype(BF16) for g in gs]
    a_rb = [jnp.where(incl, g[CHUNK:, LANES:], 0.0).astype(BF16) for g in gs]
    tinv = _inv_unit_lower([jnp.where(strict, g[:CHUNK, LANES:], 0.0) for g in gs], eye, left)
    av = [_dot(jnp.concatenate([ak, rk], axis=0), x["vst"]) for ak, rk, x in zip(a_ak, a_rk, pre)]
    wu = [_dot(t, jnp.concatenate([stack(x["at"]), stack(a[:CHUNK])], axis=1)).astype(BF16)
          for t, x, a in zip(tinv, pre, av)]
    ry = [_dot(rb, jnp.concatenate([stack(w[:, :LANES]), stack(w[:, LANES:])], axis=1)) for rb, w in zip(a_rb, wu)]
    rw = [(x["rt"].astype(F32) + y[:, :LANES]).astype(BF16) for x, y in zip(pre, ry)]
    y0 = [a[CHUNK:] + y[:, LANES:] for a, y in zip(av, ry)]
    ft = [jnp.where(same_head, _dot(jnp.concatenate([x["v"], w[:, LANES:]], axis=0),
                                    jnp.concatenate([x["kbar"], x["bbar"]], axis=0), 0, 0), 0.0)
          for x, w in zip(pre, wu)]
    gm = [jnp.where(same_head, _dot(x["bbar"], w[:, :LANES], 0, 0), 0.0).astype(BF16) for x, w in zip(pre, wu)]

    hts = [ht_ref[p] for p in range(pps)]
    for step in range(nch):
        for p in range(pps):
            n = p * nch + step
            c = items[n][1]
            ht = hts[p]
            htb = ht.astype(BF16)
            y = y0[n] + _dot(rw[n], htb, 1, 1)
            y_ref[0, c * CHUNK:(c + 1) * CHUNK, p * LANES:(p + 1) * LANES] = y.astype(y_ref.dtype)
            bon_ref[0, c * CHUNK:(c + 1) * CHUNK, p * LANES:(p + 1) * LANES] = pre[n]["bonus"].astype(bon_ref.dtype)
            hts[p] = ht * pre[n]["dec"] + ft[n] + _dot(htb, gm[n], 1, 1)
    for p in range(pps):
        ht_ref[p] = hts[p]


def _rwkv_scan(r, k, v, lw, ic, k_k, k_a, r_k, ctx, reverse, tb=None):
    nb, t, w = r.shape
    tb = tb or _tile(ctx, 256, CHUNK)
    assert t % tb == 0 and ctx % tb == 0
    nblk, ncb = t // tb, ctx // tb
    tmap = _time_block_map(nblk, ncb, reverse)
    d = 1 if reverse else 0
    pps = RWKV_PAIRS_PER_STEP
    wide = pps * LANES
    assert w % wide == 0
    blk = pl.BlockSpec((1, tb, wide), lambda b, p, i: (b, tmap(i), p))
    blkd = pl.BlockSpec((1, 1, tb, wide), lambda b, p, i: (d, b, tmap(i), p))
    par = pl.BlockSpec((1, wide), lambda b, p, i: (0, p))
    return pl.pallas_call(
        functools.partial(_rwkv_kernel, tb=tb, reverse=reverse, pps=pps),
        grid=(nb, w // wide, nblk),
        in_specs=[blk, blk, blk, blkd, blkd, par, par, par],
        out_specs=[blk, blk],
        out_shape=[jax.ShapeDtypeStruct((nb, t, w), BF16)] * 2,
        scratch_shapes=[pltpu.VMEM((pps, LANES, LANES), F32)],
        compiler_params=_cparams(("parallel", "parallel", "arbitrary")),
        name="rwkv_bwd" if reverse else "rwkv_fwd",
    )(r, k, v, lw, ic, k_k.reshape(1, w), k_a.reshape(1, w), r_k.reshape(1, w))


def _rwkv_out_kernel(yf_ref, yb_ref, bf_ref, bb_ref, g_ref, lnw_ref, lnb_ref, w_ref, x_ref, gl_ref, gc_ref, o_ref,
                     *, tm, ctx):
    left = lax.broadcasted_iota(jnp.int32, (1, LANES), 1) < HEAD

    def segmean(x):
        sl = jnp.sum(jnp.where(left, x, 0.0), axis=-1, keepdims=True)
        sr = jnp.sum(jnp.where(left, 0.0, x), axis=-1, keepdims=True)
        return jnp.where(left, sl, sr) * (1.0 / HEAD)

    parts = []
    for p in range(yf_ref.shape[2] // LANES):
        sl = slice(p * LANES, (p + 1) * LANES)
        y = yf_ref[0, :, sl].astype(F32) + yb_ref[0, :, sl].astype(F32)
        dlt = y - segmean(y)
        zn = dlt * lax.rsqrt(segmean(dlt * dlt) + C_GN_EPS)
        o = (zn * lnw_ref[:, sl] + lnb_ref[:, sl]
             + bf_ref[0, :, sl].astype(F32) + bb_ref[0, :, sl].astype(F32))
        parts.append((o * _silu(g_ref[0, :, sl])).astype(BF16))
    _residual_out(jnp.concatenate(parts, axis=1), w_ref, x_ref, gl_ref, gc_ref, o_ref, tm, ctx)


def _rwkv_out(y_f, y_b, bon_f, bon_b, gate, ln_w, ln_b, w_out, xs, mod, ctx):
    nb, t, w = y_f.shape
    n = w_out.shape[1]
    tm = _tile(t, 528, BF16_ROWS)
    tk = _tile(w, OUT_TK, LANES)
    sl = pl.BlockSpec((1, tm, tk), lambda b, i, k: (b, i, k))
    par = pl.BlockSpec((1, tk), lambda b, i, k: (0, k))
    return pl.pallas_call(
        functools.partial(_rwkv_out_kernel, tm=tm, ctx=ctx),
        grid=(nb, t // tm, w // tk),
        in_specs=[sl] * 5 + [par, par] + _residual_specs(tm, tk, n, nb),
        out_specs=pl.BlockSpec((1, tm, n), lambda b, i, k: (b, i, 0)),
        out_shape=jax.ShapeDtypeStruct((nb, t, n), F32),
        compiler_params=_cparams(("parallel", "parallel", "arbitrary")),
        name="rwkv_out",
    )(y_f, y_b, bon_f, bon_b, gate, ln_w.reshape(1, w), ln_b.reshape(1, w), w_out, xs, mod, mod)


def _rwkv_layer(mix, w_in, w0, w1, w2, a0, a1, a2, k_k, k_a, r_k, ln_w, ln_b, w_out, xs, mod, ctx, tb=None):
    r, k, v, gate = (_matmul(mix, w_in[n].astype(BF16), sel=n) for n in range(4))
    lw, ic = _lora(mix, w0, w1, w2, a0, a1, a2)
    y_f, bon_f = _rwkv_scan(r, k, v, lw, ic, k_k, k_a, r_k, ctx, False, tb)
    y_b, bon_b = _rwkv_scan(r, k, v, lw, ic, k_k, k_a, r_k, ctx, True, tb)
    return _rwkv_out(y_f, y_b, bon_f, bon_b, gate, ln_w, ln_b, w_out, xs, mod, ctx)


def _swa_mix(h, w_in, sink, ctx, width):
    kv2 = w_in.shape[1] - 2 * width
    w_perm = jnp.concatenate([w_in[:, :width], w_in[:, width + kv2:], w_in[:, width:width + kv2]], axis=1)
    z = _matmul(h, w_perm.astype(BF16))
    return _swa(z, sink, ctx, kv2 // 2 // HEAD)


def _final_norm_kernel(x_ref, w_ref, o_ref):
    x = x_ref[0]
    ms = jnp.mean(x * x, axis=-1, keepdims=True)
    o_ref[0] = x * lax.rsqrt(ms + RMS_EPS) * w_ref[...]


def _final_norm(xs, w, ctx):
    nb, t, d = xs.shape
    seq = t - ctx
    tm = _tile(math.gcd(ctx, seq), 1024)
    off = ctx // tm
    return pl.pallas_call(
        _final_norm_kernel,
        grid=(nb, seq // tm),
        in_specs=[pl.BlockSpec((1, tm, d), lambda b, i: (b, i + off, 0)),
                  pl.BlockSpec((1, d), lambda b, i: (0, 0))],
        out_specs=pl.BlockSpec((1, tm, d), lambda b, i: (b, i, 0)),
        out_shape=jax.ShapeDtypeStruct((nb, seq, d), F32),
        compiler_params=_cparams(("parallel", "parallel")),
        name="final_norm",
    )(xs, w.reshape(1, d))


def kernel(x, c, ctx, c_ctx, norm_w, mod_w, mod_b, a_w_in, a_lb_raw, a_onorm_w, a_w_out, b_w_in, b_sink, b_w_out,
           c_mu, c_w_in, c_w0, c_w1, c_w2, c_a0, c_a1, c_a2, c_k_k, c_k_a, c_r_k, c_ln_w, c_ln_b, c_w_out,
           final_norm_w):
    nb, seq, d = x.shape
    nctx = ctx.shape[1]
    depth = norm_w.shape[0]
    xs = jnp.concatenate([ctx, x], axis=1)
    cvec = jnp.zeros((8, d), F32).at[:nb].set(c).at[nb].set(c_ctx)
    mod_all = _modulation(cvec, mod_w, mod_b)
    lb_all = jnp.cumsum(jax.nn.softmax(a_lb_raw.astype(F32), axis=0), axis=0)
    lb_all = lb_all - lb_all[0]
    for i in range(depth):
        j, kind = i // 3, i % 3
        mod = mod_all[i].reshape(8, 1, 3 * d)
        if kind == 0:
            h = _prep(xs, norm_w[i], mod, nctx)
            xs = _hgrn2_layer(h, a_w_in[j].astype(BF16), lb_all[j] if j else None, a_onorm_w[j],
                              a_w_out[j].astype(BF16), xs, mod, nctx)
        elif kind == 1:
            h = _prep(xs, norm_w[i], mod, nctx)
            m = _swa_mix(h, b_w_in[j], b_sink[j], nctx, b_w_out.shape[1])
            xs = _matmul_residual(m, b_w_out[j].astype(BF16), xs, mod, nctx)
        else:
            mix = _prepc(xs, norm_w[i], mod, c_mu[j], nctx)
            xs = _rwkv_layer(mix, c_w_in[j], c_w0[j], c_w1[j], c_w2[j], c_a0[j], c_a1[j], c_a2[j],
                             c_k_k[j], c_k_a[j], c_r_k[j], c_ln_w[j], c_ln_b[j], c_w_out[j].astype(BF16),
                             xs, mod, nctx)
    return _final_norm(xs, final_norm_w, nctx)
```

```python
import functools
import math

import jax
import jax.numpy as jnp
from jax import lax
from jax.experimental import pallas as pl
from jax.experimental.pallas import tpu as pltpu

F32 = jnp.float32
BF16 = jnp.bfloat16
NEG_INF = float("-inf")

RMS_EPS = 1e-6
CHUNK = 64
SUB = 16
NSUB = CHUNK // SUB
A_DK = 128
SUBLANES = 8
VMEM_LIMIT = 56 * 1024 * 1024


def _cparams(sem):
    return pltpu.CompilerParams(dimension_semantics=sem, vmem_limit_bytes=VMEM_LIMIT)


def _tile(n, target, mult=8):
    best = None
    for t in range(mult, min(n, target) + 1, mult):
        if n % t == 0:
            best = t
    assert best is not None, (n, target, mult)
    return best


def _dot(a, b, ca=1, cb=0):
    return lax.dot_general(a.astype(BF16), b.astype(BF16), (((ca,), (cb,)), ((), ())),
                           preferred_element_type=F32)


def _split2(x):
    hi = x.astype(BF16)
    lo = (x - hi.astype(F32)).astype(BF16)
    return hi, lo


def _split3(x):
    hi = x.astype(BF16)
    r = x - hi.astype(F32)
    mid = r.astype(BF16)
    lo = (r - mid.astype(F32)).astype(BF16)
    return hi, mid, lo


def _dot_sel(sel, x):
    hi, mid, lo = _split3(x)
    return _dot(sel, hi) + _dot(sel, mid) + _dot(sel, lo)


def _dot_hi(a, b, ca=1, cb=0):
    ah, al = _split2(a)
    bh, bl = _split2(b)
    return _dot(ah, bh, ca, cb) + _dot(ah, bl, ca, cb) + _dot(al, bh, ca, cb)


def _cumsum_rows(x, reverse):
    ntile = x.shape[0] // SUBLANES
    sub = lax.broadcasted_iota(jnp.int32, (SUBLANES, 1), 0)
    tiles = [x[j * SUBLANES:(j + 1) * SUBLANES] for j in range(ntile)]
    for s in (1, 2, 4):
        if reverse:
            tiles = [y + jnp.where(sub < SUBLANES - s, pltpu.roll(y, SUBLANES - s, 0), 0.0) for y in tiles]
        else:
            tiles = [y + jnp.where(sub >= s, pltpu.roll(y, s, 0), 0.0) for y in tiles]
    edge = 0 if reverse else SUBLANES - 1
    order = range(ntile - 1, -1, -1) if reverse else range(ntile)
    carry = None
    out = [None] * ntile
    for j in order:
        out[j] = tiles[j] if carry is None else tiles[j] + carry
        total = tiles[j][edge:edge + 1, :]
        carry = total if carry is None else carry + total
    return jnp.concatenate(out, axis=0)


def _sigmoid(x):
    return 1.0 / (1.0 + jnp.exp(-x))


def _silu(x):
    return x * _sigmoid(x)


def _mod_kernel(c_ref, w_ref, b_ref, o_ref):
    s = _silu(c_ref[...])
    o_ref[0] = _dot_hi(s, w_ref[0]) + b_ref[0]


def _modulation(cvec, mod_w, mod_b):
    depth, d, n = mod_w.shape
    tn = _tile(n, 512, 128)
    return pl.pallas_call(
        _mod_kernel,
        grid=(depth, n // tn),
        in_specs=[pl.BlockSpec((8, d), lambda i, j: (0, 0)),
                  pl.BlockSpec((1, d, tn), lambda i, j: (i, 0, j)),
                  pl.BlockSpec((1, 1, tn), lambda i, j: (i, 0, j))],
        out_specs=pl.BlockSpec((1, 8, tn), lambda i, j: (i, 0, j)),
        out_shape=jax.ShapeDtypeStruct((depth, 8, n), F32),
        compiler_params=_cparams(("parallel", "parallel")),
        name="modulation",
    )(cvec, mod_w, mod_b.reshape(depth, 1, n))


def _normmod(x, nw, is_ctx, scl, shl, scc, shc):
    ms = jnp.mean(x * x, axis=-1, keepdims=True)
    y = x * lax.rsqrt(ms + RMS_EPS) * nw
    sc = jnp.where(is_ctx, scc, scl)
    sh = jnp.where(is_ctx, shc, shl)
    return y * (1.0 + sc) + sh


def _prep_kernel(x_ref, nw_ref, shl_ref, scl_ref, shc_ref, scc_ref, o_ref, *, tm, ctx):
    row = pl.program_id(1) * tm + lax.broadcasted_iota(jnp.int32, (tm, 1), 0)
    h = _normmod(x_ref[0], nw_ref[...], row < ctx, scl_ref[0], shl_ref[0], scc_ref[0], shc_ref[0])
    o_ref[0] = h.astype(o_ref.dtype)


def _mod_specs(d, nb):
    return [pl.BlockSpec((1, 1, d), lambda b, i, *_: (b, 0, 0)),
            pl.BlockSpec((1, 1, d), lambda b, i, *_: (b, 0, 1)),
            pl.BlockSpec((1, 1, d), lambda b, i, *_: (nb, 0, 0)),
            pl.BlockSpec((1, 1, d), lambda b, i, *_: (nb, 0, 1))]


def _prep(xs, nw, mod, ctx):
    nb, t, d = xs.shape
    tm = _tile(t, 1056)
    return pl.pallas_call(
        functools.partial(_prep_kernel, tm=tm, ctx=ctx),
        grid=(nb, t // tm),
        in_specs=[pl.BlockSpec((1, tm, d), lambda b, i: (b, i, 0)),
                  pl.BlockSpec((1, d), lambda b, i: (0, 0))] + _mod_specs(d, nb),
        out_specs=pl.BlockSpec((1, tm, d), lambda b, i: (b, i, 0)),
        out_shape=jax.ShapeDtypeStruct((nb, t, d), BF16),
        compiler_params=_cparams(("parallel", "parallel")),
        name="prep",
    )(xs, nw.reshape(1, d), mod, mod, mod, mod)


def _mm_kernel(a_ref, w_ref, o_ref):
    o_ref[0] = _dot(a_ref[0], w_ref[...]).astype(o_ref.dtype)


def _mm_sel_kernel(a_ref, w_ref, o_ref):
    o_ref[0] = _dot(a_ref[0, 0], w_ref[...]).astype(o_ref.dtype)


def _weight(w, rows, cols, where):
    arr, idx = w if isinstance(w, tuple) else (w, ())
    return arr, pl.BlockSpec((None,) * len(idx) + (rows, cols), lambda *g: tuple(idx) + where(*g))


def _wshape(w):
    return (w[0] if isinstance(w, tuple) else w).shape[-2:]


def _matmul(a, w, out_dtype=F32, sel=None):
    nb, t, k = a.shape[-3:]
    n = _wshape(w)[1]
    tm = _tile(t, 1056)
    tn = _tile(n, 512, 128)
    if sel is None:
        a_spec = pl.BlockSpec((1, tm, k), lambda b, i, j: (b, i, 0))
    else:
        a_spec = pl.BlockSpec((1, 1, tm, k), lambda b, i, j: (sel, b, i, 0))
    w_arr, w_spec = _weight(w, k, tn, lambda b, i, j: (0, j))
    return pl.pallas_call(
        _mm_kernel if sel is None else _mm_sel_kernel,
        grid=(nb, t // tm, n // tn),
        in_specs=[a_spec, w_spec],
        out_specs=pl.BlockSpec((1, tm, tn), lambda b, i, j: (b, i, j)),
        out_shape=jax.ShapeDtypeStruct((nb, t, n), out_dtype),
        compiler_params=_cparams(("parallel", "parallel", "arbitrary")),
        name="matmul",
    )(a, w_arr)


def _mm_res_kernel(a_ref, w_ref, x_ref, gl_ref, gc_ref, o_ref, *, tm, ctx):
    row = pl.program_id(1) * tm + lax.broadcasted_iota(jnp.int32, (tm, 1), 0)
    g = jnp.where(row < ctx, gc_ref[0], gl_ref[0])
    o_ref[0] = x_ref[0] + g * _dot(a_ref[0], w_ref[...])


def _matmul_residual(a, w, xs, mod, ctx):
    nb, t, k = a.shape
    n = _wshape(w)[1]
    tm = _tile(t, 1056)
    tn = _tile(n, 512, 128)
    goff = 2 * n // tn
    w, w_spec = _weight(w, k, tn, lambda b, i, j: (0, j))
    return pl.pallas_call(
        functools.partial(_mm_res_kernel, tm=tm, ctx=ctx),
        grid=(nb, t // tm, n // tn),
        in_specs=[pl.BlockSpec((1, tm, k), lambda b, i, j: (b, i, 0)),
                  w_spec,
                  pl.BlockSpec((1, tm, tn), lambda b, i, j: (b, i, j)),
                  pl.BlockSpec((1, 1, tn), lambda b, i, j: (b, 0, goff + j)),
                  pl.BlockSpec((1, 1, tn), lambda b, i, j: (nb, 0, goff + j))],
        out_specs=pl.BlockSpec((1, tm, tn), lambda b, i, j: (b, i, j)),
        out_shape=jax.ShapeDtypeStruct((nb, t, n), F32),
        compiler_params=_cparams(("parallel", "parallel", "arbitrary")),
        name="matmul_residual",
    )(a, w, xs, mod, mod)


def _time_block_map(nblk, nctx_blk, reverse):
    if not reverse:
        return lambda i: i
    return lambda i: jnp.where(i < nctx_blk, nctx_blk - 1 - i, nblk - 1 - i + nctx_blk)


NLEVEL = CHUNK.bit_length() - 1
GLA_HEADS_PER_STEP = 8


def _gla_gates(z, llb, l1m):
    ls = jnp.minimum(z, 0.0) - jnp.log(1.0 + jnp.exp(-jnp.abs(z)))
    if llb is None:
        return ls
    c2 = l1m + ls
    return jnp.maximum(llb, c2) + jnp.log(1.0 + jnp.exp(-jnp.abs(llb - c2)))


def _gla_consts(reverse):
    r = lax.broadcasted_iota(jnp.int32, (CHUNK, CHUNK), 0)
    i = lax.broadcasted_iota(jnp.int32, (CHUNK, CHUNK), 1)
    tr = (CHUNK - 1 - r) if reverse else r
    ti = (CHUNK - 1 - i) if reverse else i
    masks = []
    for lvl in range(NLEVEL):
        same = (tr >> (lvl + 1)) == (ti >> (lvl + 1))
        pair = jnp.logical_and(jnp.logical_and(same, ((tr >> lvl) & 1) == 1), ((ti >> lvl) & 1) == 0)
        masks.append(jnp.where(pair, 1.0, 0.0))
    masks.append(jnp.where(r == i, 1.0, 0.0))
    return masks


def _boundary_rows(b, lvl, sub, reverse):
    grp, half = 1 << (lvl + 1), 1 << lvl
    row_of = (lambda tau: CHUNK - 1 - tau) if reverse else (lambda tau: tau)
    tiles = []
    for a in range(CHUNK // SUBLANES):
        taus = sorted(row_of(a * SUBLANES + u) for u in range(SUBLANES))
        picks = {}
        for tau in taus:
            src = row_of((tau // grp) * grp + half - 1)
            picks.setdefault(src, []).append(row_of(tau) - a * SUBLANES)
        tile = None
        for src, subs in picks.items():
            piece = jnp.broadcast_to(b[src:src + 1, :], (SUBLANES, b.shape[1]))
            if tile is None:
                tile = piece
            else:
                tile = jnp.where(jnp.logical_and(sub >= min(subs), sub <= max(subs)), piece, tile)
        tiles.append(tile)
    return jnp.concatenate(tiles, axis=0)


def _gla_intra(q, v, g, k, mask_ref, reverse):
    sub = lax.broadcasted_iota(jnp.int32, (SUBLANES, 1), 0)
    b = _cumsum_rows(g, reverse)
    end = 0 if reverse else CHUNK - 1
    b_end = b[end:end + 1, :]
    qt = q * jnp.exp(b)
    kbar = k * jnp.exp(b_end - b)
    a = mask_ref[NLEVEL] * jnp.sum(q * k, axis=-1, keepdims=True)
    k_prev = pltpu.roll(k, CHUNK - 1 if reverse else 1, 0)
    a = a + mask_ref[0] * jnp.sum(q * (1.0 - k) * k_prev, axis=-1, keepdims=True)
    qb, kb = q.astype(BF16), k.astype(BF16)
    for lvl in range(1, NLEVEL):
        e = jnp.exp(-jnp.abs(b - _boundary_rows(b, lvl, sub, reverse))).astype(BF16)
        a = a + mask_ref[lvl] * _dot(qb * e, kb * e, 1, 1)
    return _dot(a, v), qt.astype(BF16), _dot(v, kbar, 0, 0), jnp.exp(b_end)


def _gla_kernel(q_ref, v_ref, g_ref, o_ref, st_ref, mask_ref, *, tb, reverse, scale, hps):
    @pl.when(pl.program_id(2) == 0)
    def _():
        st_ref[...] = jnp.zeros_like(st_ref)
        for n, m in enumerate(_gla_consts(reverse)):
            mask_ref[n] = m

    consts = mask_ref
    nch = tb // CHUNK
    order = list(range(nch - 1, -1, -1) if reverse else range(nch))
    items = [(h, c) for h in range(hps) for c in order]

    def blk(ref, h, c):
        return ref[0, c * CHUNK:(c + 1) * CHUNK, h * A_DK:(h + 1) * A_DK]

    gs = [blk(g_ref, h, c) for h, c in items]
    intra = [_gla_intra(blk(q_ref, h, c) * scale, blk(v_ref, h, c), g, 1.0 - jnp.exp(g), consts, reverse)
             for (h, c), g in zip(items, gs)]
    for h in range(hps):
        st = st_ref[h]
        states = []
        for n, (hh, c) in enumerate(items):
            if hh == h:
                states.append((c, n, st))
                st = st * intra[n][3] + intra[n][2]
        st_ref[h] = st
        for c, n, st_in in states:
            o = intra[n][0] + _dot(intra[n][1], st_in, 1, 1)
            o_ref[0, c * CHUNK:(c + 1) * CHUNK, h * A_DK:(h + 1) * A_DK] = o.astype(o_ref.dtype)


def _gla(z, nh, ctx, reverse, tb=None):
    nb, t, _ = z.shape
    tb = tb or _tile(ctx, 256, CHUNK)
    assert t % tb == 0 and ctx % tb == 0
    nblk, ncb = t // tb, ctx // tb
    tmap = _time_block_map(nblk, ncb, reverse)
    hps = GLA_HEADS_PER_STEP
    assert nh % hps == 0
    ng = nh // hps
    zcol = (3 if reverse else 2) * ng
    wide = hps * A_DK

    def spec(col0):
        return pl.BlockSpec((1, tb, wide), lambda b, h, i: (b, tmap(i), col0 + h))

    return pl.pallas_call(
        functools.partial(_gla_kernel, tb=tb, reverse=reverse, scale=A_DK ** -0.5, hps=hps),
        grid=(nb, ng, nblk),
        in_specs=[spec(0), spec(ng), spec(zcol)],
        out_specs=pl.BlockSpec((1, tb, wide), lambda b, h, i: (b, tmap(i), h)),
        out_shape=jax.ShapeDtypeStruct((nb, t, nh * A_DK), BF16),
        scratch_shapes=[pltpu.VMEM((hps, A_DK, A_DK), F32), pltpu.VMEM((NLEVEL + 1, CHUNK, CHUNK), F32)],
        compiler_params=_cparams(("parallel", "parallel", "arbitrary")),
        name="gla_bwd" if reverse else "gla_fwd",
    )(z, z, z)


BF16_ROWS = 16


def _hgrn2_in_kernel(a_ref, w_ref, *refs, lo, hi):
    o_ref = refs[-1]
    bounds = [r[...] for r in refs[:-1]] or [None, None]
    j = pl.program_id(2)
    is_gate = jnp.logical_and(j >= lo, j < hi)

    @pl.when(is_gate)
    def _():
        o_ref[0] = _gla_gates(_dot(a_ref[0], w_ref[...]), *bounds)

    @pl.when(jnp.logical_not(is_gate))
    def _():
        o_ref[0] = _dot(a_ref[0], w_ref[...])


def _hgrn2_in(h, w_in, lb, width):
    nb, t, k = h.shape
    n = _wshape(w_in)[1]
    tm = _tile(t, 1056, BF16_ROWS)
    tn = _tile(width, 512, 128)
    per = width // tn
    lo, hi = 2 * per, 4 * per
    par = pl.BlockSpec((1, tn), lambda b, i, j: (0, jnp.where(jnp.logical_and(j >= lo, j < hi), (j - lo) % per, 0)))
    bounds = [] if lb is None else [jnp.log(lb).reshape(1, width), jnp.log1p(-lb).reshape(1, width)]
    w_in, w_spec = _weight(w_in, k, tn, lambda b, i, j: (0, j))
    return pl.pallas_call(
        functools.partial(_hgrn2_in_kernel, lo=lo, hi=hi),
        grid=(nb, t // tm, n // tn),
        in_specs=[pl.BlockSpec((1, tm, k), lambda b, i, j: (b, i, 0)), w_spec] + [par] * len(bounds),
        out_specs=pl.BlockSpec((1, tm, tn), lambda b, i, j: (b, i, j)),
        out_shape=jax.ShapeDtypeStruct((nb, t, n), F32),
        compiler_params=_cparams(("parallel", "parallel", "arbitrary")),
        name="hgrn2_in",
    )(h, w_in, *bounds)


def _residual_out(lhs, w_ref, x_ref, gl_ref, gc_ref, o_ref, tm, ctx):
    k = pl.program_id(2)
    part = _dot(lhs, w_ref[...])

    @pl.when(k == 0)
    def _():
        o_ref[0] = part

    @pl.when(k > 0)
    def _():
        o_ref[0] += part

    @pl.when(k == pl.num_programs(2) - 1)
    def _():
        row = pl.program_id(1) * tm + lax.broadcasted_iota(jnp.int32, (tm, 1), 0)
        g = jnp.where(row < ctx, gc_ref[0], gl_ref[0])
        o_ref[0] = x_ref[0] + g * o_ref[0]


def _hgrn2_out_kernel(of_ref, ob_ref, g_ref, nw_ref, w_ref, x_ref, gl_ref, gc_ref, o_ref, *, tm, ctx):
    parts = []
    for h in range(of_ref.shape[2] // A_DK):
        sl = slice(h * A_DK, (h + 1) * A_DK)
        y = of_ref[0, :, sl].astype(F32) + ob_ref[0, :, sl].astype(F32)
        ms = jnp.mean(y * y, axis=-1, keepdims=True)
        yn = y * lax.rsqrt(ms + RMS_EPS) * nw_ref[...]
        parts.append((yn * _silu(g_ref[0, :, sl])).astype(BF16))
    _residual_out(jnp.concatenate(parts, axis=1), w_ref, x_ref, gl_ref, gc_ref, o_ref, tm, ctx)


OUT_TK = 1024


def _residual_specs(w_spec, tm, n, nb):
    return [w_spec,
            pl.BlockSpec((1, tm, n), lambda b, i, k: (b, i, 0)),
            pl.BlockSpec((1, 1, n), lambda b, i, k: (b, 0, 2)),
            pl.BlockSpec((1, 1, n), lambda b, i, k: (nb, 0, 2))]


def _hgrn2_out(o_f, o_b, z, onorm_w, w_out, xs, mod, ctx):
    nb, t, w = o_f.shape
    n = _wshape(w_out)[1]
    tm = _tile(t, 528, BF16_ROWS)
    tk = _tile(w, OUT_TK, A_DK)
    gate0 = 4 * (w // tk)
    sl = pl.BlockSpec((1, tm, tk), lambda b, i, k: (b, i, k))
    w_out, w_spec = _weight(w_out, tk, n, lambda b, i, k: (k, 0))
    return pl.pallas_call(
        functools.partial(_hgrn2_out_kernel, tm=tm, ctx=ctx),
        grid=(nb, t // tm, w // tk),
        in_specs=[sl, sl, pl.BlockSpec((1, tm, tk), lambda b, i, k: (b, i, gate0 + k)),
                  pl.BlockSpec((1, A_DK), lambda b, i, k: (0, 0))] + _residual_specs(w_spec, tm, n, nb),
        out_specs=pl.BlockSpec((1, tm, n), lambda b, i, k: (b, i, 0)),
        out_shape=jax.ShapeDtypeStruct((nb, t, n), F32),
        compiler_params=_cparams(("parallel", "parallel", "arbitrary")),
        name="hgrn2_out",
    )(o_f, o_b, z, onorm_w.reshape(1, A_DK), w_out, xs, mod, mod)


def _hgrn2_layer(h, w_in, lb, onorm_w, w_out, xs, mod, ctx, tb=None):
    width = _wshape(w_out)[0]
    z = _hgrn2_in(h, w_in, lb, width)
    o_f = _gla(z, width // A_DK, ctx, False, tb)
    o_b = _gla(z, width // A_DK, ctx, True, tb)
    return _hgrn2_out(o_f, o_b, z, onorm_w, w_out, xs, mod, ctx)


GRID_W = 64
ROPE_BASE = 10000.0
HEAD = 64
QB = 128
LANES = 128


def _rope_tables(t_all, ctx):
    quarter = HEAD // 4
    inv = ROPE_BASE ** (-jnp.arange(quarter, dtype=F32) / quarter)
    tl = jnp.arange(t_all - ctx)
    row = (tl // GRID_W).astype(F32)
    col = (tl % GRID_W).astype(F32)
    hdim = jnp.arange(LANES) % HEAD
    use_col = hdim >= 2 * quarter
    second = (hdim % (2 * quarter)) >= quarter
    pos = jnp.where(use_col[None, :], col[:, None], row[:, None])
    ang = pos * inv[hdim % quarter][None, :]
    cos = jnp.concatenate([jnp.ones((ctx, LANES), F32), jnp.cos(ang)], axis=0)
    sin = jnp.concatenate([jnp.zeros((ctx, LANES), F32),
                           jnp.where(second[None, :], jnp.sin(ang), -jnp.sin(ang))], axis=0)
    return cos, sin


def _rope(x, cos, sin, first):
    partner = jnp.where(first, pltpu.roll(x, LANES - HEAD // 4, 1), pltpu.roll(x, HEAD // 4, 1))
    return x * cos + partner * sin


def _swa_kernel(sink_ref, q_ref, *refs, ncb, seq, ngroups):
    (kc_ref, vc_ref, kp_ref, kq_ref, kn_ref, vp_ref, vq_ref, vn_ref,
     cq_ref, sq_ref, cp_ref, sp_ref, cn_ref, sn_ref, o_ref) = refs[-15:]
    gate_refs = refs[:-15]
    gate_tiles = gate_refs[0].shape[2] // LANES

    def gate_tile(col):
        c = col % gate_tiles
        return gate_refs[col // gate_tiles][0, :, c * LANES:(c + 1) * LANES]

    i = pl.program_id(1)
    lane = lax.broadcasted_iota(jnp.int32, (1, LANES), 1)
    first = (lane % (HEAD // 2)) < (HEAD // 4)
    left = lane < HEAD
    ntile = kc_ref.shape[2] // LANES

    def tile(x, c):
        return x[:, c * LANES:(c + 1) * LANES]

    kwin = [(kp_ref[0], cp_ref[...], sp_ref[...]), (kq_ref[0], cq_ref[...], sq_ref[...]),
            (kn_ref[0], cn_ref[...], sn_ref[...])]
    kt = [jnp.concatenate([tile(kc_ref[0], c)] + [_rope(tile(kk, c), cs, sn, first) for kk, cs, sn in kwin], axis=0)
          for c in range(ntile)]
    vt = [jnp.concatenate([tile(vc_ref[0], c), tile(vp_ref[0], c), tile(vq_ref[0], c), tile(vn_ref[0], c)], axis=0)
          for c in range(ntile)]
    nk = kt[0].shape[0]
    nctx = nk - 3 * QB

    r = lax.broadcasted_iota(jnp.int32, (QB, QB), 0)
    cidx = lax.broadcasted_iota(jnp.int32, (QB, QB), 1)
    qblk = i - ncb
    open_if = lambda cond: jnp.where(cond, 0.0, NEG_INF)
    b_prev = open_if(cidx >= r) + open_if(qblk >= 1)
    b_cur = jnp.zeros((QB, QB), F32) + open_if(qblk >= 0)
    b_next = open_if(cidx <= r) + open_if(jnp.logical_and(qblk >= 0, (qblk + 2) * QB <= seq))
    bias = jnp.concatenate([jnp.zeros((QB, nctx), F32), b_prev, b_cur, b_next], axis=1)

    in_even = lax.broadcasted_iota(jnp.int32, (2 * nk, LANES), 0) < nk
    in_left = lax.broadcasted_iota(jnp.int32, (2 * nk, LANES), 1) < HEAD
    ones2 = jnp.where(in_even == in_left, 1.0, 0.0).astype(BF16)

    scale = HEAD ** -0.5
    heads_per_group = q_ref.shape[2] // HEAD // ngroups
    pairs = heads_per_group // 2
    for g in range(ngroups):
        c, even = g // 2, g % 2 == 0
        own = left if even else jnp.logical_not(left)
        k_own = jnp.where(own, kt[c], 0.0)
        v_own = jnp.where(own, vt[c], 0.0)
        k_swp = pltpu.roll(k_own, HEAD, 1)
        v_swp = pltpu.roll(v_own, HEAD, 1)
        kk2 = jnp.concatenate([k_own, k_swp] if even else [k_swp, k_own], axis=0).astype(BF16)
        vv2 = jnp.concatenate([v_own, v_swp] if even else [v_swp, v_own], axis=0).astype(BF16)
        q8 = jnp.concatenate(
            [_rope(tile(q_ref[0], g * pairs + p), cq_ref[...], sq_ref[...], first) * scale for p in range(pairs)],
            axis=0)
        s8 = _dot(q8, kk2, 1, 1)
        probs, sinks = [], []
        for p in range(pairs):
            pe2, snk2 = [], []
            for e in range(2):
                sk = sink_ref[g * heads_per_group + 2 * p + e]
                s = s8[p * QB:(p + 1) * QB, e * nk:(e + 1) * nk] + bias
                m = jnp.maximum(jnp.max(s, axis=-1, keepdims=True), sk)
                pe2.append(jnp.exp((s - m).astype(BF16)))
                snk2.append(jnp.exp(sk - m))
            probs.append(jnp.concatenate(pe2, axis=1))
            sinks.append(jnp.where(left, snk2[0], snk2[1]))
        p8 = jnp.concatenate(probs, axis=0)
        o8 = _dot(p8, vv2)
        d8 = _dot(p8, ones2)
        for p in range(pairs):
            col = g * pairs + p
            o = o8[p * QB:(p + 1) * QB] / (d8[p * QB:(p + 1) * QB] + sinks[p])
            o_ref[0, :, col * LANES:(col + 1) * LANES] = (o * _silu(gate_tile(col))).astype(o_ref.dtype)


def _swa(z, sink, ctx, nkv):
    nb, t, ncol = z.shape
    kvw = nkv * HEAD
    w = (ncol - 2 * kvw) // 2
    assert t % QB == 0 and ctx % QB == 0 and w % kvw == 0
    nblk, ncb = t // QB, ctx // QB
    kcol, vcol = w // kvw, w // kvw + 1
    gw = math.gcd(w, w + 2 * kvw)
    assert gw % LANES == 0
    gates = [pl.BlockSpec((1, QB, gw), functools.partial(lambda b, i, c: (b, i, c), c=(w + 2 * kvw) // gw + n))
             for n in range(w // gw)]
    cos, sin = _rope_tables(t, ctx)
    prev = lambda i: jnp.maximum(i - 1, 0)
    nxt = lambda i: jnp.minimum(i + 1, nblk - 1)
    wide = lambda col: pl.BlockSpec((1, QB, w), lambda b, i: (b, i, col))
    kv = lambda col, f: pl.BlockSpec((1, QB, kvw), lambda b, i: (b, f(i), col))
    kvc = lambda col: pl.BlockSpec((1, ctx, kvw), lambda b, i: (b, 0, col))
    tab = lambda f: pl.BlockSpec((QB, LANES), lambda b, i: (f(i), 0))
    same = lambda i: i
    return pl.pallas_call(
        functools.partial(_swa_kernel, ncb=ncb, seq=t - ctx, ngroups=nkv),
        grid=(nb, nblk),
        in_specs=[pl.BlockSpec(memory_space=pltpu.SMEM), wide(0)] + gates + [kvc(kcol), kvc(vcol),
                  kv(kcol, prev), kv(kcol, same), kv(kcol, nxt), kv(vcol, prev), kv(vcol, same), kv(vcol, nxt),
                  tab(same), tab(same), tab(prev), tab(prev), tab(nxt), tab(nxt)],
        out_specs=wide(0),
        out_shape=jax.ShapeDtypeStruct((nb, t, w), BF16),
        compiler_params=_cparams(("parallel", "arbitrary")),
        name="swa",
    )(sink, z, *([z] * len(gates)), z, z, z, z, z, z, z, z, cos, sin, cos, sin, cos, sin)


C_GN_EPS = 64e-5
C_LORA_PAD = 128


def _prepc_kernel(x_ref, xp_ref, xn_ref, nw_ref, shl_ref, scl_ref, shc_ref, scc_ref, mu_ref, o_ref, scr,
                  *, tm, ctx, t_all):
    base = pl.program_id(1) * tm
    nw = nw_ref[...]
    mods = (scl_ref[0], shl_ref[0], scc_ref[0], shc_ref[0])
    row = base + lax.broadcasted_iota(jnp.int32, (tm, 1), 0)
    r8 = lax.broadcasted_iota(jnp.int32, (8, 1), 0)
    h = _normmod(x_ref[0], nw, row < ctx, *mods)
    scr[0:8, :] = _normmod(xp_ref[0], nw, (base - 8 + r8) < ctx, *mods)
    scr[8:tm + 8, :] = h
    scr[tm + 8:tm + 16, :] = _normmod(xn_ref[0], nw, (base + tm + r8) < ctx, *mods)
    has_prev = jnp.logical_and(row != 0, row != ctx)
    has_next = jnp.logical_and(row != ctx - 1, row != t_all - 1)
    xx = 0.5 * (jnp.where(has_prev, scr[7:tm + 7, :], 0.0) + jnp.where(has_next, scr[9:tm + 9, :], 0.0)) - h
    for n in range(o_ref.shape[0]):
        o_ref[n, 0] = (h + xx * mu_ref[n:n + 1, :]).astype(o_ref.dtype)


def _prepc(xs, nw, mod, mu, ctx):
    nb, t, d = xs.shape
    tm = _tile(t, 264)
    nmix = mu.shape[0]
    last8 = t // 8 - 1
    return pl.pallas_call(
        functools.partial(_prepc_kernel, tm=tm, ctx=ctx, t_all=t),
        grid=(nb, t // tm),
        in_specs=[pl.BlockSpec((1, tm, d), lambda b, i: (b, i, 0)),
                  pl.BlockSpec((1, 8, d), lambda b, i: (b, jnp.maximum(i * (tm // 8) - 1, 0), 0)),
                  pl.BlockSpec((1, 8, d), lambda b, i: (b, jnp.minimum((i + 1) * (tm // 8), last8), 0)),
                  pl.BlockSpec((1, d), lambda b, i: (0, 0))] + _mod_specs(d, nb) +
                 [pl.BlockSpec((nmix, d), lambda b, i: (0, 0))],
        out_specs=pl.BlockSpec((nmix, 1, tm, d), lambda b, i: (0, b, i, 0)),
        out_shape=jax.ShapeDtypeStruct((nmix, nb, t, d), BF16),
        scratch_shapes=[pltpu.VMEM((tm + 16, d), F32)],
        compiler_params=_cparams(("parallel", "parallel")),
        name="prep_rwkv",
    )(xs, xs, xs, nw.reshape(1, d), mod, mod, mod, mod, mu)


def _lora_kernel(xw_ref, xa_ref, w1_ref, a1_ref, w2_ref, a2_ref, w0_ref, a0_ref, lw_ref, ic_ref):
    t1 = jnp.tanh(_dot(xw_ref[0, 0], w1_ref[...]))
    t2 = _dot(xa_ref[0, 0], a1_ref[...])
    for d in range(2):
        sl = slice(d * C_LORA_PAD, (d + 1) * C_LORA_PAD)
        lw_ref[d, 0] = -math.exp(-0.5) * _sigmoid(w0_ref[d] + _dot(t1[:, sl], w2_ref[d]))
        ic_ref[d, 0] = _sigmoid(a0_ref[d] + _dot(t2[:, sl], a2_ref[d]))


def _pad_lora(w_in, w_out):
    r = w_in.shape[2]
    a = jnp.pad(w_in, ((0, 0), (0, 0), (0, C_LORA_PAD - r)))
    a = jnp.concatenate([a[0], a[1]], axis=1).astype(BF16)
    b = jnp.pad(w_out, ((0, 0), (0, C_LORA_PAD - r), (0, 0))).astype(BF16)
    return a, b


def _lora(mix, w0, w1, w2, a0, a1, a2):
    _, nb, t, d = mix.shape
    w = w0.shape[1]
    tm = _tile(t, 264)
    w1p, w2p = _pad_lora(w1, w2)
    a1p, a2p = _pad_lora(a1, a2)
    full = lambda shape: pl.BlockSpec(shape, lambda b, i: (0,) * len(shape))
    out = pl.BlockSpec((2, 1, tm, w), lambda b, i: (0, b, i, 0))
    return pl.pallas_call(
        _lora_kernel,
        grid=(nb, t // tm),
        in_specs=[pl.BlockSpec((1, 1, tm, d), lambda b, i: (4, b, i, 0)),
                  pl.BlockSpec((1, 1, tm, d), lambda b, i: (5, b, i, 0)),
                  full(w1p.shape), full(a1p.shape), full(w2p.shape), full(a2p.shape),
                  full((2, 1, w)), full((2, 1, w))],
        out_specs=[out, out],
        out_shape=[jax.ShapeDtypeStruct((2, nb, t, w), F32)] * 2,
        compiler_params=_cparams(("parallel", "parallel")),
        name="rwkv_lora",
    )(mix, mix, w1p, a1p, w2p, a2p, w0.reshape(2, 1, w), a0.reshape(2, 1, w))


RWKV_PAIRS_PER_STEP = 8


def _rwkv_consts(reverse):
    t2 = lax.broadcasted_iota(jnp.int32, (CHUNK, LANES), 0)
    s2 = lax.broadcasted_iota(jnp.int32, (CHUNK, LANES), 1) % CHUNK
    strict = (s2 > t2) if reverse else (s2 < t2)
    incl = (s2 >= t2) if reverse else (s2 <= t2)
    eye = jnp.where(t2 == s2, 1.0, 0.0)
    left = lax.broadcasted_iota(jnp.int32, (1, LANES), 1) < HEAD
    vi = lax.broadcasted_iota(jnp.int32, (LANES, LANES), 0) < HEAD
    ki = lax.broadcasted_iota(jnp.int32, (LANES, LANES), 1) < HEAD
    return strict, incl, eye, left, vi == ki


def _stack(x, left):
    x = x.astype(BF16)
    zero = jnp.zeros_like(x)
    return jnp.concatenate([jnp.where(left, x, zero), jnp.where(left, zero, x)], axis=0)


def _rwkv_prepare(r, k, v, lw, ic, kkw, kaw, rkw, consts, reverse):
    _, _, _, left, _ = consts
    stack = functools.partial(_stack, left=left)

    def segsum(x):
        sl = jnp.sum(jnp.where(left, x, 0.0), axis=-1, keepdims=True)
        sr = jnp.sum(jnp.where(left, 0.0, x), axis=-1, keepdims=True)
        return jnp.where(left, sl, sr)

    kx = k * kkw
    kk = kx / jnp.maximum(jnp.sqrt(segsum(kx * kx)), 1e-12)
    kd = k * (1.0 + (ic - 1.0) * kaw)
    b = kk * ic
    bonus = segsum(r * kd * rkw) * v
    ti = lax.broadcasted_iota(jnp.int32, (CHUNK, CHUNK), 0)
    si = lax.broadcasted_iota(jnp.int32, (CHUNK, CHUNK), 1)
    lam = _dot_sel(((si >= ti) if reverse else (si <= ti)).astype(BF16), lw)
    end = 0 if reverse else CHUNK - 1
    lam_c = lam[end:end + 1, :]
    einv = jnp.exp(-lam)
    ebar = jnp.exp(lam_c - lam)
    at = (-kk * jnp.exp(lam - lw)).astype(BF16)
    rt = (r * jnp.exp(lam)).astype(BF16)
    return dict(at=at, rt=rt, ar=jnp.concatenate([at, rt], axis=0),
                kb=jnp.concatenate([stack(kd * einv), stack(b * einv)], axis=0),
                v=v.astype(BF16), vst=stack(v), kbar=(kd * ebar).astype(BF16), bbar=(b * ebar).astype(BF16),
                dec=jnp.exp(lam_c), bonus=bonus)


def _inv_unit_lower(ns, eye, left):
    stack = functools.partial(_stack, left=left)
    ps = [eye + n for n in ns]
    ms = [_dot(n, stack(n)) for n in ns]
    for j in range(CHUNK.bit_length() - 4):
        both = [_dot(jnp.concatenate([m.astype(BF16), p.astype(BF16)], axis=0), stack(m)) for m, p in zip(ms, ps)]
        ps = [p + b[CHUNK:] for p, b in zip(ps, both)]
        ms = [b[:CHUNK] for b in both]
    ps = [p + _dot(p, stack(m)) for p, m in zip(ps, ms)]
    out = []
    for n, p in zip(ns, ps):
        tb = p.astype(BF16)
        nh, nl = _split2(n)
        res = _dot(jnp.concatenate([nh, nl], axis=0), stack(tb))
        e = eye - tb.astype(F32) + (res[:CHUNK] + res[CHUNK:])
        out.append(tb.astype(F32) + _dot(tb, stack(e)))
    return out


def _rwkv_kernel(r_ref, k_ref, v_ref, lw_ref, ic_ref, kk_ref, ka_ref, rk_ref, y_ref, bon_ref, ht_ref,
                 *, tb, reverse, pps):
    @pl.when(pl.program_id(2) == 0)
    def _():
        ht_ref[...] = jnp.zeros_like(ht_ref)

    consts = _rwkv_consts(reverse)
    strict, incl, eye, left, same_head = consts
    stack = functools.partial(_stack, left=left)
    nch = tb // CHUNK
    order = list(range(nch - 1, -1, -1) if reverse else range(nch))
    items = [(p, c) for p in range(pps) for c in order]

    def blk(ref, p, c):
        return ref[0, c * CHUNK:(c + 1) * CHUNK, p * LANES:(p + 1) * LANES]

    def blkd(ref, p, c):
        return ref[0, 0, c * CHUNK:(c + 1) * CHUNK, p * LANES:(p + 1) * LANES]

    def par(ref, p):
        return ref[:, p * LANES:(p + 1) * LANES]

    pre = [_rwkv_prepare(blk(r_ref, p, c), blk(k_ref, p, c), blk(v_ref, p, c), blkd(lw_ref, p, c),
                         blkd(ic_ref, p, c), par(kk_ref, p), par(ka_ref, p), par(rk_ref, p), consts, reverse)
           for p, c in items]
    gs = [_dot(x["ar"], x["kb"], 1, 1) for x in pre]
    a_ak = [jnp.where(strict, g[:CHUNK, :LANES], 0.0).astype(BF16) for g in gs]
    a_rk = [jnp.where(incl, g[CHUNK:, :LANES], 0.0).astype(BF16) for g in gs]
    a_rb = [jnp.where(incl, g[CHUNK:, LANES:], 0.0).astype(BF16) for g in gs]
    tinv = _inv_unit_lower([jnp.where(strict, g[:CHUNK, LANES:], 0.0) for g in gs], eye, left)
    av = [_dot(jnp.concatenate([ak, rk], axis=0), x["vst"]) for ak, rk, x in zip(a_ak, a_rk, pre)]
    wu = [_dot(t, jnp.concatenate([stack(x["at"]), stack(a[:CHUNK])], axis=1)).astype(BF16)
          for t, x, a in zip(tinv, pre, av)]
    ry = [_dot(rb, jnp.concatenate([stack(w[:, :LANES]), stack(w[:, LANES:])], axis=1)) for rb, w in zip(a_rb, wu)]
    rw = [(x["rt"].astype(F32) + y[:, :LANES]).astype(BF16) for x, y in zip(pre, ry)]
    y0 = [a[CHUNK:] + y[:, LANES:] for a, y in zip(av, ry)]
    ft = [jnp.where(same_head, _dot(jnp.concatenate([x["v"], w[:, LANES:]], axis=0),
                                    jnp.concatenate([x["kbar"], x["bbar"]], axis=0), 0, 0), 0.0)
          for x, w in zip(pre, wu)]
    gm = [jnp.where(same_head, _dot(x["bbar"], w[:, :LANES], 0, 0), 0.0).astype(BF16) for x, w in zip(pre, wu)]

    hts = [ht_ref[p] for p in range(pps)]
    for step in range(nch):
        for p in range(pps):
            n = p * nch + step
            c = items[n][1]
            ht = hts[p]
            htb = ht.astype(BF16)
            y = y0[n] + _dot(rw[n], htb, 1, 1)
            y_ref[0, c * CHUNK:(c + 1) * CHUNK, p * LANES:(p + 1) * LANES] = y.astype(y_ref.dtype)
            bon_ref[0, c * CHUNK:(c + 1) * CHUNK, p * LANES:(p + 1) * LANES] = pre[n]["bonus"].astype(bon_ref.dtype)
            hts[p] = ht * pre[n]["dec"] + ft[n] + _dot(htb, gm[n], 1, 1)
    for p in range(pps):
        ht_ref[p] = hts[p]


def _rwkv_scan(r, k, v, lw, ic, k_k, k_a, r_k, ctx, reverse, tb=None):
    nb, t, w = r.shape
    tb = tb or _tile(ctx, 256, CHUNK)
    assert t % tb == 0 and ctx % tb == 0
    nblk, ncb = t // tb, ctx // tb
    tmap = _time_block_map(nblk, ncb, reverse)
    d = 1 if reverse else 0
    pps = RWKV_PAIRS_PER_STEP
    wide = pps * LANES
    assert w % wide == 0
    blk = pl.BlockSpec((1, tb, wide), lambda b, p, i: (b, tmap(i), p))
    blkd = pl.BlockSpec((1, 1, tb, wide), lambda b, p, i: (d, b, tmap(i), p))
    par = pl.BlockSpec((1, wide), lambda b, p, i: (0, p))
    return pl.pallas_call(
        functools.partial(_rwkv_kernel, tb=tb, reverse=reverse, pps=pps),
        grid=(nb, w // wide, nblk),
        in_specs=[blk, blk, blk, blkd, blkd, par, par, par],
        out_specs=[blk, blk],
        out_shape=[jax.ShapeDtypeStruct((nb, t, w), BF16)] * 2,
        scratch_shapes=[pltpu.VMEM((pps, LANES, LANES), F32)],
        compiler_params=_cparams(("parallel", "parallel", "arbitrary")),
        name="rwkv_bwd" if reverse else "rwkv_fwd",
    )(r, k, v, lw, ic, k_k.reshape(1, w), k_a.reshape(1, w), r_k.reshape(1, w))


def _rwkv_out_kernel(yf_ref, yb_ref, bf_ref, bb_ref, g_ref, lnw_ref, lnb_ref, w_ref, x_ref, gl_ref, gc_ref, o_ref,
                     *, tm, ctx):
    left = lax.broadcasted_iota(jnp.int32, (1, LANES), 1) < HEAD

    def segmean(x):
        sl = jnp.sum(jnp.where(left, x, 0.0), axis=-1, keepdims=True)
        sr = jnp.sum(jnp.where(left, 0.0, x), axis=-1, keepdims=True)
        return jnp.where(left, sl, sr) * (1.0 / HEAD)

    parts = []
    for p in range(yf_ref.shape[2] // LANES):
        sl = slice(p * LANES, (p + 1) * LANES)
        y = yf_ref[0, :, sl].astype(F32) + yb_ref[0, :, sl].astype(F32)
        dlt = y - segmean(y)
        zn = dlt * lax.rsqrt(segmean(dlt * dlt) + C_GN_EPS)
        o = (zn * lnw_ref[:, sl] + lnb_ref[:, sl]
             + bf_ref[0, :, sl].astype(F32) + bb_ref[0, :, sl].astype(F32))
        parts.append((o * _silu(g_ref[0, :, sl])).astype(BF16))
    _residual_out(jnp.concatenate(parts, axis=1), w_ref, x_ref, gl_ref, gc_ref, o_ref, tm, ctx)


def _rwkv_out(y_f, y_b, bon_f, bon_b, gate, ln_w, ln_b, w_out, xs, mod, ctx):
    nb, t, w = y_f.shape
    n = _wshape(w_out)[1]
    tm = _tile(t, 528, BF16_ROWS)
    tk = _tile(w, OUT_TK, LANES)
    sl = pl.BlockSpec((1, tm, tk), lambda b, i, k: (b, i, k))
    par = pl.BlockSpec((1, tk), lambda b, i, k: (0, k))
    w_out, w_spec = _weight(w_out, tk, n, lambda b, i, k: (k, 0))
    return pl.pallas_call(
        functools.partial(_rwkv_out_kernel, tm=tm, ctx=ctx),
        grid=(nb, t // tm, w // tk),
        in_specs=[sl] * 5 + [par, par] + _residual_specs(w_spec, tm, n, nb),
        out_specs=pl.BlockSpec((1, tm, n), lambda b, i, k: (b, i, 0)),
        out_shape=jax.ShapeDtypeStruct((nb, t, n), F32),
        compiler_params=_cparams(("parallel", "parallel", "arbitrary")),
        name="rwkv_out",
    )(y_f, y_b, bon_f, bon_b, gate, ln_w.reshape(1, w), ln_b.reshape(1, w), w_out, xs, mod, mod)


def _rwkv_layer(mix, w_in, w0, w1, w2, a0, a1, a2, k_k, k_a, r_k, ln_w, ln_b, w_out, xs, mod, ctx, tb=None):
    w_arr, w_idx = w_in if isinstance(w_in, tuple) else (w_in, ())
    r, k, v, gate = (_matmul(mix, (w_arr, tuple(w_idx) + (n,)), sel=n) for n in range(4))
    lw, ic = _lora(mix, w0, w1, w2, a0, a1, a2)
    y_f, bon_f = _rwkv_scan(r, k, v, lw, ic, k_k, k_a, r_k, ctx, False, tb)
    y_b, bon_b = _rwkv_scan(r, k, v, lw, ic, k_k, k_a, r_k, ctx, True, tb)
    return _rwkv_out(y_f, y_b, bon_f, bon_b, gate, ln_w, ln_b, w_out, xs, mod, ctx)


def _swa_mix(h, w_in, sink, ctx, width):
    kv2 = _wshape(w_in)[1] - 2 * width
    return _swa(_matmul(h, w_in), sink, ctx, kv2 // 2 // HEAD)


def _final_norm_kernel(x_ref, w_ref, o_ref):
    x = x_ref[0]
    ms = jnp.mean(x * x, axis=-1, keepdims=True)
    o_ref[0] = x * lax.rsqrt(ms + RMS_EPS) * w_ref[...]


def _final_norm(xs, w, ctx):
    nb, t, d = xs.shape
    seq = t - ctx
    tm = _tile(math.gcd(ctx, seq), 1024)
    off = ctx // tm
    return pl.pallas_call(
        _final_norm_kernel,
        grid=(nb, seq // tm),
        in_specs=[pl.BlockSpec((1, tm, d), lambda b, i: (b, i + off, 0)),
                  pl.BlockSpec((1, d), lambda b, i: (0, 0))],
        out_specs=pl.BlockSpec((1, tm, d), lambda b, i: (b, i, 0)),
        out_shape=jax.ShapeDtypeStruct((nb, seq, d), F32),
        compiler_params=_cparams(("parallel", "parallel")),
        name="final_norm",
    )(xs, w.reshape(1, d))


def kernel(x, c, ctx, c_ctx, norm_w, mod_w, mod_b, a_w_in, a_lb_raw, a_onorm_w, a_w_out, b_w_in, b_sink, b_w_out,
           c_mu, c_w_in, c_w0, c_w1, c_w2, c_a0, c_a1, c_a2, c_k_k, c_k_a, c_r_k, c_ln_w, c_ln_b, c_w_out,
           final_norm_w):
    nb, seq, d = x.shape
    nctx = ctx.shape[1]
    depth = norm_w.shape[0]
    xs = jnp.concatenate([ctx, x], axis=1)
    cvec = jnp.zeros((8, d), F32).at[:nb].set(c).at[nb].set(c_ctx)
    mod_all = _modulation(cvec, mod_w, mod_b)
    lb_all = jnp.cumsum(jax.nn.softmax(a_lb_raw.astype(F32), axis=0), axis=0)
    lb_all = lb_all - lb_all[0]
    for i in range(depth):
        j, kind = i // 3, i % 3
        mod = mod_all[i].reshape(8, 1, 3 * d)
        if kind == 0:
            h = _prep(xs, norm_w[i], mod, nctx)
            xs = _hgrn2_layer(h, (a_w_in, (j,)), lb_all[j] if j else None, a_onorm_w[j], (a_w_out, (j,)),
                              xs, mod, nctx)
        elif kind == 1:
            h = _prep(xs, norm_w[i], mod, nctx)
            m = _swa_mix(h, (b_w_in, (j,)), b_sink[j], nctx, b_w_out.shape[1])
            xs = _matmul_residual(m, (b_w_out, (j,)), xs, mod, nctx)
        else:
            mix = _prepc(xs, norm_w[i], mod, c_mu[j], nctx)
            xs = _rwkv_layer(mix, (c_w_in, (j,)), c_w0[j], c_w1[j], c_w2[j], c_a0[j], c_a1[j], c_a2[j],
                             c_k_k[j], c_k_a[j], c_r_k[j], c_ln_w[j], c_ln_b[j], (c_w_out, (j,)),
                             xs, mod, nctx)
    return _final_norm(xs, final_norm_w, nctx)
```

```python
import functools
import math

import jax
import jax.numpy as jnp
from jax import lax
from jax.experimental import pallas as pl
from jax.experimental.pallas import tpu as pltpu

F32 = jnp.float32
BF16 = jnp.bfloat16
NEG_INF = float("-inf")

RMS_EPS = 1e-6
CHUNK = 64
SUB = 16
NSUB = CHUNK // SUB
A_DK = 128
SUBLANES = 8
VMEM_LIMIT = 56 * 1024 * 1024


def _cparams(sem):
    return pltpu.CompilerParams(dimension_semantics=sem, vmem_limit_bytes=VMEM_LIMIT)


def _tile(n, target, mult=8):
    best = None
    for t in range(mult, min(n, target) + 1, mult):
        if n % t == 0:
            best = t
    assert best is not None, (n, target, mult)
    return best


def _dot(a, b, ca=1, cb=0):
    return lax.dot_general(a.astype(BF16), b.astype(BF16), (((ca,), (cb,)), ((), ())),
                           preferred_element_type=F32)


def _split2(x):
    hi = x.astype(BF16)
    lo = (x - hi.astype(F32)).astype(BF16)
    return hi, lo


def _split3(x):
    hi = x.astype(BF16)
    r = x - hi.astype(F32)
    mid = r.astype(BF16)
    lo = (r - mid.astype(F32)).astype(BF16)
    return hi, mid, lo


def _dot_sel(sel, x):
    hi, mid, lo = _split3(x)
    return _dot(sel, hi) + _dot(sel, mid) + _dot(sel, lo)


def _dot_hi(a, b, ca=1, cb=0):
    ah, al = _split2(a)
    bh, bl = _split2(b)
    return _dot(ah, bh, ca, cb) + _dot(ah, bl, ca, cb) + _dot(al, bh, ca, cb)


def _cumsum_rows(x, reverse):
    ntile = x.shape[0] // SUBLANES
    sub = lax.broadcasted_iota(jnp.int32, (SUBLANES, 1), 0)
    tiles = [x[j * SUBLANES:(j + 1) * SUBLANES] for j in range(ntile)]
    for s in (1, 2, 4):
        if reverse:
            tiles = [y + jnp.where(sub < SUBLANES - s, pltpu.roll(y, SUBLANES - s, 0), 0.0) for y in tiles]
        else:
            tiles = [y + jnp.where(sub >= s, pltpu.roll(y, s, 0), 0.0) for y in tiles]
    edge = 0 if reverse else SUBLANES - 1
    order = range(ntile - 1, -1, -1) if reverse else range(ntile)
    carry = None
    out = [None] * ntile
    for j in order:
        out[j] = tiles[j] if carry is None else tiles[j] + carry
        total = tiles[j][edge:edge + 1, :]
        carry = total if carry is None else carry + total
    return jnp.concatenate(out, axis=0)


def _sigmoid(x):
    return 1.0 / (1.0 + jnp.exp(-x))


def _silu(x):
    return x * _sigmoid(x)


def _mod_kernel(c_ref, w_ref, b_ref, o_ref):
    s = _silu(c_ref[...])
    o_ref[0] = _dot_hi(s, w_ref[0]) + b_ref[0]


def _modulation(cvec, mod_w, mod_b):
    depth, d, n = mod_w.shape
    tn = _tile(n, 512, 128)
    return pl.pallas_call(
        _mod_kernel,
        grid=(depth, n // tn),
        in_specs=[pl.BlockSpec((8, d), lambda i, j: (0, 0)),
                  pl.BlockSpec((1, d, tn), lambda i, j: (i, 0, j)),
                  pl.BlockSpec((1, 1, tn), lambda i, j: (i, 0, j))],
        out_specs=pl.BlockSpec((1, 8, tn), lambda i, j: (i, 0, j)),
        out_shape=jax.ShapeDtypeStruct((depth, 8, n), F32),
        compiler_params=_cparams(("parallel", "parallel")),
        name="modulation",
    )(cvec, mod_w, mod_b.reshape(depth, 1, n))


def _normmod(x, nw, is_ctx, scl, shl, scc, shc):
    ms = jnp.mean(x * x, axis=-1, keepdims=True)
    y = x * lax.rsqrt(ms + RMS_EPS) * nw
    sc = jnp.where(is_ctx, scc, scl)
    sh = jnp.where(is_ctx, shc, shl)
    return y * (1.0 + sc) + sh


def _prep_kernel(x_ref, nw_ref, shl_ref, scl_ref, shc_ref, scc_ref, o_ref, *, tm, ctx):
    row = pl.program_id(1) * tm + lax.broadcasted_iota(jnp.int32, (tm, 1), 0)
    h = _normmod(x_ref[0], nw_ref[...], row < ctx, scl_ref[0], shl_ref[0], scc_ref[0], shc_ref[0])
    o_ref[0] = h.astype(o_ref.dtype)


def _mod_specs(d, nb):
    return [pl.BlockSpec((1, 1, d), lambda b, i, *_: (b, 0, 0)),
            pl.BlockSpec((1, 1, d), lambda b, i, *_: (b, 0, 1)),
            pl.BlockSpec((1, 1, d), lambda b, i, *_: (nb, 0, 0)),
            pl.BlockSpec((1, 1, d), lambda b, i, *_: (nb, 0, 1))]


def _prep(xs, nw, mod, ctx):
    nb, t, d = xs.shape
    tm = _tile(t, 1056)
    return pl.pallas_call(
        functools.partial(_prep_kernel, tm=tm, ctx=ctx),
        grid=(nb, t // tm),
        in_specs=[pl.BlockSpec((1, tm, d), lambda b, i: (b, i, 0)),
                  pl.BlockSpec((1, d), lambda b, i: (0, 0))] + _mod_specs(d, nb),
        out_specs=pl.BlockSpec((1, tm, d), lambda b, i: (b, i, 0)),
        out_shape=jax.ShapeDtypeStruct((nb, t, d), BF16),
        compiler_params=_cparams(("parallel", "parallel")),
        name="prep",
    )(xs, nw.reshape(1, d), mod, mod, mod, mod)


def _mm_kernel(a_ref, w_ref, o_ref):
    o_ref[0] = _dot(a_ref[0], w_ref[...]).astype(o_ref.dtype)


def _mm_sel_kernel(a_ref, w_ref, o_ref):
    o_ref[0] = _dot(a_ref[0, 0], w_ref[...]).astype(o_ref.dtype)


def _weight(w, rows, cols, where):
    arr, idx = w if isinstance(w, tuple) else (w, ())
    return arr, pl.BlockSpec((None,) * len(idx) + (rows, cols), lambda *g: tuple(idx) + where(*g))


def _wshape(w):
    return (w[0] if isinstance(w, tuple) else w).shape[-2:]


def _matmul(a, w, out_dtype=F32, sel=None):
    nb, t, k = a.shape[-3:]
    n = _wshape(w)[1]
    tm = _tile(t, 1056)
    tn = _tile(n, 512, 128)
    if sel is None:
        a_spec = pl.BlockSpec((1, tm, k), lambda b, i, j: (b, i, 0))
    else:
        a_spec = pl.BlockSpec((1, 1, tm, k), lambda b, i, j: (sel, b, i, 0))
    w_arr, w_spec = _weight(w, k, tn, lambda b, i, j: (0, j))
    return pl.pallas_call(
        _mm_kernel if sel is None else _mm_sel_kernel,
        grid=(nb, t // tm, n // tn),
        in_specs=[a_spec, w_spec],
        out_specs=pl.BlockSpec((1, tm, tn), lambda b, i, j: (b, i, j)),
        out_shape=jax.ShapeDtypeStruct((nb, t, n), out_dtype),
        compiler_params=_cparams(("parallel", "parallel", "arbitrary")),
        name="matmul",
    )(a, w_arr)


def _mm_res_kernel(a_ref, w_ref, x_ref, gl_ref, gc_ref, o_ref, *, tm, ctx):
    row = pl.program_id(1) * tm + lax.broadcasted_iota(jnp.int32, (tm, 1), 0)
    g = jnp.where(row < ctx, gc_ref[0], gl_ref[0])
    o_ref[0] = x_ref[0] + g * _dot(a_ref[0], w_ref[...])


def _matmul_residual(a, w, xs, mod, ctx):
    nb, t, k = a.shape
    n = _wshape(w)[1]
    tm = _tile(t, 1056)
    tn = _tile(n, 512, 128)
    goff = 2 * n // tn
    w, w_spec = _weight(w, k, tn, lambda b, i, j: (0, j))
    return pl.pallas_call(
        functools.partial(_mm_res_kernel, tm=tm, ctx=ctx),
        grid=(nb, t // tm, n // tn),
        in_specs=[pl.BlockSpec((1, tm, k), lambda b, i, j: (b, i, 0)),
                  w_spec,
                  pl.BlockSpec((1, tm, tn), lambda b, i, j: (b, i, j)),
                  pl.BlockSpec((1, 1, tn), lambda b, i, j: (b, 0, goff + j)),
                  pl.BlockSpec((1, 1, tn), lambda b, i, j: (nb, 0, goff + j))],
        out_specs=pl.BlockSpec((1, tm, tn), lambda b, i, j: (b, i, j)),
        out_shape=jax.ShapeDtypeStruct((nb, t, n), F32),
        compiler_params=_cparams(("parallel", "parallel", "arbitrary")),
        name="matmul_residual",
    )(a, w, xs, mod, mod)


def _time_block_map(nblk, nctx_blk, reverse):
    if not reverse:
        return lambda i: i
    return lambda i: jnp.where(i < nctx_blk, nctx_blk - 1 - i, nblk - 1 - i + nctx_blk)


NLEVEL = CHUNK.bit_length() - 1
GLA_HEADS_PER_STEP = 8


def _gla_gates(z, llb, l1m):
    ls = jnp.minimum(z, 0.0) - jnp.log(1.0 + jnp.exp(-jnp.abs(z)))
    if llb is None:
        return ls
    c2 = l1m + ls
    return jnp.maximum(llb, c2) + jnp.log(1.0 + jnp.exp(-jnp.abs(llb - c2)))


def _gla_consts(reverse):
    r = lax.broadcasted_iota(jnp.int32, (CHUNK, CHUNK), 0)
    i = lax.broadcasted_iota(jnp.int32, (CHUNK, CHUNK), 1)
    tr = (CHUNK - 1 - r) if reverse else r
    ti = (CHUNK - 1 - i) if reverse else i
    masks = []
    for lvl in range(NLEVEL):
        same = (tr >> (lvl + 1)) == (ti >> (lvl + 1))
        pair = jnp.logical_and(jnp.logical_and(same, ((tr >> lvl) & 1) == 1), ((ti >> lvl) & 1) == 0)
        masks.append(jnp.where(pair, 1.0, 0.0))
    masks.append(jnp.where(r == i, 1.0, 0.0))
    return masks


def _boundary_rows(b, lvl, sub, reverse):
    grp, half = 1 << (lvl + 1), 1 << lvl
    row_of = (lambda tau: CHUNK - 1 - tau) if reverse else (lambda tau: tau)
    tiles = []
    for a in range(CHUNK // SUBLANES):
        taus = sorted(row_of(a * SUBLANES + u) for u in range(SUBLANES))
        picks = {}
        for tau in taus:
            src = row_of((tau // grp) * grp + half - 1)
            picks.setdefault(src, []).append(row_of(tau) - a * SUBLANES)
        tile = None
        for src, subs in picks.items():
            piece = jnp.broadcast_to(b[src:src + 1, :], (SUBLANES, b.shape[1]))
            if tile is None:
                tile = piece
            else:
                tile = jnp.where(jnp.logical_and(sub >= min(subs), sub <= max(subs)), piece, tile)
        tiles.append(tile)
    return jnp.concatenate(tiles, axis=0)


def _gla_intra(q, v, g, k, mask_ref, reverse):
    sub = lax.broadcasted_iota(jnp.int32, (SUBLANES, 1), 0)
    b = _cumsum_rows(g, reverse)
    end = 0 if reverse else CHUNK - 1
    b_end = b[end:end + 1, :]
    qt = q * jnp.exp(b)
    kbar = k * jnp.exp(b_end - b)
    a = mask_ref[NLEVEL] * jnp.sum(q * k, axis=-1, keepdims=True)
    k_prev = pltpu.roll(k, CHUNK - 1 if reverse else 1, 0)
    a = a + mask_ref[0] * jnp.sum(q * (1.0 - k) * k_prev, axis=-1, keepdims=True)
    qb, kb = q.astype(BF16), k.astype(BF16)
    for lvl in range(1, NLEVEL):
        e = jnp.exp(-jnp.abs(b - _boundary_rows(b, lvl, sub, reverse))).astype(BF16)
        a = a + mask_ref[lvl] * _dot(qb * e, kb * e, 1, 1)
    return _dot(a, v), qt.astype(BF16), _dot(v, kbar, 0, 0), jnp.exp(b_end)


def _gla_kernel(q_ref, v_ref, g_ref, o_ref, st_ref, mask_ref, *, tb, reverse, scale, hps):
    @pl.when(pl.program_id(2) == 0)
    def _():
        st_ref[...] = jnp.zeros_like(st_ref)
        for n, m in enumerate(_gla_consts(reverse)):
            mask_ref[n] = m

    consts = mask_ref
    nch = tb // CHUNK
    order = list(range(nch - 1, -1, -1) if reverse else range(nch))
    items = [(h, c) for h in range(hps) for c in order]

    def blk(ref, h, c):
        return ref[0, c * CHUNK:(c + 1) * CHUNK, h * A_DK:(h + 1) * A_DK]

    gs = [blk(g_ref, h, c) for h, c in items]
    intra = [_gla_intra(blk(q_ref, h, c) * scale, blk(v_ref, h, c), g, 1.0 - jnp.exp(g), consts, reverse)
             for (h, c), g in zip(items, gs)]
    for h in range(hps):
        st = st_ref[h]
        states = []
        for n, (hh, c) in enumerate(items):
            if hh == h:
                states.append((c, n, st))
                st = st * intra[n][3] + intra[n][2]
        st_ref[h] = st
        for c, n, st_in in states:
            o = intra[n][0] + _dot(intra[n][1], st_in, 1, 1)
            o_ref[0, c * CHUNK:(c + 1) * CHUNK, h * A_DK:(h + 1) * A_DK] = o.astype(o_ref.dtype)


def _gla(z, nh, ctx, reverse, tb=None):
    nb, t, _ = z.shape
    tb = tb or _tile(ctx, 256, CHUNK)
    assert t % tb == 0 and ctx % tb == 0
    nblk, ncb = t // tb, ctx // tb
    tmap = _time_block_map(nblk, ncb, reverse)
    hps = GLA_HEADS_PER_STEP
    assert nh % hps == 0
    ng = nh // hps
    zcol = (3 if reverse else 2) * ng
    wide = hps * A_DK

    def spec(col0):
        return pl.BlockSpec((1, tb, wide), lambda b, h, i: (b, tmap(i), col0 + h))

    return pl.pallas_call(
        functools.partial(_gla_kernel, tb=tb, reverse=reverse, scale=A_DK ** -0.5, hps=hps),
        grid=(nb, ng, nblk),
        in_specs=[spec(0), spec(ng), spec(zcol)],
        out_specs=pl.BlockSpec((1, tb, wide), lambda b, h, i: (b, tmap(i), h)),
        out_shape=jax.ShapeDtypeStruct((nb, t, nh * A_DK), BF16),
        scratch_shapes=[pltpu.VMEM((hps, A_DK, A_DK), F32), pltpu.VMEM((NLEVEL + 1, CHUNK, CHUNK), F32)],
        compiler_params=_cparams(("parallel", "parallel", "arbitrary")),
        name="gla_bwd" if reverse else "gla_fwd",
    )(z, z, z)


BF16_ROWS = 16


def _hgrn2_in_kernel(a_ref, w_ref, *refs, lo, hi):
    o_ref = refs[-1]
    bounds = [r[...] for r in refs[:-1]] or [None, None]
    j = pl.program_id(2)
    is_gate = jnp.logical_and(j >= lo, j < hi)

    @pl.when(is_gate)
    def _():
        o_ref[0] = _gla_gates(_dot(a_ref[0], w_ref[...]), *bounds)

    @pl.when(jnp.logical_not(is_gate))
    def _():
        o_ref[0] = _dot(a_ref[0], w_ref[...])


def _hgrn2_in(h, w_in, lb, width):
    nb, t, k = h.shape
    n = _wshape(w_in)[1]
    tm = _tile(t, 1056, BF16_ROWS)
    tn = _tile(width, 512, 128)
    per = width // tn
    lo, hi = 2 * per, 4 * per
    par = pl.BlockSpec((1, tn), lambda b, i, j: (0, jnp.where(jnp.logical_and(j >= lo, j < hi), (j - lo) % per, 0)))
    bounds = [] if lb is None else [jnp.log(lb).reshape(1, width), jnp.log1p(-lb).reshape(1, width)]
    w_in, w_spec = _weight(w_in, k, tn, lambda b, i, j: (0, j))
    return pl.pallas_call(
        functools.partial(_hgrn2_in_kernel, lo=lo, hi=hi),
        grid=(nb, t // tm, n // tn),
        in_specs=[pl.BlockSpec((1, tm, k), lambda b, i, j: (b, i, 0)), w_spec] + [par] * len(bounds),
        out_specs=pl.BlockSpec((1, tm, tn), lambda b, i, j: (b, i, j)),
        out_shape=jax.ShapeDtypeStruct((nb, t, n), F32),
        compiler_params=_cparams(("parallel", "parallel", "arbitrary")),
        name="hgrn2_in",
    )(h, w_in, *bounds)


def _residual_out(lhs, w_ref, x_ref, gl_ref, gc_ref, o_ref, tm, ctx):
    k = pl.program_id(2)
    part = _dot(lhs, w_ref[...])

    @pl.when(k == 0)
    def _():
        o_ref[0] = part

    @pl.when(k > 0)
    def _():
        o_ref[0] += part

    @pl.when(k == pl.num_programs(2) - 1)
    def _():
        row = pl.program_id(1) * tm + lax.broadcasted_iota(jnp.int32, (tm, 1), 0)
        g = jnp.where(row < ctx, gc_ref[0], gl_ref[0])
        o_ref[0] = x_ref[0] + g * o_ref[0]


def _hgrn2_out_kernel(of_ref, ob_ref, g_ref, nw_ref, w_ref, x_ref, gl_ref, gc_ref, o_ref, *, tm, ctx):
    parts = []
    for h in range(of_ref.shape[2] // A_DK):
        sl = slice(h * A_DK, (h + 1) * A_DK)
        y = of_ref[0, :, sl].astype(F32) + ob_ref[0, :, sl].astype(F32)
        ms = jnp.mean(y * y, axis=-1, keepdims=True)
        yn = y * lax.rsqrt(ms + RMS_EPS) * nw_ref[...]
        parts.append((yn * _silu(g_ref[0, :, sl])).astype(BF16))
    _residual_out(jnp.concatenate(parts, axis=1), w_ref, x_ref, gl_ref, gc_ref, o_ref, tm, ctx)


OUT_TK = 1024


def _residual_specs(w_spec, tm, n, nb):
    return [w_spec,
            pl.BlockSpec((1, tm, n), lambda b, i, k: (b, i, 0)),
            pl.BlockSpec((1, 1, n), lambda b, i, k: (b, 0, 2)),
            pl.BlockSpec((1, 1, n), lambda b, i, k: (nb, 0, 2))]


def _hgrn2_out(o_f, o_b, z, onorm_w, w_out, xs, mod, ctx):
    nb, t, w = o_f.shape
    n = _wshape(w_out)[1]
    tm = _tile(t, 528, BF16_ROWS)
    tk = _tile(w, OUT_TK, A_DK)
    gate0 = 4 * (w // tk)
    sl = pl.BlockSpec((1, tm, tk), lambda b, i, k: (b, i, k))
    w_out, w_spec = _weight(w_out, tk, n, lambda b, i, k: (k, 0))
    return pl.pallas_call(
        functools.partial(_hgrn2_out_kernel, tm=tm, ctx=ctx),
        grid=(nb, t // tm, w // tk),
        in_specs=[sl, sl, pl.BlockSpec((1, tm, tk), lambda b, i, k: (b, i, gate0 + k)),
                  pl.BlockSpec((1, A_DK), lambda b, i, k: (0, 0))] + _residual_specs(w_spec, tm, n, nb),
        out_specs=pl.BlockSpec((1, tm, n), lambda b, i, k: (b, i, 0)),
        out_shape=jax.ShapeDtypeStruct((nb, t, n), F32),
        compiler_params=_cparams(("parallel", "parallel", "arbitrary")),
        name="hgrn2_out",
    )(o_f, o_b, z, onorm_w.reshape(1, A_DK), w_out, xs, mod, mod)


def _hgrn2_layer(h, w_in, lb, onorm_w, w_out, xs, mod, ctx, tb=None):
    width = _wshape(w_out)[0]
    z = _hgrn2_in(h, w_in, lb, width)
    o_f = _gla(z, width // A_DK, ctx, False, tb)
    o_b = _gla(z, width // A_DK, ctx, True, tb)
    return _hgrn2_out(o_f, o_b, z, onorm_w, w_out, xs, mod, ctx)


GRID_W = 64
ROPE_BASE = 10000.0
HEAD = 64
QB = 128
LANES = 128


def _rope_tables(t_all, ctx):
    quarter = HEAD // 4
    inv = ROPE_BASE ** (-jnp.arange(quarter, dtype=F32) / quarter)
    tl = jnp.arange(t_all - ctx)
    row = (tl // GRID_W).astype(F32)
    col = (tl % GRID_W).astype(F32)
    hdim = jnp.arange(LANES) % HEAD
    use_col = hdim >= 2 * quarter
    second = (hdim % (2 * quarter)) >= quarter
    pos = jnp.where(use_col[None, :], col[:, None], row[:, None])
    ang = pos * inv[hdim % quarter][None, :]
    cos = jnp.concatenate([jnp.ones((ctx, LANES), F32), jnp.cos(ang)], axis=0)
    sin = jnp.concatenate([jnp.zeros((ctx, LANES), F32),
                           jnp.where(second[None, :], jnp.sin(ang), -jnp.sin(ang))], axis=0)
    return cos, sin


def _rope(x, cos, sin, first):
    partner = jnp.where(first, pltpu.roll(x, LANES - HEAD // 4, 1), pltpu.roll(x, HEAD // 4, 1))
    return x * cos + partner * sin


def _swa_kernel(sink_ref, q_ref, *refs, ncb, seq, ngroups):
    (kc_ref, vc_ref, kp_ref, kq_ref, kn_ref, vp_ref, vq_ref, vn_ref,
     cq_ref, sq_ref, cp_ref, sp_ref, cn_ref, sn_ref, o_ref) = refs[-15:]
    gate_refs = refs[:-15]
    gate_tiles = gate_refs[0].shape[2] // LANES

    def gate_tile(col):
        c = col % gate_tiles
        return gate_refs[col // gate_tiles][0, :, c * LANES:(c + 1) * LANES]

    i = pl.program_id(1)
    lane = lax.broadcasted_iota(jnp.int32, (1, LANES), 1)
    first = (lane % (HEAD // 2)) < (HEAD // 4)
    left = lane < HEAD
    ntile = kc_ref.shape[2] // LANES

    def tile(x, c):
        return x[:, c * LANES:(c + 1) * LANES]

    kwin = [(kp_ref[0], cp_ref[...], sp_ref[...]), (kq_ref[0], cq_ref[...], sq_ref[...]),
            (kn_ref[0], cn_ref[...], sn_ref[...])]
    kt = [jnp.concatenate([tile(kc_ref[0], c)] + [_rope(tile(kk, c), cs, sn, first) for kk, cs, sn in kwin], axis=0)
          for c in range(ntile)]
    vt = [jnp.concatenate([tile(vc_ref[0], c), tile(vp_ref[0], c), tile(vq_ref[0], c), tile(vn_ref[0], c)], axis=0)
          for c in range(ntile)]
    nk = kt[0].shape[0]
    nctx = nk - 3 * QB

    r = lax.broadcasted_iota(jnp.int32, (QB, QB), 0)
    cidx = lax.broadcasted_iota(jnp.int32, (QB, QB), 1)
    qblk = i - ncb
    open_if = lambda cond: jnp.where(cond, 0.0, NEG_INF)
    b_prev = open_if(cidx >= r) + open_if(qblk >= 1)
    b_cur = jnp.zeros((QB, QB), F32) + open_if(qblk >= 0)
    b_next = open_if(cidx <= r) + open_if(jnp.logical_and(qblk >= 0, (qblk + 2) * QB <= seq))
    bias = jnp.concatenate([jnp.zeros((QB, nctx), F32), b_prev, b_cur, b_next], axis=1)

    in_even = lax.broadcasted_iota(jnp.int32, (2 * nk, LANES), 0) < nk
    in_left = lax.broadcasted_iota(jnp.int32, (2 * nk, LANES), 1) < HEAD
    ones2 = jnp.where(in_even == in_left, 1.0, 0.0).astype(BF16)

    scale = HEAD ** -0.5
    heads_per_group = q_ref.shape[2] // HEAD // ngroups
    pairs = heads_per_group // 2
    for g in range(ngroups):
        c, even = g // 2, g % 2 == 0
        own = left if even else jnp.logical_not(left)
        k_own = jnp.where(own, kt[c], 0.0)
        v_own = jnp.where(own, vt[c], 0.0)
        k_swp = pltpu.roll(k_own, HEAD, 1)
        v_swp = pltpu.roll(v_own, HEAD, 1)
        kk2 = jnp.concatenate([k_own, k_swp] if even else [k_swp, k_own], axis=0).astype(BF16)
        vv2 = jnp.concatenate([v_own, v_swp] if even else [v_swp, v_own], axis=0).astype(BF16)
        q8 = jnp.concatenate(
            [_rope(tile(q_ref[0], g * pairs + p), cq_ref[...], sq_ref[...], first) * scale for p in range(pairs)],
            axis=0)
        s8 = _dot(q8, kk2, 1, 1)
        probs, sinks = [], []
        for p in range(pairs):
            pe2, snk2 = [], []
            for e in range(2):
                sk = sink_ref[g * heads_per_group + 2 * p + e]
                s = s8[p * QB:(p + 1) * QB, e * nk:(e + 1) * nk] + bias
                m = jnp.maximum(jnp.max(s, axis=-1, keepdims=True), sk)
                pe2.append(jnp.exp((s - m).astype(BF16)))
                snk2.append(jnp.exp(sk - m))
            probs.append(jnp.concatenate(pe2, axis=1))
            sinks.append(jnp.where(left, snk2[0], snk2[1]))
        p8 = jnp.concatenate(probs, axis=0)
        o8 = _dot(p8, vv2)
        d8 = _dot(p8, ones2)
        for p in range(pairs):
            col = g * pairs + p
            o = o8[p * QB:(p + 1) * QB] / (d8[p * QB:(p + 1) * QB] + sinks[p])
            o_ref[0, :, col * LANES:(col + 1) * LANES] = (o * _silu(gate_tile(col))).astype(o_ref.dtype)


def _swa(z, sink, ctx, nkv):
    nb, t, ncol = z.shape
    kvw = nkv * HEAD
    w = (ncol - 2 * kvw) // 2
    assert t % QB == 0 and ctx % QB == 0 and w % kvw == 0
    nblk, ncb = t // QB, ctx // QB
    kcol, vcol = w // kvw, w // kvw + 1
    gw = math.gcd(w, w + 2 * kvw)
    assert gw % LANES == 0
    gates = [pl.BlockSpec((1, QB, gw), functools.partial(lambda b, i, c: (b, i, c), c=(w + 2 * kvw) // gw + n))
             for n in range(w // gw)]
    cos, sin = _rope_tables(t, ctx)
    prev = lambda i: jnp.maximum(i - 1, 0)
    nxt = lambda i: jnp.minimum(i + 1, nblk - 1)
    wide = lambda col: pl.BlockSpec((1, QB, w), lambda b, i: (b, i, col))
    kv = lambda col, f: pl.BlockSpec((1, QB, kvw), lambda b, i: (b, f(i), col))
    kvc = lambda col: pl.BlockSpec((1, ctx, kvw), lambda b, i: (b, 0, col))
    tab = lambda f: pl.BlockSpec((QB, LANES), lambda b, i: (f(i), 0))
    same = lambda i: i
    return pl.pallas_call(
        functools.partial(_swa_kernel, ncb=ncb, seq=t - ctx, ngroups=nkv),
        grid=(nb, nblk),
        in_specs=[pl.BlockSpec(memory_space=pltpu.SMEM), wide(0)] + gates + [kvc(kcol), kvc(vcol),
                  kv(kcol, prev), kv(kcol, same), kv(kcol, nxt), kv(vcol, prev), kv(vcol, same), kv(vcol, nxt),
                  tab(same), tab(same), tab(prev), tab(prev), tab(nxt), tab(nxt)],
        out_specs=wide(0),
        out_shape=jax.ShapeDtypeStruct((nb, t, w), BF16),
        compiler_params=_cparams(("parallel", "arbitrary")),
        name="swa",
    )(sink, z, *([z] * len(gates)), z, z, z, z, z, z, z, z, cos, sin, cos, sin, cos, sin)


C_GN_EPS = 64e-5
C_LORA_PAD = 128


def _prepc_kernel(x_ref, xp_ref, xn_ref, nw_ref, shl_ref, scl_ref, shc_ref, scc_ref, mu_ref, o_ref, scr,
                  *, tm, ctx, t_all):
    base = pl.program_id(1) * tm
    nw = nw_ref[...]
    mods = (scl_ref[0], shl_ref[0], scc_ref[0], shc_ref[0])
    row = base + lax.broadcasted_iota(jnp.int32, (tm, 1), 0)
    r8 = lax.broadcasted_iota(jnp.int32, (8, 1), 0)
    h = _normmod(x_ref[0], nw, row < ctx, *mods)
    scr[0:8, :] = _normmod(xp_ref[0], nw, (base - 8 + r8) < ctx, *mods)
    scr[8:tm + 8, :] = h
    scr[tm + 8:tm + 16, :] = _normmod(xn_ref[0], nw, (base + tm + r8) < ctx, *mods)
    has_prev = jnp.logical_and(row != 0, row != ctx)
    has_next = jnp.logical_and(row != ctx - 1, row != t_all - 1)
    xx = 0.5 * (jnp.where(has_prev, scr[7:tm + 7, :], 0.0) + jnp.where(has_next, scr[9:tm + 9, :], 0.0)) - h
    for n in range(o_ref.shape[0]):
        o_ref[n, 0] = (h + xx * mu_ref[n:n + 1, :]).astype(o_ref.dtype)


def _prepc(xs, nw, mod, mu, ctx):
    nb, t, d = xs.shape
    tm = _tile(t, 264)
    nmix = mu.shape[0]
    last8 = t // 8 - 1
    return pl.pallas_call(
        functools.partial(_prepc_kernel, tm=tm, ctx=ctx, t_all=t),
        grid=(nb, t // tm),
        in_specs=[pl.BlockSpec((1, tm, d), lambda b, i: (b, i, 0)),
                  pl.BlockSpec((1, 8, d), lambda b, i: (b, jnp.maximum(i * (tm // 8) - 1, 0), 0)),
                  pl.BlockSpec((1, 8, d), lambda b, i: (b, jnp.minimum((i + 1) * (tm // 8), last8), 0)),
                  pl.BlockSpec((1, d), lambda b, i: (0, 0))] + _mod_specs(d, nb) +
                 [pl.BlockSpec((nmix, d), lambda b, i: (0, 0))],
        out_specs=pl.BlockSpec((nmix, 1, tm, d), lambda b, i: (0, b, i, 0)),
        out_shape=jax.ShapeDtypeStruct((nmix, nb, t, d), BF16),
        scratch_shapes=[pltpu.VMEM((tm + 16, d), F32)],
        compiler_params=_cparams(("parallel", "parallel")),
        name="prep_rwkv",
    )(xs, xs, xs, nw.reshape(1, d), mod, mod, mod, mod, mu)


def _lora_kernel(xw_ref, xa_ref, w1_ref, a1_ref, w2_ref, a2_ref, w0_ref, a0_ref, lw_ref, ic_ref):
    t1 = jnp.tanh(_dot(xw_ref[0, 0], w1_ref[...]))
    t2 = _dot(xa_ref[0, 0], a1_ref[...])
    for d in range(2):
        sl = slice(d * C_LORA_PAD, (d + 1) * C_LORA_PAD)
        lw_ref[d, 0] = -math.exp(-0.5) * _sigmoid(w0_ref[d] + _dot(t1[:, sl], w2_ref[d]))
        ic_ref[d, 0] = _sigmoid(a0_ref[d] + _dot(t2[:, sl], a2_ref[d]))


def _pad_lora(w_in, w_out):
    r = w_in.shape[2]
    a = jnp.pad(w_in, ((0, 0), (0, 0), (0, C_LORA_PAD - r)))
    a = jnp.concatenate([a[0], a[1]], axis=1).astype(BF16)
    b = jnp.pad(w_out, ((0, 0), (0, C_LORA_PAD - r), (0, 0))).astype(BF16)
    return a, b


def _lora(mix, w0, w1, w2, a0, a1, a2):
    _, nb, t, d = mix.shape
    w = w0.shape[1]
    tm = _tile(t, 264)
    w1p, w2p = _pad_lora(w1, w2)
    a1p, a2p = _pad_lora(a1, a2)
    full = lambda shape: pl.BlockSpec(shape, lambda b, i: (0,) * len(shape))
    out = pl.BlockSpec((2, 1, tm, w), lambda b, i: (0, b, i, 0))
    return pl.pallas_call(
        _lora_kernel,
        grid=(nb, t // tm),
        in_specs=[pl.BlockSpec((1, 1, tm, d), lambda b, i: (4, b, i, 0)),
                  pl.BlockSpec((1, 1, tm, d), lambda b, i: (5, b, i, 0)),
                  full(w1p.shape), full(a1p.shape), full(w2p.shape), full(a2p.shape),
                  full((2, 1, w)), full((2, 1, w))],
        out_specs=[out, out],
        out_shape=[jax.ShapeDtypeStruct((2, nb, t, w), F32)] * 2,
        compiler_params=_cparams(("parallel", "parallel")),
        name="rwkv_lora",
    )(mix, mix, w1p, a1p, w2p, a2p, w0.reshape(2, 1, w), a0.reshape(2, 1, w))


RWKV_PAIRS_PER_STEP = 8


def _rwkv_consts(reverse):
    t2 = lax.broadcasted_iota(jnp.int32, (CHUNK, LANES), 0)
    s2 = lax.broadcasted_iota(jnp.int32, (CHUNK, LANES), 1) % CHUNK
    strict = (s2 > t2) if reverse else (s2 < t2)
    incl = (s2 >= t2) if reverse else (s2 <= t2)
    eye = jnp.where(t2 == s2, 1.0, 0.0)
    left = lax.broadcasted_iota(jnp.int32, (1, LANES), 1) < HEAD
    vi = lax.broadcasted_iota(jnp.int32, (LANES, LANES), 0) < HEAD
    ki = lax.broadcasted_iota(jnp.int32, (LANES, LANES), 1) < HEAD
    return strict, incl, eye, left, vi == ki


def _stack(x, left):
    x = x.astype(BF16)
    zero = jnp.zeros_like(x)
    return jnp.concatenate([jnp.where(left, x, zero), jnp.where(left, zero, x)], axis=0)


def _rwkv_prepare(r, k, v, lw, ic, kkw, kaw, rkw, consts, reverse):
    _, _, _, left, _ = consts
    stack = functools.partial(_stack, left=left)

    def segsum(x):
        sl = jnp.sum(jnp.where(left, x, 0.0), axis=-1, keepdims=True)
        sr = jnp.sum(jnp.where(left, 0.0, x), axis=-1, keepdims=True)
        return jnp.where(left, sl, sr)

    kx = k * kkw
    kk = kx / jnp.maximum(jnp.sqrt(segsum(kx * kx)), 1e-12)
    kd = k * (1.0 + (ic - 1.0) * kaw)
    b = kk * ic
    bonus = segsum(r * kd * rkw) * v
    ti = lax.broadcasted_iota(jnp.int32, (CHUNK, CHUNK), 0)
    si = lax.broadcasted_iota(jnp.int32, (CHUNK, CHUNK), 1)
    lam = _dot_sel(((si >= ti) if reverse else (si <= ti)).astype(BF16), lw)
    end = 0 if reverse else CHUNK - 1
    lam_c = lam[end:end + 1, :]
    einv = jnp.exp(-lam)
    ebar = jnp.exp(lam_c - lam)
    at = (-kk * jnp.exp(lam - lw)).astype(BF16)
    rt = (r * jnp.exp(lam)).astype(BF16)
    return dict(at=at, rt=rt, ar=jnp.concatenate([at, rt], axis=0),
                kb=jnp.concatenate([stack(kd * einv), stack(b * einv)], axis=0),
                v=v.astype(BF16), vst=stack(v), kbar=(kd * ebar).astype(BF16), bbar=(b * ebar).astype(BF16),
                dec=jnp.exp(lam_c), bonus=bonus)


def _inv_unit_lower(ns, eye, left):
    stack = functools.partial(_stack, left=left)
    ps = [eye + n for n in ns]
    ms = [_dot(n, stack(n)) for n in ns]
    for j in range(CHUNK.bit_length() - 4):
        both = [_dot(jnp.concatenate([m.astype(BF16), p.astype(BF16)], axis=0), stack(m)) for m, p in zip(ms, ps)]
        ps = [p + b[CHUNK:] for p, b in zip(ps, both)]
        ms = [b[:CHUNK] for b in both]
    ps = [p + _dot(p, stack(m)) for p, m in zip(ps, ms)]
    out = []
    for n, p in zip(ns, ps):
        tb = p.astype(BF16)
        nh, nl = _split2(n)
        res = _dot(jnp.concatenate([nh, nl], axis=0), stack(tb))
        e = eye - tb.astype(F32) + (res[:CHUNK] + res[CHUNK:])
        out.append(tb.astype(F32) + _dot(tb, stack(e)))
    return out


def _rwkv_kernel(r_ref, k_ref, v_ref, lw_ref, ic_ref, kk_ref, ka_ref, rk_ref, y_ref, bon_ref, ht_ref,
                 *, tb, reverse, pps):
    @pl.when(pl.program_id(2) == 0)
    def _():
        ht_ref[...] = jnp.zeros_like(ht_ref)

    consts = _rwkv_consts(reverse)
    strict, incl, eye, left, same_head = consts
    stack = functools.partial(_stack, left=left)
    nch = tb // CHUNK
    order = list(range(nch - 1, -1, -1) if reverse else range(nch))
    items = [(p, c) for p in range(pps) for c in order]

    def blk(ref, p, c):
        return ref[0, c * CHUNK:(c + 1) * CHUNK, p * LANES:(p + 1) * LANES]

    def blkd(ref, p, c):
        return ref[0, 0, c * CHUNK:(c + 1) * CHUNK, p * LANES:(p + 1) * LANES]

    def par(ref, p):
        return ref[:, p * LANES:(p + 1) * LANES]

    pre = [_rwkv_prepare(blk(r_ref, p, c), blk(k_ref, p, c), blk(v_ref, p, c), blkd(lw_ref, p, c),
                         blkd(ic_ref, p, c), par(kk_ref, p), par(ka_ref, p), par(rk_ref, p), consts, reverse)
           for p, c in items]
    gs = [_dot(x["ar"], x["kb"], 1, 1) for x in pre]
    a_ak = [jnp.where(strict, g[:CHUNK, :LANES], 0.0).astype(BF16) for g in gs]
    a_rk = [jnp.where(incl, g[CHUNK:, :LANES], 0.0).astype(BF16) for g in gs]
    a_rb = [jnp.where(incl, g[CHUNK:, LANES:], 0.0).astype(BF16) for g in gs]
    tinv = _inv_unit_lower([jnp.where(strict, g[:CHUNK, LANES:], 0.0) for g in gs], eye, left)
    av = [_dot(jnp.concatenate([ak, rk], axis=0), x["vst"]) for ak, rk, x in zip(a_ak, a_rk, pre)]
    wu = [_dot(t, jnp.concatenate([stack(x["at"]), stack(a[:CHUNK])], axis=1)).astype(BF16)
          for t, x, a in zip(tinv, pre, av)]
    ry = [_dot(rb, jnp.concatenate([stack(w[:, :LANES]), stack(w[:, LANES:])], axis=1)) for rb, w in zip(a_rb, wu)]
    rw = [(x["rt"].astype(F32) + y[:, :LANES]).astype(BF16) for x, y in zip(pre, ry)]
    y0 = [a[CHUNK:] + y[:, LANES:] for a, y in zip(av, ry)]
    ft = [jnp.where(same_head, _dot(jnp.concatenate([x["v"], w[:, LANES:]], axis=0),
                                    jnp.concatenate([x["kbar"], x["bbar"]], axis=0), 0, 0), 0.0)
          for x, w in zip(pre, wu)]
    gm = [jnp.where(same_head, _dot(x["bbar"], w[:, :LANES], 0, 0), 0.0).astype(BF16) for x, w in zip(pre, wu)]

    hts = [ht_ref[p] for p in range(pps)]
    for step in range(nch):
        for p in range(pps):
            n = p * nch + step
            c = items[n][1]
            ht = hts[p]
            htb = ht.astype(BF16)
            y = y0[n] + _dot(rw[n], htb, 1, 1)
            y_ref[0, c * CHUNK:(c + 1) * CHUNK, p * LANES:(p + 1) * LANES] = y.astype(y_ref.dtype)
            bon_ref[0, c * CHUNK:(c + 1) * CHUNK, p * LANES:(p + 1) * LANES] = pre[n]["bonus"].astype(bon_ref.dtype)
            hts[p] = ht * pre[n]["dec"] + ft[n] + _dot(htb, gm[n], 1, 1)
    for p in range(pps):
        ht_ref[p] = hts[p]


def _rwkv_scan(r, k, v, lw, ic, k_k, k_a, r_k, ctx, reverse, tb=None):
    nb, t, w = r.shape
    tb = tb or _tile(ctx, 256, CHUNK)
    assert t % tb == 0 and ctx % tb == 0
    nblk, ncb = t // tb, ctx // tb
    tmap = _time_block_map(nblk, ncb, reverse)
    d = 1 if reverse else 0
    pps = RWKV_PAIRS_PER_STEP
    wide = pps * LANES
    assert w % wide == 0
    blk = pl.BlockSpec((1, tb, wide), lambda b, p, i: (b, tmap(i), p))
    blkd = pl.BlockSpec((1, 1, tb, wide), lambda b, p, i: (d, b, tmap(i), p))
    par = pl.BlockSpec((1, wide), lambda b, p, i: (0, p))
    return pl.pallas_call(
        functools.partial(_rwkv_kernel, tb=tb, reverse=reverse, pps=pps),
        grid=(nb, w // wide, nblk),
        in_specs=[blk, blk, blk, blkd, blkd, par, par, par],
        out_specs=[blk, blk],
        out_shape=[jax.ShapeDtypeStruct((nb, t, w), BF16)] * 2,
        scratch_shapes=[pltpu.VMEM((pps, LANES, LANES), F32)],
        compiler_params=_cparams(("parallel", "parallel", "arbitrary")),
        name="rwkv_bwd" if reverse else "rwkv_fwd",
    )(r, k, v, lw, ic, k_k.reshape(1, w), k_a.reshape(1, w), r_k.reshape(1, w))


def _rwkv_out_kernel(yf_ref, yb_ref, bf_ref, bb_ref, g_ref, lnw_ref, lnb_ref, w_ref, x_ref, gl_ref, gc_ref, o_ref,
                     *, tm, ctx):
    left = lax.broadcasted_iota(jnp.int32, (1, LANES), 1) < HEAD

    def segmean(x):
        sl = jnp.sum(jnp.where(left, x, 0.0), axis=-1, keepdims=True)
        sr = jnp.sum(jnp.where(left, 0.0, x), axis=-1, keepdims=True)
        return jnp.where(left, sl, sr) * (1.0 / HEAD)

    parts = []
    for p in range(yf_ref.shape[2] // LANES):
        sl = slice(p * LANES, (p + 1) * LANES)
        y = yf_ref[0, :, sl].astype(F32) + yb_ref[0, :, sl].astype(F32)
        dlt = y - segmean(y)
        zn = dlt * lax.rsqrt(segmean(dlt * dlt) + C_GN_EPS)
        o = (zn * lnw_ref[:, sl] + lnb_ref[:, sl]
             + bf_ref[0, :, sl].astype(F32) + bb_ref[0, :, sl].astype(F32))
        parts.append((o * _silu(g_ref[0, :, sl])).astype(BF16))
    _residual_out(jnp.concatenate(parts, axis=1), w_ref, x_ref, gl_ref, gc_ref, o_ref, tm, ctx)


def _rwkv_out(y_f, y_b, bon_f, bon_b, gate, ln_w, ln_b, w_out, xs, mod, ctx):
    nb, t, w = y_f.shape
    n = _wshape(w_out)[1]
    tm = _tile(t, 528, BF16_ROWS)
    tk = _tile(w, OUT_TK, LANES)
    sl = pl.BlockSpec((1, tm, tk), lambda b, i, k: (b, i, k))
    par = pl.BlockSpec((1, tk), lambda b, i, k: (0, k))
    w_out, w_spec = _weight(w_out, tk, n, lambda b, i, k: (k, 0))
    return pl.pallas_call(
        functools.partial(_rwkv_out_kernel, tm=tm, ctx=ctx),
        grid=(nb, t // tm, w // tk),
        in_specs=[sl] * 5 + [par, par] + _residual_specs(w_spec, tm, n, nb),
        out_specs=pl.BlockSpec((1, tm, n), lambda b, i, k: (b, i, 0)),
        out_shape=jax.ShapeDtypeStruct((nb, t, n), F32),
        compiler_params=_cparams(("parallel", "parallel", "arbitrary")),
        name="rwkv_out",
    )(y_f, y_b, bon_f, bon_b, gate, ln_w.reshape(1, w), ln_b.reshape(1, w), w_out, xs, mod, mod)


def _rwkv_layer(mix, w_in, w0, w1, w2, a0, a1, a2, k_k, k_a, r_k, ln_w, ln_b, w_out, xs, mod, ctx, tb=None):
    w_arr, w_idx = w_in if isinstance(w_in, tuple) else (w_in, ())
    r, k, v, gate = (_matmul(mix, (w_arr, tuple(w_idx) + (n,)), sel=n) for n in range(4))
    lw, ic = _lora(mix, w0, w1, w2, a0, a1, a2)
    y_f, bon_f = _rwkv_scan(r, k, v, lw, ic, k_k, k_a, r_k, ctx, False, tb)
    y_b, bon_b = _rwkv_scan(r, k, v, lw, ic, k_k, k_a, r_k, ctx, True, tb)
    return _rwkv_out(y_f, y_b, bon_f, bon_b, gate, ln_w, ln_b, w_out, xs, mod, ctx)


def _swa_mix(h, w_in, sink, ctx, width):
    kv2 = _wshape(w_in)[1] - 2 * width
    return _swa(_matmul(h, w_in), sink, ctx, kv2 // 2 // HEAD)


def _final_norm_kernel(x_ref, w_ref, o_ref):
    x = x_ref[0]
    ms = jnp.mean(x * x, axis=-1, keepdims=True)
    o_ref[0] = x * lax.rsqrt(ms + RMS_EPS) * w_ref[...]


def _final_norm(xs, w, ctx):
    nb, t, d = xs.shape
    seq = t - ctx
    tm = _tile(math.gcd(ctx, seq), 1024)
    off = ctx // tm
    return pl.pallas_call(
        _final_norm_kernel,
        grid=(nb, seq // tm),
        in_specs=[pl.BlockSpec((1, tm, d), lambda b, i: (b, i + off, 0)),
                  pl.BlockSpec((1, d), lambda b, i: (0, 0))],
        out_specs=pl.BlockSpec((1, tm, d), lambda b, i: (b, i, 0)),
        out_shape=jax.ShapeDtypeStruct((nb, seq, d), F32),
        compiler_params=_cparams(("parallel", "parallel")),
        name="final_norm",
    )(xs, w.reshape(1, d))


def kernel(x, c, ctx, c_ctx, norm_w, mod_w, mod_b, a_w_in, a_lb_raw, a_onorm_w, a_w_out, b_w_in, b_sink, b_w_out,
           c_mu, c_w_in, c_w0, c_w1, c_w2, c_a0, c_a1, c_a2, c_k_k, c_k_a, c_r_k, c_ln_w, c_ln_b, c_w_out,
           final_norm_w):
    nb, seq, d = x.shape
    nctx = ctx.shape[1]
    depth = norm_w.shape[0]
    xs = jnp.concatenate([ctx, x], axis=1)
    cvec = jnp.zeros((8, d), F32).at[:nb].set(c).at[nb].set(c_ctx)
    mod_all = _modulation(cvec, mod_w, mod_b)
    lb_all = jnp.cumsum(jax.nn.softmax(a_lb_raw.astype(F32), axis=0), axis=0)
    lb_all = lb_all - lb_all[0]
    a_w_in, a_w_out, b_w_in, b_w_out, c_w_in, c_w_out = (
        w.astype(BF16) for w in (a_w_in, a_w_out, b_w_in, b_w_out, c_w_in, c_w_out))
    for i in range(depth):
        j, kind = i // 3, i % 3
        mod = mod_all[i].reshape(8, 1, 3 * d)
        if kind == 0:
            h = _prep(xs, norm_w[i], mod, nctx)
            xs = _hgrn2_layer(h, (a_w_in, (j,)), lb_all[j] if j else None, a_onorm_w[j], (a_w_out, (j,)),
                              xs, mod, nctx)
        elif kind == 1:
            h = _prep(xs, norm_w[i], mod, nctx)
            m = _swa_mix(h, (b_w_in, (j,)), b_sink[j], nctx, b_w_out.shape[1])
            xs = _matmul_residual(m, (b_w_out, (j,)), xs, mod, nctx)
        else:
            mix = _prepc(xs, norm_w[i], mod, c_mu[j], nctx)
            xs = _rwkv_layer(mix, (c_w_in, (j,)), c_w0[j], c_w1[j], c_w2[j], c_a0[j], c_a1[j], c_a2[j],
                             c_k_k[j], c_k_a[j], c_r_k[j], c_ln_w[j], c_ln_b[j], (c_w_out, (j,)),
                             xs, mod, nctx)
    return _final_norm(xs, final_norm_w, nctx)
```

```python
import functools
import math

import jax
import jax.numpy as jnp
from jax import lax
from jax.experimental import pallas as pl
from jax.experimental.pallas import tpu as pltpu

F32 = jnp.float32
BF16 = jnp.bfloat16
NEG_INF = float("-inf")

RMS_EPS = 1e-6
CHUNK = 64
SUB = 16
NSUB = CHUNK // SUB
A_DK = 128
SUBLANES = 8
VMEM_LIMIT = 56 * 1024 * 1024


def _cparams(sem):
    return pltpu.CompilerParams(dimension_semantics=sem, vmem_limit_bytes=VMEM_LIMIT)


def _tile(n, target, mult=8):
    best = None
    for t in range(mult, min(n, target) + 1, mult):
        if n % t == 0:
            best = t
    assert best is not None, (n, target, mult)
    return best


def _dot(a, b, ca=1, cb=0):
    return lax.dot_general(a.astype(BF16), b.astype(BF16), (((ca,), (cb,)), ((), ())),
                           preferred_element_type=F32)


def _split2(x):
    hi = x.astype(BF16)
    lo = (x - hi.astype(F32)).astype(BF16)
    return hi, lo


def _split3(x):
    hi = x.astype(BF16)
    r = x - hi.astype(F32)
    mid = r.astype(BF16)
    lo = (r - mid.astype(F32)).astype(BF16)
    return hi, mid, lo


def _dot_sel(sel, x):
    hi, mid, lo = _split3(x)
    return _dot(sel, hi) + _dot(sel, mid) + _dot(sel, lo)


def _dot_hi(a, b, ca=1, cb=0):
    ah, al = _split2(a)
    bh, bl = _split2(b)
    return _dot(ah, bh, ca, cb) + _dot(ah, bl, ca, cb) + _dot(al, bh, ca, cb)


def _cumsum_rows(x, reverse):
    ntile = x.shape[0] // SUBLANES
    sub = lax.broadcasted_iota(jnp.int32, (SUBLANES, 1), 0)
    tiles = [x[j * SUBLANES:(j + 1) * SUBLANES] for j in range(ntile)]
    for s in (1, 2, 4):
        if reverse:
            tiles = [y + jnp.where(sub < SUBLANES - s, pltpu.roll(y, SUBLANES - s, 0), 0.0) for y in tiles]
        else:
            tiles = [y + jnp.where(sub >= s, pltpu.roll(y, s, 0), 0.0) for y in tiles]
    edge = 0 if reverse else SUBLANES - 1
    order = range(ntile - 1, -1, -1) if reverse else range(ntile)
    carry = None
    out = [None] * ntile
    for j in order:
        out[j] = tiles[j] if carry is None else tiles[j] + carry
        total = tiles[j][edge:edge + 1, :]
        carry = total if carry is None else carry + total
    return jnp.concatenate(out, axis=0)


def _sigmoid(x):
    return 1.0 / (1.0 + jnp.exp(-x))


def _silu(x):
    return x * _sigmoid(x)


def _mod_kernel(c_ref, w_ref, b_ref, o_ref):
    s = _silu(c_ref[...])
    o_ref[0] = _dot_hi(s, w_ref[0]) + b_ref[0]


def _modulation(cvec, mod_w, mod_b):
    depth, d, n = mod_w.shape
    tn = _tile(n, 512, 128)
    return pl.pallas_call(
        _mod_kernel,
        grid=(depth, n // tn),
        in_specs=[pl.BlockSpec((8, d), lambda i, j: (0, 0)),
                  pl.BlockSpec((1, d, tn), lambda i, j: (i, 0, j)),
                  pl.BlockSpec((1, 1, tn), lambda i, j: (i, 0, j))],
        out_specs=pl.BlockSpec((1, 8, tn), lambda i, j: (i, 0, j)),
        out_shape=jax.ShapeDtypeStruct((depth, 8, n), F32),
        compiler_params=_cparams(("parallel", "parallel")),
        name="modulation",
    )(cvec, mod_w, mod_b.reshape(depth, 1, n))


def _normmod(x, nw, is_ctx, scl, shl, scc, shc):
    ms = jnp.mean(x * x, axis=-1, keepdims=True)
    y = x * lax.rsqrt(ms + RMS_EPS) * nw
    sc = jnp.where(is_ctx, scc, scl)
    sh = jnp.where(is_ctx, shc, shl)
    return y * (1.0 + sc) + sh


def _mod_specs(d, nb):
    return [pl.BlockSpec((1, 1, d), lambda b, i, *_: (b, 0, 0)),
            pl.BlockSpec((1, 1, d), lambda b, i, *_: (b, 0, 1)),
            pl.BlockSpec((1, 1, d), lambda b, i, *_: (nb, 0, 0)),
            pl.BlockSpec((1, 1, d), lambda b, i, *_: (nb, 0, 1))]


def _mm_kernel(a_ref, w_ref, o_ref):
    o_ref[0] = _dot(a_ref[0], w_ref[...]).astype(o_ref.dtype)


def _mm_sel_kernel(a_ref, w_ref, o_ref):
    o_ref[0] = _dot(a_ref[0, 0], w_ref[...]).astype(o_ref.dtype)


def _weight(w, rows, cols, where):
    arr, idx = w if isinstance(w, tuple) else (w, ())
    return arr, pl.BlockSpec((None,) * len(idx) + (rows, cols), lambda *g: tuple(idx) + where(*g))


def _wshape(w):
    return (w[0] if isinstance(w, tuple) else w).shape[-2:]


def _matmul(a, w, out_dtype=F32, sel=None):
    nb, t, k = a.shape[-3:]
    n = _wshape(w)[1]
    tm = _tile(t, 1056)
    tn = _tile(n, 512, 128)
    if sel is None:
        a_spec = pl.BlockSpec((1, tm, k), lambda b, i, j: (b, i, 0))
    else:
        a_spec = pl.BlockSpec((1, 1, tm, k), lambda b, i, j: (sel, b, i, 0))
    w_arr, w_spec = _weight(w, k, tn, lambda b, i, j: (0, j))
    return pl.pallas_call(
        _mm_kernel if sel is None else _mm_sel_kernel,
        grid=(nb, t // tm, n // tn),
        in_specs=[a_spec, w_spec],
        out_specs=pl.BlockSpec((1, tm, tn), lambda b, i, j: (b, i, j)),
        out_shape=jax.ShapeDtypeStruct((nb, t, n), out_dtype),
        compiler_params=_cparams(("parallel", "parallel", "arbitrary")),
        name="matmul",
    )(a, w_arr)


def _mm_res_kernel(a_ref, w_ref, x_ref, gl_ref, gc_ref, o_ref, *, tm, ctx):
    row = pl.program_id(1) * tm + lax.broadcasted_iota(jnp.int32, (tm, 1), 0)
    g = jnp.where(row < ctx, gc_ref[0], gl_ref[0])
    o_ref[0] = x_ref[0] + g * _dot(a_ref[0], w_ref[...])


def _matmul_residual(a, w, xs, mod, ctx):
    nb, t, k = a.shape
    n = _wshape(w)[1]
    tm = _tile(t, 1056)
    tn = _tile(n, 512, 128)
    goff = 2 * n // tn
    w, w_spec = _weight(w, k, tn, lambda b, i, j: (0, j))
    return pl.pallas_call(
        functools.partial(_mm_res_kernel, tm=tm, ctx=ctx),
        grid=(nb, t // tm, n // tn),
        in_specs=[pl.BlockSpec((1, tm, k), lambda b, i, j: (b, i, 0)),
                  w_spec,
                  pl.BlockSpec((1, tm, tn), lambda b, i, j: (b, i, j)),
                  pl.BlockSpec((1, 1, tn), lambda b, i, j: (b, 0, goff + j)),
                  pl.BlockSpec((1, 1, tn), lambda b, i, j: (nb, 0, goff + j))],
        out_specs=pl.BlockSpec((1, tm, tn), lambda b, i, j: (b, i, j)),
        out_shape=jax.ShapeDtypeStruct((nb, t, n), F32),
        compiler_params=_cparams(("parallel", "parallel", "arbitrary")),
        name="matmul_residual",
    )(a, w, xs, mod, mod)


def _time_block_map(nblk, nctx_blk, reverse):
    if not reverse:
        return lambda i: i
    return lambda i: jnp.where(i < nctx_blk, nctx_blk - 1 - i, nblk - 1 - i + nctx_blk)


NLEVEL = CHUNK.bit_length() - 1
GLA_HEADS_PER_STEP = 8


def _gla_gates(z, llb, l1m):
    ls = jnp.minimum(z, 0.0) - jnp.log(1.0 + jnp.exp(-jnp.abs(z)))
    if llb is None:
        return ls
    c2 = l1m + ls
    return jnp.maximum(llb, c2) + jnp.log(1.0 + jnp.exp(-jnp.abs(llb - c2)))


def _gla_consts(reverse):
    r = lax.broadcasted_iota(jnp.int32, (CHUNK, CHUNK), 0)
    i = lax.broadcasted_iota(jnp.int32, (CHUNK, CHUNK), 1)
    tr = (CHUNK - 1 - r) if reverse else r
    ti = (CHUNK - 1 - i) if reverse else i
    masks = []
    for lvl in range(NLEVEL):
        same = (tr >> (lvl + 1)) == (ti >> (lvl + 1))
        pair = jnp.logical_and(jnp.logical_and(same, ((tr >> lvl) & 1) == 1), ((ti >> lvl) & 1) == 0)
        masks.append(jnp.where(pair, 1.0, 0.0))
    masks.append(jnp.where(r == i, 1.0, 0.0))
    return masks


def _boundary_rows(b, lvl, sub, reverse):
    grp, half = 1 << (lvl + 1), 1 << lvl
    row_of = (lambda tau: CHUNK - 1 - tau) if reverse else (lambda tau: tau)
    tiles = []
    for a in range(CHUNK // SUBLANES):
        taus = sorted(row_of(a * SUBLANES + u) for u in range(SUBLANES))
        picks = {}
        for tau in taus:
            src = row_of((tau // grp) * grp + half - 1)
            picks.setdefault(src, []).append(row_of(tau) - a * SUBLANES)
        tile = None
        for src, subs in picks.items():
            piece = jnp.broadcast_to(b[src:src + 1, :], (SUBLANES, b.shape[1]))
            if tile is None:
                tile = piece
            else:
                tile = jnp.where(jnp.logical_and(sub >= min(subs), sub <= max(subs)), piece, tile)
        tiles.append(tile)
    return jnp.concatenate(tiles, axis=0)


def _gla_intra(q, v, g, k, mask_ref, reverse):
    sub = lax.broadcasted_iota(jnp.int32, (SUBLANES, 1), 0)
    b = _cumsum_rows(g, reverse)
    end = 0 if reverse else CHUNK - 1
    b_end = b[end:end + 1, :]
    qt = q * jnp.exp(b)
    kbar = k * jnp.exp(b_end - b)
    a = mask_ref[NLEVEL] * jnp.sum(q * k, axis=-1, keepdims=True)
    k_prev = pltpu.roll(k, CHUNK - 1 if reverse else 1, 0)
    a = a + mask_ref[0] * jnp.sum(q * (1.0 - k) * k_prev, axis=-1, keepdims=True)
    qb, kb = q.astype(BF16), k.astype(BF16)
    for lvl in range(1, NLEVEL):
        e = jnp.exp(-jnp.abs(b - _boundary_rows(b, lvl, sub, reverse))).astype(BF16)
        a = a + mask_ref[lvl] * _dot(qb * e, kb * e, 1, 1)
    return _dot(a, v), qt.astype(BF16), _dot(v, kbar, 0, 0), jnp.exp(b_end)


def _gla_kernel(q_ref, v_ref, g_ref, o_ref, st_ref, mask_ref, *, tb, reverse, scale, hps):
    @pl.when(pl.program_id(2) == 0)
    def _():
        st_ref[...] = jnp.zeros_like(st_ref)
        for n, m in enumerate(_gla_consts(reverse)):
            mask_ref[n] = m

    consts = mask_ref
    nch = tb // CHUNK
    order = list(range(nch - 1, -1, -1) if reverse else range(nch))
    items = [(h, c) for h in range(hps) for c in order]

    def blk(ref, h, c):
        return ref[0, c * CHUNK:(c + 1) * CHUNK, h * A_DK:(h + 1) * A_DK]

    gs = [blk(g_ref, h, c) for h, c in items]
    intra = [_gla_intra(blk(q_ref, h, c) * scale, blk(v_ref, h, c), g, 1.0 - jnp.exp(g), consts, reverse)
             for (h, c), g in zip(items, gs)]
    for h in range(hps):
        st = st_ref[h]
        states = []
        for n, (hh, c) in enumerate(items):
            if hh == h:
                states.append((c, n, st))
                st = st * intra[n][3] + intra[n][2]
        st_ref[h] = st
        for c, n, st_in in states:
            o = intra[n][0] + _dot(intra[n][1], st_in, 1, 1)
            o_ref[0, c * CHUNK:(c + 1) * CHUNK, h * A_DK:(h + 1) * A_DK] = o.astype(o_ref.dtype)


def _gla(z, nh, ctx, reverse, tb=None):
    nb, t, _ = z.shape
    tb = tb or _tile(ctx, 256, CHUNK)
    assert t % tb == 0 and ctx % tb == 0
    nblk, ncb = t // tb, ctx // tb
    tmap = _time_block_map(nblk, ncb, reverse)
    hps = GLA_HEADS_PER_STEP
    assert nh % hps == 0
    ng = nh // hps
    zcol = (3 if reverse else 2) * ng
    wide = hps * A_DK

    def spec(col0):
        return pl.BlockSpec((1, tb, wide), lambda b, h, i: (b, tmap(i), col0 + h))

    return pl.pallas_call(
        functools.partial(_gla_kernel, tb=tb, reverse=reverse, scale=A_DK ** -0.5, hps=hps),
        grid=(nb, ng, nblk),
        in_specs=[spec(0), spec(ng), spec(zcol)],
        out_specs=pl.BlockSpec((1, tb, wide), lambda b, h, i: (b, tmap(i), h)),
        out_shape=jax.ShapeDtypeStruct((nb, t, nh * A_DK), BF16),
        scratch_shapes=[pltpu.VMEM((hps, A_DK, A_DK), F32), pltpu.VMEM((NLEVEL + 1, CHUNK, CHUNK), F32)],
        compiler_params=_cparams(("parallel", "parallel", "arbitrary")),
        name="gla_bwd" if reverse else "gla_fwd",
    )(z, z, z)


BF16_ROWS = 16


def _in_proj_kernel(x_ref, nw_ref, shl_ref, scl_ref, shc_ref, scc_ref, w_ref, *refs, lo, hi, tm, ctx):
    o_ref, h_ref = refs[-2:]
    bounds = [r[...] for r in refs[:-2]] or [None, None]
    j = pl.program_id(2)

    @pl.when(j == 0)
    def _():
        row = pl.program_id(1) * tm + lax.broadcasted_iota(jnp.int32, (tm, 1), 0)
        h = _normmod(x_ref[0], nw_ref[...], row < ctx, scl_ref[0], shl_ref[0], scc_ref[0], shc_ref[0])
        h_ref[...] = h.astype(h_ref.dtype)

    is_gate = jnp.logical_and(j >= lo, j < hi)

    @pl.when(is_gate)
    def _():
        o_ref[0] = _gla_gates(_dot(h_ref[...], w_ref[...]), *bounds)

    @pl.when(jnp.logical_not(is_gate))
    def _():
        o_ref[0] = _dot(h_ref[...], w_ref[...])


def _in_proj(xs, nw, mod, ctx, w_in, gate_width=0, lb=None):
    nb, t, d = xs.shape
    n = _wshape(w_in)[1]
    tm = _tile(t, 1056, BF16_ROWS)
    tn = _tile(gate_width or n, 512, 128)
    per = max(gate_width // tn, 1)
    lo, hi = (2 * per, 4 * per) if gate_width else (0, 0)
    par = pl.BlockSpec((1, tn), lambda b, i, j: (0, jnp.where(jnp.logical_and(j >= lo, j < hi), (j - lo) % per, 0)))
    bounds = [] if lb is None else [jnp.log(lb).reshape(1, gate_width), jnp.log1p(-lb).reshape(1, gate_width)]
    w_in, w_spec = _weight(w_in, d, tn, lambda b, i, j: (0, j))
    return pl.pallas_call(
        functools.partial(_in_proj_kernel, lo=lo, hi=hi, tm=tm, ctx=ctx),
        grid=(nb, t // tm, n // tn),
        in_specs=[pl.BlockSpec((1, tm, d), lambda b, i, j: (b, i, 0)), pl.BlockSpec((1, d), lambda b, i, j: (0, 0))]
                 + _mod_specs(d, nb) + [w_spec] + [par] * len(bounds),
        out_specs=pl.BlockSpec((1, tm, tn), lambda b, i, j: (b, i, j)),
        out_shape=jax.ShapeDtypeStruct((nb, t, n), F32),
        scratch_shapes=[pltpu.VMEM((tm, d), BF16)],
        compiler_params=_cparams(("parallel", "parallel", "arbitrary")),
        name="in_proj",
    )(xs, nw.reshape(1, d), mod, mod, mod, mod, w_in, *bounds)


def _residual_out(lhs, w_ref, x_ref, gl_ref, gc_ref, o_ref, tm, ctx):
    k = pl.program_id(2)
    part = _dot(lhs, w_ref[...])

    @pl.when(k == 0)
    def _():
        o_ref[0] = part

    @pl.when(k > 0)
    def _():
        o_ref[0] += part

    @pl.when(k == pl.num_programs(2) - 1)
    def _():
        row = pl.program_id(1) * tm + lax.broadcasted_iota(jnp.int32, (tm, 1), 0)
        g = jnp.where(row < ctx, gc_ref[0], gl_ref[0])
        o_ref[0] = x_ref[0] + g * o_ref[0]


def _hgrn2_out_kernel(of_ref, ob_ref, g_ref, nw_ref, w_ref, x_ref, gl_ref, gc_ref, o_ref, *, tm, ctx):
    parts = []
    for h in range(of_ref.shape[2] // A_DK):
        sl = slice(h * A_DK, (h + 1) * A_DK)
        y = of_ref[0, :, sl].astype(F32) + ob_ref[0, :, sl].astype(F32)
        ms = jnp.mean(y * y, axis=-1, keepdims=True)
        yn = y * lax.rsqrt(ms + RMS_EPS) * nw_ref[...]
        parts.append((yn * _silu(g_ref[0, :, sl])).astype(BF16))
    _residual_out(jnp.concatenate(parts, axis=1), w_ref, x_ref, gl_ref, gc_ref, o_ref, tm, ctx)


OUT_TK = 1024


def _residual_specs(w_spec, tm, n, nb):
    return [w_spec,
            pl.BlockSpec((1, tm, n), lambda b, i, k: (b, i, 0)),
            pl.BlockSpec((1, 1, n), lambda b, i, k: (b, 0, 2)),
            pl.BlockSpec((1, 1, n), lambda b, i, k: (nb, 0, 2))]


def _hgrn2_out(o_f, o_b, z, onorm_w, w_out, xs, mod, ctx):
    nb, t, w = o_f.shape
    n = _wshape(w_out)[1]
    tm = _tile(t, 528, BF16_ROWS)
    tk = _tile(w, OUT_TK, A_DK)
    gate0 = 4 * (w // tk)
    sl = pl.BlockSpec((1, tm, tk), lambda b, i, k: (b, i, k))
    w_out, w_spec = _weight(w_out, tk, n, lambda b, i, k: (k, 0))
    return pl.pallas_call(
        functools.partial(_hgrn2_out_kernel, tm=tm, ctx=ctx),
        grid=(nb, t // tm, w // tk),
        in_specs=[sl, sl, pl.BlockSpec((1, tm, tk), lambda b, i, k: (b, i, gate0 + k)),
                  pl.BlockSpec((1, A_DK), lambda b, i, k: (0, 0))] + _residual_specs(w_spec, tm, n, nb),
        out_specs=pl.BlockSpec((1, tm, n), lambda b, i, k: (b, i, 0)),
        out_shape=jax.ShapeDtypeStruct((nb, t, n), F32),
        compiler_params=_cparams(("parallel", "parallel", "arbitrary")),
        name="hgrn2_out",
    )(o_f, o_b, z, onorm_w.reshape(1, A_DK), w_out, xs, mod, mod)


def _hgrn2_layer(xs, nw, w_in, lb, onorm_w, w_out, mod, ctx, tb=None):
    width = _wshape(w_out)[0]
    z = _in_proj(xs, nw, mod, ctx, w_in, width, lb)
    o_f = _gla(z, width // A_DK, ctx, False, tb)
    o_b = _gla(z, width // A_DK, ctx, True, tb)
    return _hgrn2_out(o_f, o_b, z, onorm_w, w_out, xs, mod, ctx)


GRID_W = 64
ROPE_BASE = 10000.0
HEAD = 64
QB = 128
LANES = 128


def _rope_tables(t_all, ctx):
    quarter = HEAD // 4
    inv = ROPE_BASE ** (-jnp.arange(quarter, dtype=F32) / quarter)
    tl = jnp.arange(t_all - ctx)
    row = (tl // GRID_W).astype(F32)
    col = (tl % GRID_W).astype(F32)
    hdim = jnp.arange(LANES) % HEAD
    use_col = hdim >= 2 * quarter
    second = (hdim % (2 * quarter)) >= quarter
    pos = jnp.where(use_col[None, :], col[:, None], row[:, None])
    ang = pos * inv[hdim % quarter][None, :]
    cos = jnp.concatenate([jnp.ones((ctx, LANES), F32), jnp.cos(ang)], axis=0)
    sin = jnp.concatenate([jnp.zeros((ctx, LANES), F32),
                           jnp.where(second[None, :], jnp.sin(ang), -jnp.sin(ang))], axis=0)
    return cos, sin


def _rope(x, cos, sin, first):
    partner = jnp.where(first, pltpu.roll(x, LANES - HEAD // 4, 1), pltpu.roll(x, HEAD // 4, 1))
    return x * cos + partner * sin


def _swa_kernel(sink_ref, q_ref, *refs, ncb, seq, ngroups):
    (kc_ref, vc_ref, kp_ref, kq_ref, kn_ref, vp_ref, vq_ref, vn_ref,
     cq_ref, sq_ref, cp_ref, sp_ref, cn_ref, sn_ref, o_ref) = refs[-15:]
    gate_refs = refs[:-15]
    gate_tiles = gate_refs[0].shape[2] // LANES

    def gate_tile(col):
        c = col % gate_tiles
        return gate_refs[col // gate_tiles][0, :, c * LANES:(c + 1) * LANES]

    i = pl.program_id(1)
    lane = lax.broadcasted_iota(jnp.int32, (1, LANES), 1)
    first = (lane % (HEAD // 2)) < (HEAD // 4)
    left = lane < HEAD
    ntile = kc_ref.shape[2] // LANES

    def tile(x, c):
        return x[:, c * LANES:(c + 1) * LANES]

    kwin = [(kp_ref[0], cp_ref[...], sp_ref[...]), (kq_ref[0], cq_ref[...], sq_ref[...]),
            (kn_ref[0], cn_ref[...], sn_ref[...])]
    kt = [jnp.concatenate([tile(kc_ref[0], c)] + [_rope(tile(kk, c), cs, sn, first) for kk, cs, sn in kwin], axis=0)
          for c in range(ntile)]
    vt = [jnp.concatenate([tile(vc_ref[0], c), tile(vp_ref[0], c), tile(vq_ref[0], c), tile(vn_ref[0], c)], axis=0)
          for c in range(ntile)]
    nk = kt[0].shape[0]
    nctx = nk - 3 * QB

    r = lax.broadcasted_iota(jnp.int32, (QB, QB), 0)
    cidx = lax.broadcasted_iota(jnp.int32, (QB, QB), 1)
    qblk = i - ncb
    open_if = lambda cond: jnp.where(cond, 0.0, NEG_INF)
    b_prev = open_if(cidx >= r) + open_if(qblk >= 1)
    b_cur = jnp.zeros((QB, QB), F32) + open_if(qblk >= 0)
    b_next = open_if(cidx <= r) + open_if(jnp.logical_and(qblk >= 0, (qblk + 2) * QB <= seq))
    bias = jnp.concatenate([jnp.zeros((QB, nctx), F32), b_prev, b_cur, b_next], axis=1)

    in_even = lax.broadcasted_iota(jnp.int32, (2 * nk, LANES), 0) < nk
    in_left = lax.broadcasted_iota(jnp.int32, (2 * nk, LANES), 1) < HEAD
    ones2 = jnp.where(in_even == in_left, 1.0, 0.0).astype(BF16)

    scale = HEAD ** -0.5
    heads_per_group = q_ref.shape[2] // HEAD // ngroups
    pairs = heads_per_group // 2
    for g in range(ngroups):
        c, even = g // 2, g % 2 == 0
        own = left if even else jnp.logical_not(left)
        k_own = jnp.where(own, kt[c], 0.0)
        v_own = jnp.where(own, vt[c], 0.0)
        k_swp = pltpu.roll(k_own, HEAD, 1)
        v_swp = pltpu.roll(v_own, HEAD, 1)
        kk2 = jnp.concatenate([k_own, k_swp] if even else [k_swp, k_own], axis=0).astype(BF16)
        vv2 = jnp.concatenate([v_own, v_swp] if even else [v_swp, v_own], axis=0).astype(BF16)
        q8 = jnp.concatenate(
            [_rope(tile(q_ref[0], g * pairs + p), cq_ref[...], sq_ref[...], first) * scale for p in range(pairs)],
            axis=0)
        s8 = _dot(q8, kk2, 1, 1)
        probs, sinks = [], []
        for p in range(pairs):
            pe2, snk2 = [], []
            for e in range(2):
                sk = sink_ref[g * heads_per_group + 2 * p + e]
                s = s8[p * QB:(p + 1) * QB, e * nk:(e + 1) * nk] + bias
                m = jnp.maximum(jnp.max(s, axis=-1, keepdims=True), sk)
                pe2.append(jnp.exp((s - m).astype(BF16)))
                snk2.append(jnp.exp(sk - m))
            probs.append(jnp.concatenate(pe2, axis=1))
            sinks.append(jnp.where(left, snk2[0], snk2[1]))
        p8 = jnp.concatenate(probs, axis=0)
        o8 = _dot(p8, vv2)
        d8 = _dot(p8, ones2)
        for p in range(pairs):
            col = g * pairs + p
            o = o8[p * QB:(p + 1) * QB] / (d8[p * QB:(p + 1) * QB] + sinks[p])
            o_ref[0, :, col * LANES:(col + 1) * LANES] = (o * _silu(gate_tile(col))).astype(o_ref.dtype)


def _swa(z, sink, ctx, nkv):
    nb, t, ncol = z.shape
    kvw = nkv * HEAD
    w = (ncol - 2 * kvw) // 2
    assert t % QB == 0 and ctx % QB == 0 and w % kvw == 0
    nblk, ncb = t // QB, ctx // QB
    kcol, vcol = w // kvw, w // kvw + 1
    gw = math.gcd(w, w + 2 * kvw)
    assert gw % LANES == 0
    gates = [pl.BlockSpec((1, QB, gw), functools.partial(lambda b, i, c: (b, i, c), c=(w + 2 * kvw) // gw + n))
             for n in range(w // gw)]
    cos, sin = _rope_tables(t, ctx)
    prev = lambda i: jnp.maximum(i - 1, 0)
    nxt = lambda i: jnp.minimum(i + 1, nblk - 1)
    wide = lambda col: pl.BlockSpec((1, QB, w), lambda b, i: (b, i, col))
    kv = lambda col, f: pl.BlockSpec((1, QB, kvw), lambda b, i: (b, f(i), col))
    kvc = lambda col: pl.BlockSpec((1, ctx, kvw), lambda b, i: (b, 0, col))
    tab = lambda f: pl.BlockSpec((QB, LANES), lambda b, i: (f(i), 0))
    same = lambda i: i
    return pl.pallas_call(
        functools.partial(_swa_kernel, ncb=ncb, seq=t - ctx, ngroups=nkv),
        grid=(nb, nblk),
        in_specs=[pl.BlockSpec(memory_space=pltpu.SMEM), wide(0)] + gates + [kvc(kcol), kvc(vcol),
                  kv(kcol, prev), kv(kcol, same), kv(kcol, nxt), kv(vcol, prev), kv(vcol, same), kv(vcol, nxt),
                  tab(same), tab(same), tab(prev), tab(prev), tab(nxt), tab(nxt)],
        out_specs=wide(0),
        out_shape=jax.ShapeDtypeStruct((nb, t, w), BF16),
        compiler_params=_cparams(("parallel", "arbitrary")),
        name="swa",
    )(sink, z, *([z] * len(gates)), z, z, z, z, z, z, z, z, cos, sin, cos, sin, cos, sin)


C_GN_EPS = 64e-5
C_LORA_PAD = 128


def _prepc_kernel(x_ref, xp_ref, xn_ref, nw_ref, shl_ref, scl_ref, shc_ref, scc_ref, mu_ref, o_ref, scr,
                  *, tm, ctx, t_all):
    base = pl.program_id(1) * tm
    nw = nw_ref[...]
    mods = (scl_ref[0], shl_ref[0], scc_ref[0], shc_ref[0])
    row = base + lax.broadcasted_iota(jnp.int32, (tm, 1), 0)
    r8 = lax.broadcasted_iota(jnp.int32, (8, 1), 0)
    h = _normmod(x_ref[0], nw, row < ctx, *mods)
    scr[0:8, :] = _normmod(xp_ref[0], nw, (base - 8 + r8) < ctx, *mods)
    scr[8:tm + 8, :] = h
    scr[tm + 8:tm + 16, :] = _normmod(xn_ref[0], nw, (base + tm + r8) < ctx, *mods)
    has_prev = jnp.logical_and(row != 0, row != ctx)
    has_next = jnp.logical_and(row != ctx - 1, row != t_all - 1)
    xx = 0.5 * (jnp.where(has_prev, scr[7:tm + 7, :], 0.0) + jnp.where(has_next, scr[9:tm + 9, :], 0.0)) - h
    hb, xb = h.astype(BF16), xx.astype(BF16)
    for n in range(o_ref.shape[0]):
        o_ref[n, 0] = hb + xb * mu_ref[n:n + 1, :].astype(BF16)


def _prepc(xs, nw, mod, mu, ctx):
    nb, t, d = xs.shape
    tm = _tile(t, 264)
    nmix = mu.shape[0]
    last8 = t // 8 - 1
    return pl.pallas_call(
        functools.partial(_prepc_kernel, tm=tm, ctx=ctx, t_all=t),
        grid=(nb, t // tm),
        in_specs=[pl.BlockSpec((1, tm, d), lambda b, i: (b, i, 0)),
                  pl.BlockSpec((1, 8, d), lambda b, i: (b, jnp.maximum(i * (tm // 8) - 1, 0), 0)),
                  pl.BlockSpec((1, 8, d), lambda b, i: (b, jnp.minimum((i + 1) * (tm // 8), last8), 0)),
                  pl.BlockSpec((1, d), lambda b, i: (0, 0))] + _mod_specs(d, nb) +
                 [pl.BlockSpec((nmix, d), lambda b, i: (0, 0))],
        out_specs=pl.BlockSpec((nmix, 1, tm, d), lambda b, i: (0, b, i, 0)),
        out_shape=jax.ShapeDtypeStruct((nmix, nb, t, d), BF16),
        scratch_shapes=[pltpu.VMEM((tm + 16, d), F32)],
        compiler_params=_cparams(("parallel", "parallel")),
        name="prep_rwkv",
    )(xs, xs, xs, nw.reshape(1, d), mod, mod, mod, mod, mu)


def _lora_kernel(xw_ref, xa_ref, w1_ref, a1_ref, w2_ref, a2_ref, w0_ref, a0_ref, lw_ref, ic_ref):
    t1 = jnp.tanh(_dot(xw_ref[0, 0], w1_ref[...]))
    t2 = _dot(xa_ref[0, 0], a1_ref[...])
    for d in range(2):
        sl = slice(d * C_LORA_PAD, (d + 1) * C_LORA_PAD)
        lw_ref[d, 0] = -math.exp(-0.5) * _sigmoid(w0_ref[d] + _dot(t1[:, sl], w2_ref[d]))
        ic_ref[d, 0] = _sigmoid(a0_ref[d] + _dot(t2[:, sl], a2_ref[d]))


def _pad_lora(w_in, w_out):
    r = w_in.shape[2]
    a = jnp.pad(w_in, ((0, 0), (0, 0), (0, C_LORA_PAD - r)))
    a = jnp.concatenate([a[0], a[1]], axis=1).astype(BF16)
    b = jnp.pad(w_out, ((0, 0), (0, C_LORA_PAD - r), (0, 0))).astype(BF16)
    return a, b


def _lora(mix, w0, w1, w2, a0, a1, a2):
    _, nb, t, d = mix.shape
    w = w0.shape[1]
    tm = _tile(t, 264)
    w1p, w2p = _pad_lora(w1, w2)
    a1p, a2p = _pad_lora(a1, a2)
    full = lambda shape: pl.BlockSpec(shape, lambda b, i: (0,) * len(shape))
    out = pl.BlockSpec((2, 1, tm, w), lambda b, i: (0, b, i, 0))
    return pl.pallas_call(
        _lora_kernel,
        grid=(nb, t // tm),
        in_specs=[pl.BlockSpec((1, 1, tm, d), lambda b, i: (4, b, i, 0)),
                  pl.BlockSpec((1, 1, tm, d), lambda b, i: (5, b, i, 0)),
                  full(w1p.shape), full(a1p.shape), full(w2p.shape), full(a2p.shape),
                  full((2, 1, w)), full((2, 1, w))],
        out_specs=[out, out],
        out_shape=[jax.ShapeDtypeStruct((2, nb, t, w), F32)] * 2,
        compiler_params=_cparams(("parallel", "parallel")),
        name="rwkv_lora",
    )(mix, mix, w1p, a1p, w2p, a2p, w0.reshape(2, 1, w), a0.reshape(2, 1, w))


RWKV_PAIRS_PER_STEP = 8


def _rwkv_consts(reverse):
    t2 = lax.broadcasted_iota(jnp.int32, (CHUNK, LANES), 0)
    s2 = lax.broadcasted_iota(jnp.int32, (CHUNK, LANES), 1) % CHUNK
    strict = jnp.where((s2 > t2) if reverse else (s2 < t2), 1.0, 0.0)
    incl = jnp.where((s2 >= t2) if reverse else (s2 <= t2), 1.0, 0.0)
    eye = jnp.where(t2 == s2, 1.0, 0.0)
    vi = lax.broadcasted_iota(jnp.int32, (LANES, LANES), 0) < HEAD
    ki = lax.broadcasted_iota(jnp.int32, (LANES, LANES), 1) < HEAD
    return (strict, incl, eye), jnp.where(vi == ki, 1.0, 0.0)


def _stack(x, left):
    x = x.astype(BF16)
    zero = jnp.zeros_like(x)
    return jnp.concatenate([jnp.where(left, x, zero), jnp.where(left, zero, x)], axis=0)


def _rwkv_prepare(r, k, v, lw, ic, kkw, kaw, rkw, left, reverse):
    stack = functools.partial(_stack, left=left)

    def segsum(x):
        sl = jnp.sum(jnp.where(left, x, 0.0), axis=-1, keepdims=True)
        sr = jnp.sum(jnp.where(left, 0.0, x), axis=-1, keepdims=True)
        return jnp.where(left, sl, sr)

    kx = k * kkw
    kk = kx / jnp.maximum(jnp.sqrt(segsum(kx * kx)), 1e-12)
    kd = k * (1.0 + (ic - 1.0) * kaw)
    b = kk * ic
    bonus = segsum(r * kd * rkw) * v
    ti = lax.broadcasted_iota(jnp.int32, (CHUNK, CHUNK), 0)
    si = lax.broadcasted_iota(jnp.int32, (CHUNK, CHUNK), 1)
    lam = _dot_sel(((si >= ti) if reverse else (si <= ti)).astype(BF16), lw)
    end = 0 if reverse else CHUNK - 1
    lam_c = lam[end:end + 1, :]
    einv = jnp.exp(-lam)
    ebar = jnp.exp(lam_c - lam)
    at = (-kk * jnp.exp(lam - lw)).astype(BF16)
    rt = (r * jnp.exp(lam)).astype(BF16)
    return dict(at=at, rt=rt, ar=jnp.concatenate([at, rt], axis=0),
                kb=jnp.concatenate([stack(kd * einv), stack(b * einv)], axis=0),
                v=v.astype(BF16), vst=stack(v), kbar=(kd * ebar).astype(BF16), bbar=(b * ebar).astype(BF16),
                dec=jnp.exp(lam_c), bonus=bonus)


def _inv_unit_lower(ns, eye, left):
    stack = functools.partial(_stack, left=left)
    ps = [eye + n for n in ns]
    ms = [_dot(n, stack(n)) for n in ns]
    for j in range(CHUNK.bit_length() - 4):
        both = [_dot(jnp.concatenate([m.astype(BF16), p.astype(BF16)], axis=0), stack(m)) for m, p in zip(ms, ps)]
        ps = [p + b[CHUNK:] for p, b in zip(ps, both)]
        ms = [b[:CHUNK] for b in both]
    ps = [p + _dot(p, stack(m)) for p, m in zip(ps, ms)]
    out = []
    for n, p in zip(ns, ps):
        tb = p.astype(BF16)
        nh, nl = _split2(n)
        res = _dot(jnp.concatenate([nh, nl], axis=0), stack(tb))
        e = eye - tb.astype(F32) + (res[:CHUNK] + res[CHUNK:])
        out.append(tb.astype(F32) + _dot(tb, stack(e)))
    return out


def _rwkv_kernel(r_ref, k_ref, v_ref, lw_ref, ic_ref, kk_ref, ka_ref, rk_ref, y_ref, bon_ref, ht_ref, mask_ref,
                 head_ref, *, tb, reverse, pps):
    @pl.when(pl.program_id(2) == 0)
    def _():
        ht_ref[...] = jnp.zeros_like(ht_ref)
        masks, same_head = _rwkv_consts(reverse)
        for n, m in enumerate(masks):
            mask_ref[n] = m
        head_ref[...] = same_head

    left = lax.broadcasted_iota(jnp.int32, (1, LANES), 1) < HEAD
    stack = functools.partial(_stack, left=left)
    nch = tb // CHUNK
    order = list(range(nch - 1, -1, -1) if reverse else range(nch))
    items = [(p, c) for p in range(pps) for c in order]

    def blk(ref, p, c):
        return ref[0, c * CHUNK:(c + 1) * CHUNK, p * LANES:(p + 1) * LANES]

    def blkd(ref, p, c):
        return ref[0, 0, c * CHUNK:(c + 1) * CHUNK, p * LANES:(p + 1) * LANES]

    def par(ref, p):
        return ref[:, p * LANES:(p + 1) * LANES]

    pre = [_rwkv_prepare(blk(r_ref, p, c), blk(k_ref, p, c), blk(v_ref, p, c), blkd(lw_ref, p, c),
                         blkd(ic_ref, p, c), par(kk_ref, p), par(ka_ref, p), par(rk_ref, p), left, reverse)
           for p, c in items]
    gs = [_dot(x["ar"], x["kb"], 1, 1) for x in pre]
    a_ak = [(g[:CHUNK, :LANES] * mask_ref[0]).astype(BF16) for g in gs]
    a_rk = [(g[CHUNK:, :LANES] * mask_ref[1]).astype(BF16) for g in gs]
    a_rb = [(g[CHUNK:, LANES:] * mask_ref[1]).astype(BF16) for g in gs]
    tinv = _inv_unit_lower([g[:CHUNK, LANES:] * mask_ref[0] for g in gs], mask_ref[2], left)
    av = [_dot(jnp.concatenate([ak, rk], axis=0), x["vst"]) for ak, rk, x in zip(a_ak, a_rk, pre)]
    wu = [_dot(t, jnp.concatenate([stack(x["at"]), stack(a[:CHUNK])], axis=1)).astype(BF16)
          for t, x, a in zip(tinv, pre, av)]
    ry = [_dot(rb, jnp.concatenate([stack(w[:, :LANES]), stack(w[:, LANES:])], axis=1)) for rb, w in zip(a_rb, wu)]
    rw = [(x["rt"].astype(F32) + y[:, :LANES]).astype(BF16) for x, y in zip(pre, ry)]
    y0 = [a[CHUNK:] + y[:, LANES:] for a, y in zip(av, ry)]
    ft = [_dot(jnp.concatenate([x["v"], w[:, LANES:]], axis=0),
               jnp.concatenate([x["kbar"], x["bbar"]], axis=0), 0, 0) * head_ref[...] for x, w in zip(pre, wu)]
    gm = [(_dot(x["bbar"], w[:, :LANES], 0, 0) * head_ref[...]).astype(BF16) for x, w in zip(pre, wu)]

    hts = [ht_ref[p] for p in range(pps)]
    for step in range(nch):
        for p in range(pps):
            n = p * nch + step
            c = items[n][1]
            ht = hts[p]
            htb = ht.astype(BF16)
            y = y0[n] + _dot(rw[n], htb, 1, 1)
            y_ref[0, c * CHUNK:(c + 1) * CHUNK, p * LANES:(p + 1) * LANES] = y.astype(y_ref.dtype)
            bon_ref[0, c * CHUNK:(c + 1) * CHUNK, p * LANES:(p + 1) * LANES] = pre[n]["bonus"].astype(bon_ref.dtype)
            hts[p] = ht * pre[n]["dec"] + ft[n] + _dot(htb, gm[n], 1, 1)
    for p in range(pps):
        ht_ref[p] = hts[p]


def _rwkv_scan(r, k, v, lw, ic, k_k, k_a, r_k, ctx, reverse, tb=None):
    nb, t, w = r.shape
    tb = tb or _tile(ctx, 256, CHUNK)
    assert t % tb == 0 and ctx % tb == 0
    nblk, ncb = t // tb, ctx // tb
    tmap = _time_block_map(nblk, ncb, reverse)
    d = 1 if reverse else 0
    pps = RWKV_PAIRS_PER_STEP
    wide = pps * LANES
    assert w % wide == 0
    blk = pl.BlockSpec((1, tb, wide), lambda b, p, i: (b, tmap(i), p))
    blkd = pl.BlockSpec((1, 1, tb, wide), lambda b, p, i: (d, b, tmap(i), p))
    par = pl.BlockSpec((1, wide), lambda b, p, i: (0, p))
    return pl.pallas_call(
        functools.partial(_rwkv_kernel, tb=tb, reverse=reverse, pps=pps),
        grid=(nb, w // wide, nblk),
        in_specs=[blk, blk, blk, blkd, blkd, par, par, par],
        out_specs=[blk, blk],
        out_shape=[jax.ShapeDtypeStruct((nb, t, w), BF16)] * 2,
        scratch_shapes=[pltpu.VMEM((pps, LANES, LANES), F32), pltpu.VMEM((3, CHUNK, LANES), F32),
                        pltpu.VMEM((LANES, LANES), F32)],
        compiler_params=_cparams(("parallel", "parallel", "arbitrary")),
        name="rwkv_bwd" if reverse else "rwkv_fwd",
    )(r, k, v, lw, ic, k_k.reshape(1, w), k_a.reshape(1, w), r_k.reshape(1, w))


def _rwkv_out_kernel(yf_ref, yb_ref, bf_ref, bb_ref, g_ref, lnw_ref, lnb_ref, w_ref, x_ref, gl_ref, gc_ref, o_ref,
                     *, tm, ctx):
    left = lax.broadcasted_iota(jnp.int32, (1, LANES), 1) < HEAD

    def segmean(x):
        sl = jnp.sum(jnp.where(left, x, 0.0), axis=-1, keepdims=True)
        sr = jnp.sum(jnp.where(left, 0.0, x), axis=-1, keepdims=True)
        return jnp.where(left, sl, sr) * (1.0 / HEAD)

    parts = []
    for p in range(yf_ref.shape[2] // LANES):
        sl = slice(p * LANES, (p + 1) * LANES)
        y = yf_ref[0, :, sl].astype(F32) + yb_ref[0, :, sl].astype(F32)
        dlt = y - segmean(y)
        zn = dlt * lax.rsqrt(segmean(dlt * dlt) + C_GN_EPS)
        o = (zn * lnw_ref[:, sl] + lnb_ref[:, sl]
             + bf_ref[0, :, sl].astype(F32) + bb_ref[0, :, sl].astype(F32))
        parts.append((o * _silu(g_ref[0, :, sl])).astype(BF16))
    _residual_out(jnp.concatenate(parts, axis=1), w_ref, x_ref, gl_ref, gc_ref, o_ref, tm, ctx)


def _rwkv_out(y_f, y_b, bon_f, bon_b, gate, ln_w, ln_b, w_out, xs, mod, ctx):
    nb, t, w = y_f.shape
    n = _wshape(w_out)[1]
    tm = _tile(t, 528, BF16_ROWS)
    tk = _tile(w, OUT_TK, LANES)
    sl = pl.BlockSpec((1, tm, tk), lambda b, i, k: (b, i, k))
    par = pl.BlockSpec((1, tk), lambda b, i, k: (0, k))
    w_out, w_spec = _weight(w_out, tk, n, lambda b, i, k: (k, 0))
    return pl.pallas_call(
        functools.partial(_rwkv_out_kernel, tm=tm, ctx=ctx),
        grid=(nb, t // tm, w // tk),
        in_specs=[sl] * 5 + [par, par] + _residual_specs(w_spec, tm, n, nb),
        out_specs=pl.BlockSpec((1, tm, n), lambda b, i, k: (b, i, 0)),
        out_shape=jax.ShapeDtypeStruct((nb, t, n), F32),
        compiler_params=_cparams(("parallel", "parallel", "arbitrary")),
        name="rwkv_out",
    )(y_f, y_b, bon_f, bon_b, gate, ln_w.reshape(1, w), ln_b.reshape(1, w), w_out, xs, mod, mod)


def _rwkv_layer(mix, w_in, w0, w1, w2, a0, a1, a2, k_k, k_a, r_k, ln_w, ln_b, w_out, xs, mod, ctx, tb=None):
    w_arr, w_idx = w_in if isinstance(w_in, tuple) else (w_in, ())
    r, k, v, gate = (_matmul(mix, (w_arr, tuple(w_idx) + (n,)), sel=n) for n in range(4))
    lw, ic = _lora(mix, w0, w1, w2, a0, a1, a2)
    y_f, bon_f = _rwkv_scan(r, k, v, lw, ic, k_k, k_a, r_k, ctx, False, tb)
    y_b, bon_b = _rwkv_scan(r, k, v, lw, ic, k_k, k_a, r_k, ctx, True, tb)
    return _rwkv_out(y_f, y_b, bon_f, bon_b, gate, ln_w, ln_b, w_out, xs, mod, ctx)


def _swa_mix(xs, nw, mod, w_in, sink, ctx, width):
    kv2 = _wshape(w_in)[1] - 2 * width
    return _swa(_in_proj(xs, nw, mod, ctx, w_in), sink, ctx, kv2 // 2 // HEAD)


def _final_norm_kernel(x_ref, w_ref, o_ref):
    x = x_ref[0]
    ms = jnp.mean(x * x, axis=-1, keepdims=True)
    o_ref[0] = x * lax.rsqrt(ms + RMS_EPS) * w_ref[...]


def _final_norm(xs, w, ctx):
    nb, t, d = xs.shape
    seq = t - ctx
    tm = _tile(math.gcd(ctx, seq), 1024)
    off = ctx // tm
    return pl.pallas_call(
        _final_norm_kernel,
        grid=(nb, seq // tm),
        in_specs=[pl.BlockSpec((1, tm, d), lambda b, i: (b, i + off, 0)),
                  pl.BlockSpec((1, d), lambda b, i: (0, 0))],
        out_specs=pl.BlockSpec((1, tm, d), lambda b, i: (b, i, 0)),
        out_shape=jax.ShapeDtypeStruct((nb, seq, d), F32),
        compiler_params=_cparams(("parallel", "parallel")),
        name="final_norm",
    )(xs, w.reshape(1, d))


def kernel(x, c, ctx, c_ctx, norm_w, mod_w, mod_b, a_w_in, a_lb_raw, a_onorm_w, a_w_out, b_w_in, b_sink, b_w_out,
           c_mu, c_w_in, c_w0, c_w1, c_w2, c_a0, c_a1, c_a2, c_k_k, c_k_a, c_r_k, c_ln_w, c_ln_b, c_w_out,
           final_norm_w):
    nb, seq, d = x.shape
    nctx = ctx.shape[1]
    depth = norm_w.shape[0]
    xs = jnp.concatenate([ctx, x], axis=1)
    cvec = jnp.zeros((8, d), F32).at[:nb].set(c).at[nb].set(c_ctx)
    mod_all = _modulation(cvec, mod_w, mod_b)
    lb_all = jnp.cumsum(jax.nn.softmax(a_lb_raw.astype(F32), axis=0), axis=0)
    lb_all = lb_all - lb_all[0]
    a_w_in, a_w_out, b_w_in, b_w_out, c_w_in, c_w_out = (
        w.astype(BF16) for w in (a_w_in, a_w_out, b_w_in, b_w_out, c_w_in, c_w_out))
    for i in range(depth):
        j, kind = i // 3, i % 3
        mod = mod_all[i].reshape(8, 1, 3 * d)
        if kind == 0:
            xs = _hgrn2_layer(xs, norm_w[i], (a_w_in, (j,)), lb_all[j] if j else None, a_onorm_w[j],
                              (a_w_out, (j,)), mod, nctx)
        elif kind == 1:
            m = _swa_mix(xs, norm_w[i], mod, (b_w_in, (j,)), b_sink[j], nctx, b_w_out.shape[1])
            xs = _matmul_residual(m, (b_w_out, (j,)), xs, mod, nctx)
        else:
            mix = _prepc(xs, norm_w[i], mod, c_mu[j], nctx)
            xs = _rwkv_layer(mix, (c_w_in, (j,)), c_w0[j], c_w1[j], c_w2[j], c_a0[j], c_a1[j], c_a2[j],
                             c_k_k[j], c_k_a[j], c_r_k[j], c_ln_w[j], c_ln_b[j], (c_w_out, (j,)),
                             xs, mod, nctx)
    return _final_norm(xs, final_norm_w, nctx)
```

```python
import functools
import math

import jax
import jax.numpy as jnp
from jax import lax
from jax.experimental import pallas as pl
from jax.experimental.pallas import tpu as pltpu

F32 = jnp.float32
BF16 = jnp.bfloat16
NEG_INF = float("-inf")

RMS_EPS = 1e-6
CHUNK = 64
SUB = 16
NSUB = CHUNK // SUB
A_DK = 128
SUBLANES = 8
VMEM_LIMIT = 56 * 1024 * 1024


def _cparams(sem):
    return pltpu.CompilerParams(dimension_semantics=sem, vmem_limit_bytes=VMEM_LIMIT)


def _tile(n, target, mult=8):
    best = None
    for t in range(mult, min(n, target) + 1, mult):
        if n % t == 0:
            best = t
    assert best is not None, (n, target, mult)
    return best


def _dot(a, b, ca=1, cb=0):
    return lax.dot_general(a.astype(BF16), b.astype(BF16), (((ca,), (cb,)), ((), ())),
                           preferred_element_type=F32)


def _split2(x):
    hi = x.astype(BF16)
    lo = (x - hi.astype(F32)).astype(BF16)
    return hi, lo


def _split3(x):
    hi = x.astype(BF16)
    r = x - hi.astype(F32)
    mid = r.astype(BF16)
    lo = (r - mid.astype(F32)).astype(BF16)
    return hi, mid, lo


def _dot_sel(sel, x):
    hi, mid, lo = _split3(x)
    return _dot(sel, hi) + _dot(sel, mid) + _dot(sel, lo)


def _dot_hi(a, b, ca=1, cb=0):
    ah, al = _split2(a)
    bh, bl = _split2(b)
    return _dot(ah, bh, ca, cb) + _dot(ah, bl, ca, cb) + _dot(al, bh, ca, cb)


def _cumsum_rows(x, reverse):
    ntile = x.shape[0] // SUBLANES
    sub = lax.broadcasted_iota(jnp.int32, (SUBLANES, 1), 0)
    tiles = [x[j * SUBLANES:(j + 1) * SUBLANES] for j in range(ntile)]
    for s in (1, 2, 4):
        if reverse:
            tiles = [y + jnp.where(sub < SUBLANES - s, pltpu.roll(y, SUBLANES - s, 0), 0.0) for y in tiles]
        else:
            tiles = [y + jnp.where(sub >= s, pltpu.roll(y, s, 0), 0.0) for y in tiles]
    edge = 0 if reverse else SUBLANES - 1
    order = range(ntile - 1, -1, -1) if reverse else range(ntile)
    carry = None
    out = [None] * ntile
    for j in order:
        out[j] = tiles[j] if carry is None else tiles[j] + carry
        total = tiles[j][edge:edge + 1, :]
        carry = total if carry is None else carry + total
    return jnp.concatenate(out, axis=0)


def _sigmoid(x):
    return 1.0 / (1.0 + jnp.exp(-x))


def _silu(x):
    return x * _sigmoid(x)


def _mod_kernel(c_ref, w_ref, b_ref, o_ref):
    s = _silu(c_ref[...])
    o_ref[0] = _dot_hi(s, w_ref[0]) + b_ref[0]


def _modulation(cvec, mod_w, mod_b):
    depth, d, n = mod_w.shape
    tn = _tile(n, 512, 128)
    return pl.pallas_call(
        _mod_kernel,
        grid=(depth, n // tn),
        in_specs=[pl.BlockSpec((8, d), lambda i, j: (0, 0)),
                  pl.BlockSpec((1, d, tn), lambda i, j: (i, 0, j)),
                  pl.BlockSpec((1, 1, tn), lambda i, j: (i, 0, j))],
        out_specs=pl.BlockSpec((1, 8, tn), lambda i, j: (i, 0, j)),
        out_shape=jax.ShapeDtypeStruct((depth, 8, n), F32),
        compiler_params=_cparams(("parallel", "parallel")),
        name="modulation",
    )(cvec, mod_w, mod_b.reshape(depth, 1, n))


def _normmod(x, nw, is_ctx, scl, shl, scc, shc):
    ms = jnp.mean(x * x, axis=-1, keepdims=True)
    y = x * lax.rsqrt(ms + RMS_EPS) * nw
    sc = jnp.where(is_ctx, scc, scl)
    sh = jnp.where(is_ctx, shc, shl)
    return y * (1.0 + sc) + sh


def _prep_kernel(x_ref, nw_ref, shl_ref, scl_ref, shc_ref, scc_ref, o_ref, *, tm, ctx):
    row = pl.program_id(1) * tm + lax.broadcasted_iota(jnp.int32, (tm, 1), 0)
    h = _normmod(x_ref[0], nw_ref[...], row < ctx, scl_ref[0], shl_ref[0], scc_ref[0], shc_ref[0])
    o_ref[0] = h.astype(o_ref.dtype)


def _mod_specs(d, nb):
    return [pl.BlockSpec((1, 1, d), lambda b, i, *_: (b, 0, 0)),
            pl.BlockSpec((1, 1, d), lambda b, i, *_: (b, 0, 1)),
            pl.BlockSpec((1, 1, d), lambda b, i, *_: (nb, 0, 0)),
            pl.BlockSpec((1, 1, d), lambda b, i, *_: (nb, 0, 1))]


def _prep(xs, nw, mod, ctx):
    nb, t, d = xs.shape
    tm = _tile(t, 1056)
    return pl.pallas_call(
        functools.partial(_prep_kernel, tm=tm, ctx=ctx),
        grid=(nb, t // tm),
        in_specs=[pl.BlockSpec((1, tm, d), lambda b, i: (b, i, 0)),
                  pl.BlockSpec((1, d), lambda b, i: (0, 0))] + _mod_specs(d, nb),
        out_specs=pl.BlockSpec((1, tm, d), lambda b, i: (b, i, 0)),
        out_shape=jax.ShapeDtypeStruct((nb, t, d), BF16),
        compiler_params=_cparams(("parallel", "parallel")),
        name="prep",
    )(xs, nw.reshape(1, d), mod, mod, mod, mod)


def _mm_kernel(a_ref, w_ref, o_ref):
    o_ref[0] = _dot(a_ref[0], w_ref[...]).astype(o_ref.dtype)


def _mm_sel_kernel(a_ref, w_ref, o_ref):
    o_ref[0] = _dot(a_ref[0, 0], w_ref[...]).astype(o_ref.dtype)


def _weight(w, rows, cols, where):
    arr, idx = w if isinstance(w, tuple) else (w, ())
    return arr, pl.BlockSpec((None,) * len(idx) + (rows, cols), lambda *g: tuple(idx) + where(*g))


def _wshape(w):
    return (w[0] if isinstance(w, tuple) else w).shape[-2:]


def _matmul(a, w, out_dtype=F32, sel=None):
    nb, t, k = a.shape[-3:]
    n = _wshape(w)[1]
    tm = _tile(t, 1056)
    tn = _tile(n, 1024, 128)
    if sel is None:
        a_spec = pl.BlockSpec((1, tm, k), lambda b, i, j: (b, i, 0))
    else:
        a_spec = pl.BlockSpec((1, 1, tm, k), lambda b, i, j: (sel, b, i, 0))
    w_arr, w_spec = _weight(w, k, tn, lambda b, i, j: (0, j))
    return pl.pallas_call(
        _mm_kernel if sel is None else _mm_sel_kernel,
        grid=(nb, t // tm, n // tn),
        in_specs=[a_spec, w_spec],
        out_specs=pl.BlockSpec((1, tm, tn), lambda b, i, j: (b, i, j)),
        out_shape=jax.ShapeDtypeStruct((nb, t, n), out_dtype),
        compiler_params=_cparams(("parallel", "parallel", "arbitrary")),
        name="matmul",
    )(a, w_arr)


def _mm_res_kernel(a_ref, w_ref, x_ref, gl_ref, gc_ref, o_ref, *, tm, ctx):
    row = pl.program_id(1) * tm + lax.broadcasted_iota(jnp.int32, (tm, 1), 0)
    g = jnp.where(row < ctx, gc_ref[0], gl_ref[0])
    o_ref[0] = x_ref[0] + g * _dot(a_ref[0], w_ref[...])


def _matmul_residual(a, w, xs, mod, ctx):
    nb, t, k = a.shape
    n = _wshape(w)[1]
    tm = _tile(t, 1056)
    tn = _tile(n, 512, 128)
    goff = 2 * n // tn
    w, w_spec = _weight(w, k, tn, lambda b, i, j: (0, j))
    return pl.pallas_call(
        functools.partial(_mm_res_kernel, tm=tm, ctx=ctx),
        grid=(nb, t // tm, n // tn),
        in_specs=[pl.BlockSpec((1, tm, k), lambda b, i, j: (b, i, 0)),
                  w_spec,
                  pl.BlockSpec((1, tm, tn), lambda b, i, j: (b, i, j)),
                  pl.BlockSpec((1, 1, tn), lambda b, i, j: (b, 0, goff + j)),
                  pl.BlockSpec((1, 1, tn), lambda b, i, j: (nb, 0, goff + j))],
        out_specs=pl.BlockSpec((1, tm, tn), lambda b, i, j: (b, i, j)),
        out_shape=jax.ShapeDtypeStruct((nb, t, n), F32),
        compiler_params=_cparams(("parallel", "parallel", "arbitrary")),
        name="matmul_residual",
    )(a, w, xs, mod, mod)


def _time_block_map(nblk, nctx_blk, reverse):
    if not reverse:
        return lambda i: i
    return lambda i: jnp.where(i < nctx_blk, nctx_blk - 1 - i, nblk - 1 - i + nctx_blk)


NLEVEL = CHUNK.bit_length() - 1
GLA_HEADS_PER_STEP = 8


def _gla_gates(z, llb, l1m):
    ls = jnp.minimum(z, 0.0) - jnp.log(1.0 + jnp.exp(-jnp.abs(z)))
    if llb is None:
        return ls
    c2 = l1m + ls
    return jnp.maximum(llb, c2) + jnp.log(1.0 + jnp.exp(-jnp.abs(llb - c2)))


def _gla_consts(reverse):
    r = lax.broadcasted_iota(jnp.int32, (CHUNK, CHUNK), 0)
    i = lax.broadcasted_iota(jnp.int32, (CHUNK, CHUNK), 1)
    tr = (CHUNK - 1 - r) if reverse else r
    ti = (CHUNK - 1 - i) if reverse else i
    masks = []
    for lvl in range(NLEVEL):
        same = (tr >> (lvl + 1)) == (ti >> (lvl + 1))
        pair = jnp.logical_and(jnp.logical_and(same, ((tr >> lvl) & 1) == 1), ((ti >> lvl) & 1) == 0)
        masks.append(jnp.where(pair, 1.0, 0.0))
    masks.append(jnp.where(r == i, 1.0, 0.0))
    return masks


def _boundary_rows(b, lvl, sub, reverse):
    grp, half = 1 << (lvl + 1), 1 << lvl
    row_of = (lambda tau: CHUNK - 1 - tau) if reverse else (lambda tau: tau)
    tiles = []
    for a in range(CHUNK // SUBLANES):
        taus = sorted(row_of(a * SUBLANES + u) for u in range(SUBLANES))
        picks = {}
        for tau in taus:
            src = row_of((tau // grp) * grp + half - 1)
            picks.setdefault(src, []).append(row_of(tau) - a * SUBLANES)
        tile = None
        for src, subs in picks.items():
            piece = jnp.broadcast_to(b[src:src + 1, :], (SUBLANES, b.shape[1]))
            if tile is None:
                tile = piece
            else:
                tile = jnp.where(jnp.logical_and(sub >= min(subs), sub <= max(subs)), piece, tile)
        tiles.append(tile)
    return jnp.concatenate(tiles, axis=0)


def _gla_intra(q, v, g, k, mask_ref, reverse):
    sub = lax.broadcasted_iota(jnp.int32, (SUBLANES, 1), 0)
    b = _cumsum_rows(g, reverse)
    end = 0 if reverse else CHUNK - 1
    b_end = b[end:end + 1, :]
    qt = q * jnp.exp(b)
    kbar = k * jnp.exp(b_end - b)
    a = mask_ref[NLEVEL] * jnp.sum(q * k, axis=-1, keepdims=True)
    k_prev = pltpu.roll(k, CHUNK - 1 if reverse else 1, 0)
    a = a + mask_ref[0] * jnp.sum(q * (1.0 - k) * k_prev, axis=-1, keepdims=True)
    qb, kb = q.astype(BF16), k.astype(BF16)
    for lvl in range(1, NLEVEL):
        e = jnp.exp(-jnp.abs(b - _boundary_rows(b, lvl, sub, reverse))).astype(BF16)
        a = a + mask_ref[lvl] * _dot(qb * e, kb * e, 1, 1)
    return _dot(a, v), qt.astype(BF16), _dot(v, kbar, 0, 0), jnp.exp(b_end)


def _gla_kernel(q_ref, v_ref, g_ref, o_ref, st_ref, mask_ref, *, tb, reverse, scale, hps):
    @pl.when(pl.program_id(2) == 0)
    def _():
        st_ref[...] = jnp.zeros_like(st_ref)
        for n, m in enumerate(_gla_consts(reverse)):
            mask_ref[n] = m

    consts = mask_ref
    nch = tb // CHUNK
    order = list(range(nch - 1, -1, -1) if reverse else range(nch))
    items = [(h, c) for h in range(hps) for c in order]

    def blk(ref, h, c):
        return ref[0, c * CHUNK:(c + 1) * CHUNK, h * A_DK:(h + 1) * A_DK]

    gs = [blk(g_ref, h, c) for h, c in items]
    intra = [_gla_intra(blk(q_ref, h, c) * scale, blk(v_ref, h, c), g, 1.0 - jnp.exp(g), consts, reverse)
             for (h, c), g in zip(items, gs)]
    for h in range(hps):
        st = st_ref[h]
        states = []
        for n, (hh, c) in enumerate(items):
            if hh == h:
                states.append((c, n, st))
                st = st * intra[n][3] + intra[n][2]
        st_ref[h] = st
        for c, n, st_in in states:
            o = intra[n][0] + _dot(intra[n][1], st_in, 1, 1)
            o_ref[0, c * CHUNK:(c + 1) * CHUNK, h * A_DK:(h + 1) * A_DK] = o.astype(o_ref.dtype)


def _gla(z, nh, ctx, reverse, tb=None):
    nb, t, _ = z.shape
    tb = tb or _tile(ctx, 256, CHUNK)
    assert t % tb == 0 and ctx % tb == 0
    nblk, ncb = t // tb, ctx // tb
    tmap = _time_block_map(nblk, ncb, reverse)
    hps = GLA_HEADS_PER_STEP
    assert nh % hps == 0
    ng = nh // hps
    zcol = (3 if reverse else 2) * ng
    wide = hps * A_DK

    def spec(col0):
        return pl.BlockSpec((1, tb, wide), lambda b, h, i: (b, tmap(i), col0 + h))

    return pl.pallas_call(
        functools.partial(_gla_kernel, tb=tb, reverse=reverse, scale=A_DK ** -0.5, hps=hps),
        grid=(nb, ng, nblk),
        in_specs=[spec(0), spec(ng), spec(zcol)],
        out_specs=pl.BlockSpec((1, tb, wide), lambda b, h, i: (b, tmap(i), h)),
        out_shape=jax.ShapeDtypeStruct((nb, t, nh * A_DK), BF16),
        scratch_shapes=[pltpu.VMEM((hps, A_DK, A_DK), F32), pltpu.VMEM((NLEVEL + 1, CHUNK, CHUNK), F32)],
        compiler_params=_cparams(("parallel", "parallel", "arbitrary")),
        name="gla_bwd" if reverse else "gla_fwd",
    )(z, z, z)


BF16_ROWS = 16


def _hgrn2_in_kernel(a_ref, w_ref, *refs, lo, hi):
    o_ref = refs[-1]
    bounds = [r[...] for r in refs[:-1]] or [None, None]
    j = pl.program_id(2)
    is_gate = jnp.logical_and(j >= lo, j < hi)

    @pl.when(is_gate)
    def _():
        o_ref[0] = _gla_gates(_dot(a_ref[0], w_ref[...]), *bounds)

    @pl.when(jnp.logical_not(is_gate))
    def _():
        o_ref[0] = _dot(a_ref[0], w_ref[...])


def _hgrn2_in(h, w_in, lb, width):
    nb, t, k = h.shape
    n = _wshape(w_in)[1]
    tm = _tile(t, 1056, BF16_ROWS)
    tn = _tile(width, 1024, 128)
    per = width // tn
    lo, hi = 2 * per, 4 * per
    par = pl.BlockSpec((1, tn), lambda b, i, j: (0, jnp.where(jnp.logical_and(j >= lo, j < hi), (j - lo) % per, 0)))
    bounds = [] if lb is None else [jnp.log(lb).reshape(1, width), jnp.log1p(-lb).reshape(1, width)]
    w_in, w_spec = _weight(w_in, k, tn, lambda b, i, j: (0, j))
    return pl.pallas_call(
        functools.partial(_hgrn2_in_kernel, lo=lo, hi=hi),
        grid=(nb, t // tm, n // tn),
        in_specs=[pl.BlockSpec((1, tm, k), lambda b, i, j: (b, i, 0)), w_spec] + [par] * len(bounds),
        out_specs=pl.BlockSpec((1, tm, tn), lambda b, i, j: (b, i, j)),
        out_shape=jax.ShapeDtypeStruct((nb, t, n), F32),
        compiler_params=_cparams(("parallel", "parallel", "arbitrary")),
        name="hgrn2_in",
    )(h, w_in, *bounds)


def _residual_out(lhs, w_ref, x_ref, gl_ref, gc_ref, o_ref, tm, ctx):
    k = pl.program_id(2)
    part = _dot(lhs, w_ref[...])

    @pl.when(k == 0)
    def _():
        o_ref[0] = part

    @pl.when(k > 0)
    def _():
        o_ref[0] += part

    @pl.when(k == pl.num_programs(2) - 1)
    def _():
        row = pl.program_id(1) * tm + lax.broadcasted_iota(jnp.int32, (tm, 1), 0)
        g = jnp.where(row < ctx, gc_ref[0], gl_ref[0])
        o_ref[0] = x_ref[0] + g * o_ref[0]


def _hgrn2_out_kernel(of_ref, ob_ref, g_ref, nw_ref, w_ref, x_ref, gl_ref, gc_ref, o_ref, *, tm, ctx):
    parts = []
    for h in range(of_ref.shape[2] // A_DK):
        sl = slice(h * A_DK, (h + 1) * A_DK)
        y = of_ref[0, :, sl].astype(F32) + ob_ref[0, :, sl].astype(F32)
        ms = jnp.mean(y * y, axis=-1, keepdims=True)
        yn = y * lax.rsqrt(ms + RMS_EPS) * nw_ref[...]
        parts.append((yn * _silu(g_ref[0, :, sl])).astype(BF16))
    _residual_out(jnp.concatenate(parts, axis=1), w_ref, x_ref, gl_ref, gc_ref, o_ref, tm, ctx)


OUT_TK = 1024


def _residual_specs(w_spec, tm, n, nb):
    return [w_spec,
            pl.BlockSpec((1, tm, n), lambda b, i, k: (b, i, 0)),
            pl.BlockSpec((1, 1, n), lambda b, i, k: (b, 0, 2)),
            pl.BlockSpec((1, 1, n), lambda b, i, k: (nb, 0, 2))]


def _hgrn2_out(o_f, o_b, z, onorm_w, w_out, xs, mod, ctx):
    nb, t, w = o_f.shape
    n = _wshape(w_out)[1]
    tm = _tile(t, 528, BF16_ROWS)
    tk = _tile(w, OUT_TK, A_DK)
    gate0 = 4 * (w // tk)
    sl = pl.BlockSpec((1, tm, tk), lambda b, i, k: (b, i, k))
    w_out, w_spec = _weight(w_out, tk, n, lambda b, i, k: (k, 0))
    return pl.pallas_call(
        functools.partial(_hgrn2_out_kernel, tm=tm, ctx=ctx),
        grid=(nb, t // tm, w // tk),
        in_specs=[sl, sl, pl.BlockSpec((1, tm, tk), lambda b, i, k: (b, i, gate0 + k)),
                  pl.BlockSpec((1, A_DK), lambda b, i, k: (0, 0))] + _residual_specs(w_spec, tm, n, nb),
        out_specs=pl.BlockSpec((1, tm, n), lambda b, i, k: (b, i, 0)),
        out_shape=jax.ShapeDtypeStruct((nb, t, n), F32),
        compiler_params=_cparams(("parallel", "parallel", "arbitrary")),
        name="hgrn2_out",
    )(o_f, o_b, z, onorm_w.reshape(1, A_DK), w_out, xs, mod, mod)


def _hgrn2_layer(h, w_in, lb, onorm_w, w_out, xs, mod, ctx, tb=None):
    width = _wshape(w_out)[0]
    z = _hgrn2_in(h, w_in, lb, width)
    o_f = _gla(z, width // A_DK, ctx, False, tb)
    o_b = _gla(z, width // A_DK, ctx, True, tb)
    return _hgrn2_out(o_f, o_b, z, onorm_w, w_out, xs, mod, ctx)


GRID_W = 64
ROPE_BASE = 10000.0
HEAD = 64
QB = 128
LANES = 128


def _rope_tables(t_all, ctx):
    quarter = HEAD // 4
    inv = ROPE_BASE ** (-jnp.arange(quarter, dtype=F32) / quarter)
    tl = jnp.arange(t_all - ctx)
    row = (tl // GRID_W).astype(F32)
    col = (tl % GRID_W).astype(F32)
    hdim = jnp.arange(LANES) % HEAD
    use_col = hdim >= 2 * quarter
    second = (hdim % (2 * quarter)) >= quarter
    pos = jnp.where(use_col[None, :], col[:, None], row[:, None])
    ang = pos * inv[hdim % quarter][None, :]
    cos = jnp.concatenate([jnp.ones((ctx, LANES), F32), jnp.cos(ang)], axis=0)
    sin = jnp.concatenate([jnp.zeros((ctx, LANES), F32),
                           jnp.where(second[None, :], jnp.sin(ang), -jnp.sin(ang))], axis=0)
    return cos, sin


def _rope(x, cos, sin, first):
    partner = jnp.where(first, pltpu.roll(x, LANES - HEAD // 4, 1), pltpu.roll(x, HEAD // 4, 1))
    return x * cos + partner * sin


def _swa_kernel(sink_ref, q_ref, *refs, ncb, seq, ngroups):
    (kc_ref, vc_ref, kp_ref, kq_ref, kn_ref, vp_ref, vq_ref, vn_ref,
     cq_ref, sq_ref, cp_ref, sp_ref, cn_ref, sn_ref, o_ref) = refs[-15:]
    gate_refs = refs[:-15]
    gate_tiles = gate_refs[0].shape[2] // LANES

    def gate_tile(col):
        c = col % gate_tiles
        return gate_refs[col // gate_tiles][0, :, c * LANES:(c + 1) * LANES]

    i = pl.program_id(1)
    lane = lax.broadcasted_iota(jnp.int32, (1, LANES), 1)
    first = (lane % (HEAD // 2)) < (HEAD // 4)
    left = lane < HEAD
    ntile = kc_ref.shape[2] // LANES

    def tile(x, c):
        return x[:, c * LANES:(c + 1) * LANES]

    kwin = [(kp_ref[0], cp_ref[...], sp_ref[...]), (kq_ref[0], cq_ref[...], sq_ref[...]),
            (kn_ref[0], cn_ref[...], sn_ref[...])]
    kt = [jnp.concatenate([tile(kc_ref[0], c)] + [_rope(tile(kk, c), cs, sn, first) for kk, cs, sn in kwin], axis=0)
          for c in range(ntile)]
    vt = [jnp.concatenate([tile(vc_ref[0], c), tile(vp_ref[0], c), tile(vq_ref[0], c), tile(vn_ref[0], c)], axis=0)
          for c in range(ntile)]
    nk = kt[0].shape[0]
    nctx = nk - 3 * QB

    r = lax.broadcasted_iota(jnp.int32, (QB, QB), 0)
    cidx = lax.broadcasted_iota(jnp.int32, (QB, QB), 1)
    qblk = i - ncb
    open_if = lambda cond: jnp.where(cond, 0.0, NEG_INF)
    b_prev = open_if(cidx >= r) + open_if(qblk >= 1)
    b_cur = jnp.zeros((QB, QB), F32) + open_if(qblk >= 0)
    b_next = open_if(cidx <= r) + open_if(jnp.logical_and(qblk >= 0, (qblk + 2) * QB <= seq))
    bias = jnp.concatenate([jnp.zeros((QB, nctx), F32), b_prev, b_cur, b_next], axis=1)

    in_even = lax.broadcasted_iota(jnp.int32, (2 * nk, LANES), 0) < nk
    in_left = lax.broadcasted_iota(jnp.int32, (2 * nk, LANES), 1) < HEAD
    ones2 = jnp.where(in_even == in_left, 1.0, 0.0).astype(BF16)

    scale = HEAD ** -0.5
    heads_per_group = q_ref.shape[2] // HEAD // ngroups
    pairs = heads_per_group // 2
    for g in range(ngroups):
        c, even = g // 2, g % 2 == 0
        own = left if even else jnp.logical_not(left)
        k_own = jnp.where(own, kt[c], 0.0)
        v_own = jnp.where(own, vt[c], 0.0)
        k_swp = pltpu.roll(k_own, HEAD, 1)
        v_swp = pltpu.roll(v_own, HEAD, 1)
        kk2 = jnp.concatenate([k_own, k_swp] if even else [k_swp, k_own], axis=0).astype(BF16)
        vv2 = jnp.concatenate([v_own, v_swp] if even else [v_swp, v_own], axis=0).astype(BF16)
        q8 = jnp.concatenate(
            [_rope(tile(q_ref[0], g * pairs + p), cq_ref[...], sq_ref[...], first) * scale for p in range(pairs)],
            axis=0)
        s8 = _dot(q8, kk2, 1, 1)
        probs, sinks = [], []
        for p in range(pairs):
            pe2, snk2 = [], []
            for e in range(2):
                sk = sink_ref[g * heads_per_group + 2 * p + e]
                s = s8[p * QB:(p + 1) * QB, e * nk:(e + 1) * nk] + bias
                m = jnp.maximum(jnp.max(s, axis=-1, keepdims=True), sk)
                pe2.append(jnp.exp((s - m).astype(BF16)))
                snk2.append(jnp.exp(sk - m))
            probs.append(jnp.concatenate(pe2, axis=1))
            sinks.append(jnp.where(left, snk2[0], snk2[1]))
        p8 = jnp.concatenate(probs, axis=0)
        o8 = _dot(p8, vv2)
        d8 = _dot(p8, ones2)
        for p in range(pairs):
            col = g * pairs + p
            o = o8[p * QB:(p + 1) * QB] / (d8[p * QB:(p + 1) * QB] + sinks[p])
            o_ref[0, :, col * LANES:(col + 1) * LANES] = (o * _silu(gate_tile(col))).astype(o_ref.dtype)


def _swa(z, sink, ctx, nkv):
    nb, t, ncol = z.shape
    kvw = nkv * HEAD
    w = (ncol - 2 * kvw) // 2
    assert t % QB == 0 and ctx % QB == 0 and w % kvw == 0
    nblk, ncb = t // QB, ctx // QB
    kcol, vcol = w // kvw, w // kvw + 1
    gw = math.gcd(w, w + 2 * kvw)
    assert gw % LANES == 0
    gates = [pl.BlockSpec((1, QB, gw), functools.partial(lambda b, i, c: (b, i, c), c=(w + 2 * kvw) // gw + n))
             for n in range(w // gw)]
    cos, sin = _rope_tables(t, ctx)
    prev = lambda i: jnp.maximum(i - 1, 0)
    nxt = lambda i: jnp.minimum(i + 1, nblk - 1)
    wide = lambda col: pl.BlockSpec((1, QB, w), lambda b, i: (b, i, col))
    kv = lambda col, f: pl.BlockSpec((1, QB, kvw), lambda b, i: (b, f(i), col))
    kvc = lambda col: pl.BlockSpec((1, ctx, kvw), lambda b, i: (b, 0, col))
    tab = lambda f: pl.BlockSpec((QB, LANES), lambda b, i: (f(i), 0))
    same = lambda i: i
    return pl.pallas_call(
        functools.partial(_swa_kernel, ncb=ncb, seq=t - ctx, ngroups=nkv),
        grid=(nb, nblk),
        in_specs=[pl.BlockSpec(memory_space=pltpu.SMEM), wide(0)] + gates + [kvc(kcol), kvc(vcol),
                  kv(kcol, prev), kv(kcol, same), kv(kcol, nxt), kv(vcol, prev), kv(vcol, same), kv(vcol, nxt),
                  tab(same), tab(same), tab(prev), tab(prev), tab(nxt), tab(nxt)],
        out_specs=wide(0),
        out_shape=jax.ShapeDtypeStruct((nb, t, w), BF16),
        compiler_params=_cparams(("parallel", "arbitrary")),
        name="swa",
    )(sink, z, *([z] * len(gates)), z, z, z, z, z, z, z, z, cos, sin, cos, sin, cos, sin)


C_GN_EPS = 64e-5
C_LORA_PAD = 128


def _prepc_kernel(x_ref, xp_ref, xn_ref, nw_ref, shl_ref, scl_ref, shc_ref, scc_ref, mu_ref, o_ref, scr,
                  *, tm, ctx, t_all):
    base = pl.program_id(1) * tm
    nw = nw_ref[...]
    mods = (scl_ref[0], shl_ref[0], scc_ref[0], shc_ref[0])
    row = base + lax.broadcasted_iota(jnp.int32, (tm, 1), 0)
    r8 = lax.broadcasted_iota(jnp.int32, (8, 1), 0)
    h = _normmod(x_ref[0], nw, row < ctx, *mods)
    scr[0:8, :] = _normmod(xp_ref[0], nw, (base - 8 + r8) < ctx, *mods)
    scr[8:tm + 8, :] = h
    scr[tm + 8:tm + 16, :] = _normmod(xn_ref[0], nw, (base + tm + r8) < ctx, *mods)
    has_prev = jnp.logical_and(row != 0, row != ctx)
    has_next = jnp.logical_and(row != ctx - 1, row != t_all - 1)
    xx = 0.5 * (jnp.where(has_prev, scr[7:tm + 7, :], 0.0) + jnp.where(has_next, scr[9:tm + 9, :], 0.0)) - h
    for n in range(o_ref.shape[0]):
        o_ref[n, 0] = (h + xx * mu_ref[n:n + 1, :]).astype(o_ref.dtype)


def _prepc(xs, nw, mod, mu, ctx):
    nb, t, d = xs.shape
    tm = _tile(t, 264)
    nmix = mu.shape[0]
    last8 = t // 8 - 1
    return pl.pallas_call(
        functools.partial(_prepc_kernel, tm=tm, ctx=ctx, t_all=t),
        grid=(nb, t // tm),
        in_specs=[pl.BlockSpec((1, tm, d), lambda b, i: (b, i, 0)),
                  pl.BlockSpec((1, 8, d), lambda b, i: (b, jnp.maximum(i * (tm // 8) - 1, 0), 0)),
                  pl.BlockSpec((1, 8, d), lambda b, i: (b, jnp.minimum((i + 1) * (tm // 8), last8), 0)),
                  pl.BlockSpec((1, d), lambda b, i: (0, 0))] + _mod_specs(d, nb) +
                 [pl.BlockSpec((nmix, d), lambda b, i: (0, 0))],
        out_specs=pl.BlockSpec((nmix, 1, tm, d), lambda b, i: (0, b, i, 0)),
        out_shape=jax.ShapeDtypeStruct((nmix, nb, t, d), BF16),
        scratch_shapes=[pltpu.VMEM((tm + 16, d), F32)],
        compiler_params=_cparams(("parallel", "parallel")),
        name="prep_rwkv",
    )(xs, xs, xs, nw.reshape(1, d), mod, mod, mod, mod, mu)


def _lora_kernel(xw_ref, xa_ref, w1_ref, a1_ref, w2_ref, a2_ref, w0_ref, a0_ref, lw_ref, ic_ref):
    t1 = jnp.tanh(_dot(xw_ref[0, 0], w1_ref[...]))
    t2 = _dot(xa_ref[0, 0], a1_ref[...])
    for d in range(2):
        sl = slice(d * C_LORA_PAD, (d + 1) * C_LORA_PAD)
        lw_ref[d, 0] = -math.exp(-0.5) * _sigmoid(w0_ref[d] + _dot(t1[:, sl], w2_ref[d]))
        ic_ref[d, 0] = _sigmoid(a0_ref[d] + _dot(t2[:, sl], a2_ref[d]))


def _pad_lora(w_in, w_out):
    r = w_in.shape[2]
    a = jnp.pad(w_in, ((0, 0), (0, 0), (0, C_LORA_PAD - r)))
    a = jnp.concatenate([a[0], a[1]], axis=1).astype(BF16)
    b = jnp.pad(w_out, ((0, 0), (0, C_LORA_PAD - r), (0, 0))).astype(BF16)
    return a, b


def _lora(mix, w0, w1, w2, a0, a1, a2):
    _, nb, t, d = mix.shape
    w = w0.shape[1]
    tm = _tile(t, 264)
    w1p, w2p = _pad_lora(w1, w2)
    a1p, a2p = _pad_lora(a1, a2)
    full = lambda shape: pl.BlockSpec(shape, lambda b, i: (0,) * len(shape))
    out = pl.BlockSpec((2, 1, tm, w), lambda b, i: (0, b, i, 0))
    return pl.pallas_call(
        _lora_kernel,
        grid=(nb, t // tm),
        in_specs=[pl.BlockSpec((1, 1, tm, d), lambda b, i: (4, b, i, 0)),
                  pl.BlockSpec((1, 1, tm, d), lambda b, i: (5, b, i, 0)),
                  full(w1p.shape), full(a1p.shape), full(w2p.shape), full(a2p.shape),
                  full((2, 1, w)), full((2, 1, w))],
        out_specs=[out, out],
        out_shape=[jax.ShapeDtypeStruct((2, nb, t, w), F32)] * 2,
        compiler_params=_cparams(("parallel", "parallel")),
        name="rwkv_lora",
    )(mix, mix, w1p, a1p, w2p, a2p, w0.reshape(2, 1, w), a0.reshape(2, 1, w))


RWKV_PAIRS_PER_STEP = 8


def _rwkv_consts(reverse):
    t2 = lax.broadcasted_iota(jnp.int32, (CHUNK, LANES), 0)
    s2 = lax.broadcasted_iota(jnp.int32, (CHUNK, LANES), 1) % CHUNK
    strict = (s2 > t2) if reverse else (s2 < t2)
    incl = (s2 >= t2) if reverse else (s2 <= t2)
    eye = jnp.where(t2 == s2, 1.0, 0.0)
    left = lax.broadcasted_iota(jnp.int32, (1, LANES), 1) < HEAD
    vi = lax.broadcasted_iota(jnp.int32, (LANES, LANES), 0) < HEAD
    ki = lax.broadcasted_iota(jnp.int32, (LANES, LANES), 1) < HEAD
    return strict, incl, eye, left, vi == ki


def _stack(x, left):
    x = x.astype(BF16)
    zero = jnp.zeros_like(x)
    return jnp.concatenate([jnp.where(left, x, zero), jnp.where(left, zero, x)], axis=0)


def _rwkv_prepare(r, k, v, lw, ic, kkw, kaw, rkw, consts, reverse):
    _, _, _, left, _ = consts
    stack = functools.partial(_stack, left=left)

    def segsum(x):
        sl = jnp.sum(jnp.where(left, x, 0.0), axis=-1, keepdims=True)
        sr = jnp.sum(jnp.where(left, 0.0, x), axis=-1, keepdims=True)
        return jnp.where(left, sl, sr)

    kx = k * kkw
    kk = kx / jnp.maximum(jnp.sqrt(segsum(kx * kx)), 1e-12)
    kd = k * (1.0 + (ic - 1.0) * kaw)
    b = kk * ic
    bonus = segsum(r * kd * rkw) * v
    ti = lax.broadcasted_iota(jnp.int32, (CHUNK, CHUNK), 0)
    si = lax.broadcasted_iota(jnp.int32, (CHUNK, CHUNK), 1)
    lam = _dot_sel(((si >= ti) if reverse else (si <= ti)).astype(BF16), lw)
    end = 0 if reverse else CHUNK - 1
    lam_c = lam[end:end + 1, :]
    einv = jnp.exp(-lam)
    ebar = jnp.exp(lam_c - lam)
    at = (-kk * jnp.exp(lam - lw)).astype(BF16)
    rt = (r * jnp.exp(lam)).astype(BF16)
    return dict(at=at, rt=rt, ar=jnp.concatenate([at, rt], axis=0),
                kb=jnp.concatenate([stack(kd * einv), stack(b * einv)], axis=0),
                v=v.astype(BF16), vst=stack(v), kbar=(kd * ebar).astype(BF16), bbar=(b * ebar).astype(BF16),
                dec=jnp.exp(lam_c), bonus=bonus)


def _inv_unit_lower(ns, eye, left):
    stack = functools.partial(_stack, left=left)
    ps = [eye + n for n in ns]
    ms = [_dot(n, stack(n)) for n in ns]
    for j in range(CHUNK.bit_length() - 4):
        both = [_dot(jnp.concatenate([m.astype(BF16), p.astype(BF16)], axis=0), stack(m)) for m, p in zip(ms, ps)]
        ps = [p + b[CHUNK:] for p, b in zip(ps, both)]
        ms = [b[:CHUNK] for b in both]
    ps = [p + _dot(p, stack(m)) for p, m in zip(ps, ms)]
    out = []
    for n, p in zip(ns, ps):
        tb = p.astype(BF16)
        nh, nl = _split2(n)
        res = _dot(jnp.concatenate([nh, nl], axis=0), stack(tb))
        e = eye - tb.astype(F32) + (res[:CHUNK] + res[CHUNK:])
        out.append(tb.astype(F32) + _dot(tb, stack(e)))
    return out


def _rwkv_kernel(r_ref, k_ref, v_ref, lw_ref, ic_ref, kk_ref, ka_ref, rk_ref, y_ref, bon_ref, ht_ref,
                 *, tb, reverse, pps):
    @pl.when(pl.program_id(2) == 0)
    def _():
        ht_ref[...] = jnp.zeros_like(ht_ref)

    consts = _rwkv_consts(reverse)
    strict, incl, eye, left, same_head = consts
    stack = functools.partial(_stack, left=left)
    nch = tb // CHUNK
    order = list(range(nch - 1, -1, -1) if reverse else range(nch))
    items = [(p, c) for p in range(pps) for c in order]

    def blk(ref, p, c):
        return ref[0, c * CHUNK:(c + 1) * CHUNK, p * LANES:(p + 1) * LANES]

    def blkd(ref, p, c):
        return ref[0, 0, c * CHUNK:(c + 1) * CHUNK, p * LANES:(p + 1) * LANES]

    def par(ref, p):
        return ref[:, p * LANES:(p + 1) * LANES]

    pre = [_rwkv_prepare(blk(r_ref, p, c), blk(k_ref, p, c), blk(v_ref, p, c), blkd(lw_ref, p, c),
                         blkd(ic_ref, p, c), par(kk_ref, p), par(ka_ref, p), par(rk_ref, p), consts, reverse)
           for p, c in items]
    gs = [_dot(x["ar"], x["kb"], 1, 1) for x in pre]
    a_ak = [jnp.where(strict, g[:CHUNK, :LANES], 0.0).astype(BF16) for g in gs]
    a_rk = [jnp.where(incl, g[CHUNK:, :LANES], 0.0).astype(BF16) for g in gs]
    a_rb = [jnp.where(incl, g[CHUNK:, LANES:], 0.0).astype(BF16) for g in gs]
    tinv = _inv_unit_lower([jnp.where(strict, g[:CHUNK, LANES:], 0.0) for g in gs], eye, left)
    av = [_dot(jnp.concatenate([ak, rk], axis=0), x["vst"]) for ak, rk, x in zip(a_ak, a_rk, pre)]
    wu = [_dot(t, jnp.concatenate([stack(x["at"]), stack(a[:CHUNK])], axis=1)).astype(BF16)
          for t, x, a in zip(tinv, pre, av)]
    ry = [_dot(rb, jnp.concatenate([stack(w[:, :LANES]), stack(w[:, LANES:])], axis=1)) for rb, w in zip(a_rb, wu)]
    rw = [(x["rt"].astype(F32) + y[:, :LANES]).astype(BF16) for x, y in zip(pre, ry)]
    y0 = [a[CHUNK:] + y[:, LANES:] for a, y in zip(av, ry)]
    ft = [jnp.where(same_head, _dot(jnp.concatenate([x["v"], w[:, LANES:]], axis=0),
                                    jnp.concatenate([x["kbar"], x["bbar"]], axis=0), 0, 0), 0.0)
          for x, w in zip(pre, wu)]
    gm = [jnp.where(same_head, _dot(x["bbar"], w[:, :LANES], 0, 0), 0.0).astype(BF16) for x, w in zip(pre, wu)]

    hts = [ht_ref[p] for p in range(pps)]
    for step in range(nch):
        for p in range(pps):
            n = p * nch + step
            c = items[n][1]
            ht = hts[p]
            htb = ht.astype(BF16)
            y = y0[n] + _dot(rw[n], htb, 1, 1)
            y_ref[0, c * CHUNK:(c + 1) * CHUNK, p * LANES:(p + 1) * LANES] = y.astype(y_ref.dtype)
            bon_ref[0, c * CHUNK:(c + 1) * CHUNK, p * LANES:(p + 1) * LANES] = pre[n]["bonus"].astype(bon_ref.dtype)
            hts[p] = ht * pre[n]["dec"] + ft[n] + _dot(htb, gm[n], 1, 1)
    for p in range(pps):
        ht_ref[p] = hts[p]


def _rwkv_scan(r, k, v, lw, ic, k_k, k_a, r_k, ctx, reverse, tb=None):
    nb, t, w = r.shape
    tb = tb or _tile(ctx, 256, CHUNK)
    assert t % tb == 0 and ctx % tb == 0
    nblk, ncb = t // tb, ctx // tb
    tmap = _time_block_map(nblk, ncb, reverse)
    d = 1 if reverse else 0
    pps = RWKV_PAIRS_PER_STEP
    wide = pps * LANES
    assert w % wide == 0
    blk = pl.BlockSpec((1, tb, wide), lambda b, p, i: (b, tmap(i), p))
    blkd = pl.BlockSpec((1, 1, tb, wide), lambda b, p, i: (d, b, tmap(i), p))
    par = pl.BlockSpec((1, wide), lambda b, p, i: (0, p))
    return pl.pallas_call(
        functools.partial(_rwkv_kernel, tb=tb, reverse=reverse, pps=pps),
        grid=(nb, w // wide, nblk),
        in_specs=[blk, blk, blk, blkd, blkd, par, par, par],
        out_specs=[blk, blk],
        out_shape=[jax.ShapeDtypeStruct((nb, t, w), BF16)] * 2,
        scratch_shapes=[pltpu.VMEM((pps, LANES, LANES), F32)],
        compiler_params=_cparams(("parallel", "parallel", "arbitrary")),
        name="rwkv_bwd" if reverse else "rwkv_fwd",
    )(r, k, v, lw, ic, k_k.reshape(1, w), k_a.reshape(1, w), r_k.reshape(1, w))


def _rwkv_out_kernel(yf_ref, yb_ref, bf_ref, bb_ref, g_ref, lnw_ref, lnb_ref, w_ref, x_ref, gl_ref, gc_ref, o_ref,
                     *, tm, ctx):
    left = lax.broadcasted_iota(jnp.int32, (1, LANES), 1) < HEAD

    def segmean(x):
        sl = jnp.sum(jnp.where(left, x, 0.0), axis=-1, keepdims=True)
        sr = jnp.sum(jnp.where(left, 0.0, x), axis=-1, keepdims=True)
        return jnp.where(left, sl, sr) * (1.0 / HEAD)

    parts = []
    for p in range(yf_ref.shape[2] // LANES):
        sl = slice(p * LANES, (p + 1) * LANES)
        y = yf_ref[0, :, sl].astype(F32) + yb_ref[0, :, sl].astype(F32)
        dlt = y - segmean(y)
        zn = dlt * lax.rsqrt(segmean(dlt * dlt) + C_GN_EPS)
        o = (zn * lnw_ref[:, sl] + lnb_ref[:, sl]
             + bf_ref[0, :, sl].astype(F32) + bb_ref[0, :, sl].astype(F32))
        parts.append((o * _silu(g_ref[0, :, sl])).astype(BF16))
    _residual_out(jnp.concatenate(parts, axis=1), w_ref, x_ref, gl_ref, gc_ref, o_ref, tm, ctx)


def _rwkv_out(y_f, y_b, bon_f, bon_b, gate, ln_w, ln_b, w_out, xs, mod, ctx):
    nb, t, w = y_f.shape
    n = _wshape(w_out)[1]
    tm = _tile(t, 528, BF16_ROWS)
    tk = _tile(w, OUT_TK, LANES)
    sl = pl.BlockSpec((1, tm, tk), lambda b, i, k: (b, i, k))
    par = pl.BlockSpec((1, tk), lambda b, i, k: (0, k))
    w_out, w_spec = _weight(w_out, tk, n, lambda b, i, k: (k, 0))
    return pl.pallas_call(
        functools.partial(_rwkv_out_kernel, tm=tm, ctx=ctx),
        grid=(nb, t // tm, w // tk),
        in_specs=[sl] * 5 + [par, par] + _residual_specs(w_spec, tm, n, nb),
        out_specs=pl.BlockSpec((1, tm, n), lambda b, i, k: (b, i, 0)),
        out_shape=jax.ShapeDtypeStruct((nb, t, n), F32),
        compiler_params=_cparams(("parallel", "parallel", "arbitrary")),
        name="rwkv_out",
    )(y_f, y_b, bon_f, bon_b, gate, ln_w.reshape(1, w), ln_b.reshape(1, w), w_out, xs, mod, mod)


def _rwkv_layer(mix, w_in, w0, w1, w2, a0, a1, a2, k_k, k_a, r_k, ln_w, ln_b, w_out, xs, mod, ctx, tb=None):
    w_arr, w_idx = w_in if isinstance(w_in, tuple) else (w_in, ())
    r, k, v, gate = (_matmul(mix, (w_arr, tuple(w_idx) + (n,)), sel=n) for n in range(4))
    lw, ic = _lora(mix, w0, w1, w2, a0, a1, a2)
    y_f, bon_f = _rwkv_scan(r, k, v, lw, ic, k_k, k_a, r_k, ctx, False, tb)
    y_b, bon_b = _rwkv_scan(r, k, v, lw, ic, k_k, k_a, r_k, ctx, True, tb)
    return _rwkv_out(y_f, y_b, bon_f, bon_b, gate, ln_w, ln_b, w_out, xs, mod, ctx)


def _swa_mix(h, w_in, sink, ctx, width):
    kv2 = _wshape(w_in)[1] - 2 * width
    return _swa(_matmul(h, w_in), sink, ctx, kv2 // 2 // HEAD)


def _final_norm_kernel(x_ref, w_ref, o_ref):
    x = x_ref[0]
    ms = jnp.mean(x * x, axis=-1, keepdims=True)
    o_ref[0] = x * lax.rsqrt(ms + RMS_EPS) * w_ref[...]


def _final_norm(xs, w, ctx):
    nb, t, d = xs.shape
    seq = t - ctx
    tm = _tile(math.gcd(ctx, seq), 1024)
    off = ctx // tm
    return pl.pallas_call(
        _final_norm_kernel,
        grid=(nb, seq // tm),
        in_specs=[pl.BlockSpec((1, tm, d), lambda b, i: (b, i + off, 0)),
                  pl.BlockSpec((1, d), lambda b, i: (0, 0))],
        out_specs=pl.BlockSpec((1, tm, d), lambda b, i: (b, i, 0)),
        out_shape=jax.ShapeDtypeStruct((nb, seq, d), F32),
        compiler_params=_cparams(("parallel", "parallel")),
        name="final_norm",
    )(xs, w.reshape(1, d))


def kernel(x, c, ctx, c_ctx, norm_w, mod_w, mod_b, a_w_in, a_lb_raw, a_onorm_w, a_w_out, b_w_in, b_sink, b_w_out,
           c_mu, c_w_in, c_w0, c_w1, c_w2, c_a0, c_a1, c_a2, c_k_k, c_k_a, c_r_k, c_ln_w, c_ln_b, c_w_out,
           final_norm_w):
    nb, seq, d = x.shape
    nctx = ctx.shape[1]
    depth = norm_w.shape[0]
    xs = jnp.concatenate([ctx, x], axis=1)
    cvec = jnp.zeros((8, d), F32).at[:nb].set(c).at[nb].set(c_ctx)
    mod_all = _modulation(cvec, mod_w, mod_b)
    lb_all = jnp.cumsum(jax.nn.softmax(a_lb_raw.astype(F32), axis=0), axis=0)
    lb_all = lb_all - lb_all[0]
    a_w_in, a_w_out, b_w_in, b_w_out, c_w_in, c_w_out = (
        w.astype(BF16) for w in (a_w_in, a_w_out, b_w_in, b_w_out, c_w_in, c_w_out))
    for i in range(depth):
        j, kind = i // 3, i % 3
        mod = mod_all[i].reshape(8, 1, 3 * d)
        if kind == 0:
            h = _prep(xs, norm_w[i], mod, nctx)
            xs = _hgrn2_layer(h, (a_w_in, (j,)), lb_all[j] if j else None, a_onorm_w[j], (a_w_out, (j,)),
                              xs, mod, nctx)
        elif kind == 1:
            h = _prep(xs, norm_w[i], mod, nctx)
            m = _swa_mix(h, (b_w_in, (j,)), b_sink[j], nctx, b_w_out.shape[1])
            xs = _matmul_residual(m, (b_w_out, (j,)), xs, mod, nctx)
        else:
            mix = _prepc(xs, norm_w[i], mod, c_mu[j], nctx)
            xs = _rwkv_layer(mix, (c_w_in, (j,)), c_w0[j], c_w1[j], c_w2[j], c_a0[j], c_a1[j], c_a2[j],
                             c_k_k[j], c_k_a[j], c_r_k[j], c_ln_w[j], c_ln_b[j], (c_w_out, (j,)),
                             xs, mod, nctx)
    return _final_norm(xs, final_norm_w, nctx)
```
